```python
import math
import jax
import jax.numpy as jnp
from jax import lax
import numpy as np

D_MODEL = 4096
BATCH = 4
SEQ = 4096
DEPTH = 2

EPS = 1e-6
PLE_DIM = 256
Q_BLOCK = 128
HEADS = D_MODEL // 512
N_BRANCH = 4
BRANCH_W = HEADS * 128

DA_HEADS = HEADS
DA_QK_DIM = 64
DA_V_DIM = 128
MLA_HEADS = HEADS
MLA_Q_RANK = 1024
MLA_KV_RANK = 512
MLA_NOPE = 128
MLA_ROPE = 64
MLA_V = 128
ROPE_THETA = 10000.0
DIL_GROUPS = ((128, 1), (512, 4), (2048, 16))
DIL_HEADS = HEADS
DIL_DIM = 128
GDN_HEADS = HEADS
GDN_DK = 128
GDN_DV = 128
CONV_W = 4
GDN_CHUNK = 64
N_BUCKETS = 32
MAX_DIST = 2048
N_BIAS_HEADS = DA_HEADS + len(DIL_GROUPS) * DIL_HEADS
D_FF = ((8 * D_MODEL + 3 * 256 - 1) // (3 * 256)) * 256

DA_COLS = DA_HEADS * (4 * DA_QK_DIM + DA_V_DIM)
MLA_COLS = MLA_Q_RANK + MLA_KV_RANK + MLA_ROPE
DIL_COLS = len(DIL_GROUPS) * 3 * DIL_HEADS * DIL_DIM
GDN_COLS = GDN_HEADS * (2 * GDN_DK + 2 * GDN_DV + 2)
IN_COLS = DA_COLS + MLA_COLS + DIL_COLS + GDN_COLS
IN_SPLITS = (DA_COLS, DA_COLS + MLA_COLS, DA_COLS + MLA_COLS + DIL_COLS)

kernel_name = "hybrid_gated_parallel_mixers"


def rmsnorm(x, g):
    xf = x.astype(jnp.float32)
    y = xf * lax.rsqrt(jnp.mean(xf * xf, axis=-1, keepdims=True) + EPS)
    return (y * g.astype(jnp.float32)).astype(x.dtype)


def l2norm(x):
    xf = x.astype(jnp.float32)
    return xf * lax.rsqrt(jnp.sum(xf * xf, axis=-1, keepdims=True) + EPS)


def rel_bucket(dist):
    max_exact = N_BUCKETS // 2
    d = jnp.maximum(dist, 0)
    large = max_exact + (jnp.log(jnp.maximum(d, 1).astype(jnp.float32) / max_exact)
                         / math.log(MAX_DIST / max_exact) * (N_BUCKETS - max_exact)).astype(jnp.int32)
    large = jnp.minimum(large, N_BUCKETS - 1)
    return jnp.where(d < max_exact, d, large)


def rope(x, pos):
    half = x.shape[-1] // 2
    inv = ROPE_THETA ** (-jnp.arange(half, dtype=jnp.float32) / half)
    ang = pos.astype(jnp.float32)[:, None] * inv[None, :]
    cos = jnp.cos(ang)[None, :, None, :]
    sin = jnp.sin(ang)[None, :, None, :]
    xf = x.astype(jnp.float32)
    x1, x2 = xf[..., :half], xf[..., half:]
    return jnp.concatenate([x1 * cos - x2 * sin, x2 * cos + x1 * sin], axis=-1).astype(x.dtype)


def diff_attention(z, lam, lam_init, q_g, k_g, o_g, table):
    b, s, _ = z.shape
    nqk = DA_HEADS * 2 * DA_QK_DIM
    q = rmsnorm(z[..., :nqk].reshape(b, s, DA_HEADS, 2, DA_QK_DIM), q_g)
    k = rmsnorm(z[..., nqk:2 * nqk].reshape(b, s, DA_HEADS, 2, DA_QK_DIM), k_g)
    v = z[..., 2 * nqk:].reshape(b, s, DA_HEADS, DA_V_DIM)
    scale = DA_QK_DIM ** -0.5
    kpos = jnp.arange(s)

    def block(i):
        start = i * Q_BLOCK
        qb = lax.dynamic_slice_in_dim(q, start, Q_BLOCK, axis=1)
        dist = (start + jnp.arange(Q_BLOCK))[:, None] - kpos[None, :]
        bias = jnp.transpose(table[rel_bucket(dist)], (2, 0, 1)).astype(jnp.float32)
        logits = jnp.einsum('bqhmd,bkhmd->bmhqk', qb, k).astype(jnp.float32) * scale + bias
        logits = jnp.where(dist >= 0, logits, -jnp.inf)
        probs = jax.nn.softmax(logits, axis=-1)
        attn = probs[:, 0] - lam * probs[:, 1]
        return jnp.einsum('bhqk,bkhd->bqhd', attn.astype(v.dtype), v)

    o = lax.map(block, jnp.arange(s // Q_BLOCK))
    o = jnp.moveaxis(o, 0, 1).reshape(b, s, DA_HEADS, DA_V_DIM)
    o = rmsnorm(o, o_g) * (1.0 - lam_init)
    return o.reshape(b, s, DA_HEADS * DA_V_DIM)


def mla_attention(z, pos, w_uq, w_ukv, cq_g, ckv_g, qn_g, kn_g, qr_g, kr_g):
    b, s, _ = z.shape
    c_q = rmsnorm(z[..., :MLA_Q_RANK], cq_g)
    c_kv = rmsnorm(z[..., MLA_Q_RANK:MLA_Q_RANK + MLA_KV_RANK], ckv_g)
    k_r = z[..., MLA_Q_RANK + MLA_KV_RANK:]
    q = (c_q @ w_uq).reshape(b, s, MLA_HEADS, MLA_NOPE + MLA_ROPE)
    kv = (c_kv @ w_ukv).reshape(b, s, MLA_HEADS, MLA_NOPE + MLA_V)
    q_n = rmsnorm(q[..., :MLA_NOPE], qn_g)
    q_r = rope(rmsnorm(q[..., MLA_NOPE:], qr_g), pos)
    k_n = rmsnorm(kv[..., :MLA_NOPE], kn_g)
    v = kv[..., MLA_NOPE:]
    k_r = rope(rmsnorm(k_r, kr_g)[:, :, None, :], pos)[:, :, 0]
    scale = (MLA_NOPE + MLA_ROPE) ** -0.5
    kpos = jnp.arange(s)

    def block(i):
        start = i * Q_BLOCK
        qn_b = lax.dynamic_slice_in_dim(q_n, start, Q_BLOCK, axis=1)
        qr_b = lax.dynamic_slice_in_dim(q_r, start, Q_BLOCK, axis=1)
        logits = (jnp.einsum('bqhd,bkhd->bhqk', qn_b, k_n)
                  + jnp.einsum('bqhd,bkd->bhqk', qr_b, k_r)).astype(jnp.float32) * scale
        causal = (start + jnp.arange(Q_BLOCK))[:, None] >= kpos[None, :]
        probs = jax.nn.softmax(jnp.where(causal, logits, -jnp.inf), axis=-1)
        return jnp.einsum('bhqk,bkhd->bqhd', probs.astype(v.dtype), v)

    o = lax.map(block, jnp.arange(s // Q_BLOCK))
    return jnp.moveaxis(o, 0, 1).reshape(b, s, MLA_HEADS * MLA_V)


def dilated_group(q, k, v, window, dil, table):
    b, s, h, d = q.shape
    steps = window // dil
    length = s // dil
    lp = -(-length // Q_BLOCK) * Q_BLOCK
    nb = lp // Q_BLOCK

    def to_sub(t):
        t = t.reshape(b, length, dil, h, d).transpose(0, 2, 1, 3, 4)
        return jnp.pad(t, ((0, 0), (0, 0), (0, lp - length), (0, 0), (0, 0)))

    def band(t):
        tb = t.reshape(b, dil, nb, Q_BLOCK, h, d)
        prev = jnp.pad(tb, ((0, 0), (0, 0), (1, 0), (0, 0), (0, 0), (0, 0)))[:, :, :-1]
        return jnp.concatenate([prev, tb], axis=3)

    qb = to_sub(q).reshape(b, dil, nb, Q_BLOCK, h, d)
    kb = band(to_sub(k))
    vb = band(to_sub(v))
    qi = jnp.arange(Q_BLOCK)[:, None]
    kj = jnp.arange(2 * Q_BLOCK)[None, :]
    step = qi + Q_BLOCK - kj
    key_sub = jnp.arange(nb)[:, None, None] * Q_BLOCK + kj[None] - Q_BLOCK
    valid = (step >= 0) & (step <= steps) & (key_sub >= 0)
    bias = jnp.transpose(table[rel_bucket(step * dil)], (2, 0, 1)).astype(jnp.float32)
    logits = jnp.einsum('brnqhd,brnkhd->brnhqk', qb, kb).astype(jnp.float32) * (d ** -0.5) + bias
    logits = jnp.where(valid[None, None, :, None], logits, -jnp.inf)
    m = jnp.max(logits, axis=-1, keepdims=True)
    e = jnp.exp(logits - m)
    den = jnp.sum(e, axis=-1, keepdims=True)
    o = jnp.einsum('brnhqk,brnkhd->brnqhd', (e / den).astype(v.dtype), vb)
    lse = (m + jnp.log(den))[..., 0]
    o = o.reshape(b, dil, lp, h, d)[:, :, :length].transpose(0, 2, 1, 3, 4).reshape(b, s, h, d)
    lse = lse.transpose(0, 1, 2, 4, 3).reshape(b, dil, lp, h)[:, :, :length]
    lse = lse.transpose(0, 2, 1, 3).reshape(b, s, h)
    return o, lse


def dilated_attention(z, q_g, k_g, table):
    b, s, _ = z.shape
    zc = z.reshape(b, s, len(DIL_GROUPS), 3, DIL_HEADS, DIL_DIM)
    q = rmsnorm(zc[:, :, :, 0], q_g)
    k = rmsnorm(zc[:, :, :, 1], k_g)
    v = zc[:, :, :, 2]
    outs, lses = [], []
    for gi, (window, dil) in enumerate(DIL_GROUPS):
        lo = DA_HEADS + gi * DIL_HEADS
        o_g, lse_g = dilated_group(q[:, :, gi], k[:, :, gi], v[:, :, gi], window, dil,
                                   table[:, lo:lo + DIL_HEADS])
        outs.append(o_g)
        lses.append(lse_g)
    w = jax.nn.softmax(jnp.stack(lses, axis=0), axis=0)
    o = sum(w[gi][..., None] * outs[gi].astype(jnp.float32) for gi in range(len(DIL_GROUPS)))
    return o.astype(z.dtype).reshape(b, s, DIL_HEADS * DIL_DIM)


def causal_conv(t, w):
    c = t.shape[-1]
    return lax.conv_general_dilated(t, w[:, None, :].astype(t.dtype), window_strides=(1,),
                                    padding=[(CONV_W - 1, 0)],
                                    dimension_numbers=('NWC', 'WIO', 'NWC'),
                                    feature_group_count=c)


def chunk_gated_delta_rule(q, k, v, g, beta):
    b, s, h, dk = q.shape
    dv = v.shape[-1]
    c = GDN_CHUNK
    n = s // c

    def chunks(t):
        return t.astype(jnp.float32).reshape(b, n, c, h, -1).transpose(0, 3, 1, 2, 4)

    q, k, v = chunks(q), chunks(k), chunks(v)
    beta = beta.astype(jnp.float32).reshape(b, n, c, h).transpose(0, 3, 1, 2)
    g = jnp.cumsum(g.astype(jnp.float32).reshape(b, n, c, h).transpose(0, 3, 1, 2), axis=-1)
    tril = jnp.tril(jnp.ones((c, c), dtype=bool))
    strict = jnp.tril(jnp.ones((c, c), dtype=bool), -1)
    decay = jnp.exp(jnp.where(tril, g[..., :, None] - g[..., None, :], -jnp.inf))
    kb = k * beta[..., None]
    vb = v * beta[..., None]
    lower = jnp.where(strict, jnp.einsum('bhncd,bhnkd->bhnck', kb, k) * decay, 0.0)
    eye = jnp.eye(c, dtype=jnp.float32)
    tmat = lax.linalg.triangular_solve(eye + lower, jnp.broadcast_to(eye, lower.shape),
                                       left_side=True, lower=True, unit_diagonal=True)
    u = jnp.einsum('bhnck,bhnkv->bhncv', tmat, vb)
    w = jnp.einsum('bhnck,bhnkd->bhncd', tmat, kb * jnp.exp(g)[..., None])
    qg = q * jnp.exp(g)[..., None]
    kdec = k * jnp.exp(g[..., -1:] - g)[..., None]
    intra = jnp.einsum('bhncd,bhnkd->bhnck', q, k) * decay
    glast = g[..., -1]

    def step(state, xs):
        qg_c, kd_c, u_c, w_c, a_c, gl_c = xs
        v_new = u_c - jnp.einsum('bhck,bhkv->bhcv', w_c, state)
        o_c = jnp.einsum('bhck,bhkv->bhcv', qg_c, state) + jnp.einsum('bhcj,bhjv->bhcv', a_c, v_new)
        state = state * jnp.exp(gl_c)[..., None, None] + jnp.einsum('bhck,bhcv->bhkv', kd_c, v_new)
        return state, o_c

    xs = tuple(jnp.moveaxis(t, 2, 0) for t in (qg, kdec, u, w, intra, glast))
    state0 = jnp.zeros((b, h, dk, dv), jnp.float32)
    _, o = lax.scan(step, state0, xs)
    return o.transpose(1, 0, 3, 2, 4).reshape(b, s, h, dv)


def gated_deltanet(z, conv_w, a_log, dt_bias, o_g):
    b, s, _ = z.shape
    nq = GDN_HEADS * GDN_DK
    nv = GDN_HEADS * GDN_DV
    nqkv = 2 * nq + nv
    qkv = jax.nn.silu(causal_conv(z[..., :nqkv], conv_w))
    a = z[..., nqkv:nqkv + GDN_HEADS]
    bt = z[..., nqkv + GDN_HEADS:nqkv + 2 * GDN_HEADS]
    gate = z[..., nqkv + 2 * GDN_HEADS:].reshape(b, s, GDN_HEADS, GDN_DV)
    q = l2norm(qkv[..., :nq].reshape(b, s, GDN_HEADS, GDN_DK)) * (GDN_DK ** -0.5)
    k = l2norm(qkv[..., nq:2 * nq].reshape(b, s, GDN_HEADS, GDN_DK))
    v = qkv[..., 2 * nq:].reshape(b, s, GDN_HEADS, GDN_DV)
    beta = jax.nn.sigmoid(bt.astype(jnp.float32))
    g = -jnp.exp(a_log.astype(jnp.float32)) * jax.nn.softplus(a.astype(jnp.float32) + dt_bias.astype(jnp.float32))
    o = chunk_gated_delta_rule(q, k, v, g, beta).astype(z.dtype)
    o = rmsnorm(o, o_g) * jax.nn.silu(gate)
    return o.reshape(b, s, nv)


def setup_inputs(seed: int = 0) -> dict:
    key = jax.random.key(seed)
    keys = jax.random.split(key, 40)
    counter = [0]

    def nxt():
        kk = keys[counter[0]]
        counter[0] += 1
        return kk

    def nrm(shape, scale):
        return jax.random.normal(nxt(), shape, jnp.float32) * scale

    def gain(*shape):
        return 1.0 + nrm(shape, 0.02)

    x = nrm((BATCH, SEQ, D_MODEL), 1.0)
    p = nrm((DEPTH, BATCH, SEQ, PLE_DIM), 1.0)
    rel_bias = nrm((N_BUCKETS, N_BIAS_HEADS), 0.1)
    norm_mix_g = gain(DEPTH, D_MODEL)
    w_in = nrm((DEPTH, D_MODEL, IN_COLS), D_MODEL ** -0.5)
    da_lambda = nrm((DEPTH, 4, DA_QK_DIM), 0.1)
    da_q_g = gain(DEPTH, DA_QK_DIM)
    da_k_g = gain(DEPTH, DA_QK_DIM)
    da_o_g = gain(DEPTH, DA_V_DIM)
    mla_w_uq = nrm((DEPTH, MLA_Q_RANK, MLA_HEADS * (MLA_NOPE + MLA_ROPE)), MLA_Q_RANK ** -0.5)
    mla_w_ukv = nrm((DEPTH, MLA_KV_RANK, MLA_HEADS * (MLA_NOPE + MLA_V)), MLA_KV_RANK ** -0.5)
    mla_cq_g = gain(DEPTH, MLA_Q_RANK)
    mla_ckv_g = gain(DEPTH, MLA_KV_RANK)
    mla_qn_g = gain(DEPTH, MLA_NOPE)
    mla_kn_g = gain(DEPTH, MLA_NOPE)
    mla_qr_g = gain(DEPTH, MLA_ROPE)
    mla_kr_g = gain(DEPTH, MLA_ROPE)
    dil_q_g = gain(DEPTH, DIL_DIM)
    dil_k_g = gain(DEPTH, DIL_DIM)
    gdn_conv_w = nrm((DEPTH, CONV_W, GDN_HEADS * (2 * GDN_DK + GDN_DV)), CONV_W ** -0.5)
    gdn_a_log = jnp.log(jax.random.uniform(nxt(), (DEPTH, GDN_HEADS), jnp.float32, 1.0, 16.0))
    dt = jnp.exp(jax.random.uniform(nxt(), (DEPTH, GDN_HEADS), jnp.float32,
                                    math.log(1e-3), math.log(1e-1)))
    gdn_dt_bias = dt + jnp.log(-jnp.expm1(-dt))
    gdn_o_g = gain(DEPTH, GDN_DV)
    w_bgate = nrm((DEPTH, N_BRANCH, D_MODEL, D_MODEL), D_MODEL ** -0.5)
    b_bgate = nrm((DEPTH, N_BRANCH, D_MODEL), 0.1)
    w_branch = nrm((DEPTH, N_BRANCH, BRANCH_W, D_MODEL), BRANCH_W ** -0.5)
    w_out = nrm((DEPTH, D_MODEL, D_MODEL), D_MODEL ** -0.5)
    norm_ffn_g = gain(DEPTH, D_MODEL)
    w_ffn_in = nrm((DEPTH, D_MODEL, 2 * D_FF), D_MODEL ** -0.5)
    w_ffn_out = nrm((DEPTH, D_FF, D_MODEL), D_FF ** -0.5)
    norm_ple_g = gain(DEPTH, D_MODEL)
    w_ple = nrm((DEPTH, PLE_DIM, D_MODEL), PLE_DIM ** -0.5)
    w_ple_gate = nrm((DEPTH, D_MODEL, D_MODEL), D_MODEL ** -0.5)
    return {"x": x, "p": p, "rel_bias": rel_bias, "norm_mix_g": norm_mix_g, "w_in": w_in,
            "da_lambda": da_lambda, "da_q_g": da_q_g, "da_k_g": da_k_g, "da_o_g": da_o_g,
            "mla_w_uq": mla_w_uq, "mla_w_ukv": mla_w_ukv, "mla_cq_g": mla_cq_g, "mla_ckv_g": mla_ckv_g,
            "mla_qn_g": mla_qn_g, "mla_kn_g": mla_kn_g, "mla_qr_g": mla_qr_g, "mla_kr_g": mla_kr_g,
            "dil_q_g": dil_q_g, "dil_k_g": dil_k_g, "gdn_conv_w": gdn_conv_w, "gdn_a_log": gdn_a_log,
            "gdn_dt_bias": gdn_dt_bias, "gdn_o_g": gdn_o_g, "w_bgate": w_bgate, "b_bgate": b_bgate,
            "w_branch": w_branch, "w_out": w_out, "norm_ffn_g": norm_ffn_g, "w_ffn_in": w_ffn_in,
            "w_ffn_out": w_ffn_out, "norm_ple_g": norm_ple_g, "w_ple": w_ple, "w_ple_gate": w_ple_gate}


def reference(x, p, rel_bias, norm_mix_g, w_in, da_lambda, da_q_g, da_k_g, da_o_g,
              mla_w_uq, mla_w_ukv, mla_cq_g, mla_ckv_g, mla_qn_g, mla_kn_g, mla_qr_g, mla_kr_g,
              dil_q_g, dil_k_g, gdn_conv_w, gdn_a_log, gdn_dt_bias, gdn_o_g,
              w_bgate, b_bgate, w_branch, w_out, norm_ffn_g, w_ffn_in, w_ffn_out,
              norm_ple_g, w_ple, w_ple_gate):
    s = x.shape[1]
    pos = jnp.arange(s)
    da_table = rel_bias[:, :DA_HEADS]
    for i in range(DEPTH):
        h = rmsnorm(x, norm_mix_g[i])
        z = h @ w_in[i]
        z_a, z_b, z_c, z_d = jnp.split(z, IN_SPLITS, axis=-1)
        lam_init = 0.8 - 0.6 * math.exp(-0.3 * i)
        lq1, lk1, lq2, lk2 = (da_lambda[i, j].astype(jnp.float32) for j in range(4))
        lam = jnp.exp(jnp.sum(lq1 * lk1)) - jnp.exp(jnp.sum(lq2 * lk2)) + lam_init
        o_a = diff_attention(z_a, lam, lam_init, da_q_g[i], da_k_g[i], da_o_g[i], da_table)
        o_b = mla_attention(z_b, pos, mla_w_uq[i], mla_w_ukv[i], mla_cq_g[i], mla_ckv_g[i],
                            mla_qn_g[i], mla_kn_g[i], mla_qr_g[i], mla_kr_g[i])
        o_c = dilated_attention(z_c, dil_q_g[i], dil_k_g[i], rel_bias)
        o_d = gated_deltanet(z_d, gdn_conv_w[i], gdn_a_log[i], gdn_dt_bias[i], gdn_o_g[i])
        merged = None
        for n, o_n in enumerate((o_a, o_b, o_c, o_d)):
            gate = jax.nn.sigmoid((h @ w_bgate[i, n] + b_bgate[i, n]).astype(jnp.float32)).astype(x.dtype)
            term = gate * (o_n @ w_branch[i, n])
            merged = term if merged is None else merged + term
        x = x + merged @ w_out[i]
        hf = rmsnorm(x, norm_ffn_g[i])
        gu = hf @ w_ffn_in[i]
        x = x + (jax.nn.silu(gu[..., :D_FF]) * gu[..., D_FF:]) @ w_ffn_out[i]
        hp = rmsnorm(x, norm_ple_g[i])
        x = x + (p[i] @ w_ple[i]) * jax.nn.sigmoid(hp @ w_ple_gate[i])
    return x
```

```python
import functools
import math

import jax
import jax.numpy as jnp
from jax import lax
from jax.experimental import pallas as pl
from jax.experimental.pallas import tpu as pltpu

D_MODEL = 4096
DEPTH = 2
EPS = 1e-6
PLE_DIM = 256
Q_BLOCK = 128
HEADS = 8
N_BRANCH = 4
BRANCH_W = 1024
DA_QK_DIM = 64
DA_V_DIM = 128
MLA_Q_RANK = 1024
MLA_KV_RANK = 512
MLA_NOPE = 128
MLA_ROPE = 64
MLA_V = 128
ROPE_THETA = 10000.0
DIL_GROUPS = ((128, 1), (512, 4), (2048, 16))
DIL_DIM = 128
GDN_DK = 128
GDN_DV = 128
CONV_W = 4
GDN_CHUNK = 64
N_BUCKETS = 32
MAX_DIST = 2048
D_FF = 11008

DA_COLS = 3072
MLA_COLS = 1600
DIL_COLS = 9216
GDN_QKV = 3072

LANE = 128
Z_A = 0
Z_B = Z_A + DA_COLS
Z_C = Z_B + MLA_COLS + 64
Z_D = Z_C + DIL_COLS
Z_COLS = 18 * 1024
D_FF_PAD = 11 * 1024

VMEM_LIMIT = 56 * 1024 * 1024


def _cparams(sem):
    return pltpu.CompilerParams(dimension_semantics=sem, vmem_limit_bytes=VMEM_LIMIT)


def _rmsnorm_kernel(x_ref, g_ref, o_ref):
    x = x_ref[...]
    ms = jnp.mean(x * x, axis=-1, keepdims=True)
    o_ref[...] = (x * lax.rsqrt(ms + EPS) * g_ref[...]).astype(o_ref.dtype)


def rmsnorm_rows(x, g, bm=512):
    m, d = x.shape
    return pl.pallas_call(
        _rmsnorm_kernel,
        grid=(m // bm,),
        in_specs=[pl.BlockSpec((bm, d), lambda i: (i, 0)),
                  pl.BlockSpec((1, d), lambda i: (0, 0))],
        out_specs=pl.BlockSpec((bm, d), lambda i: (i, 0)),
        out_shape=jax.ShapeDtypeStruct((m, d), jnp.bfloat16),
        compiler_params=_cparams(("parallel",)),
        name="rmsnorm_rows",
    )(x, g.reshape(1, d))


def _mm_kernel(x_ref, w_ref, o_ref):
    o_ref[...] = jnp.dot(x_ref[...], w_ref[...], preferred_element_type=jnp.float32).astype(o_ref.dtype)


def matmul(x, w, out_dtype, bm=1024, bn=1024, name="matmul"):
    m, k = x.shape
    _, n = w.shape
    bn = min(bn, n)
    return pl.pallas_call(
        _mm_kernel,
        grid=(m // bm, n // bn),
        in_specs=[pl.BlockSpec((bm, k), lambda i, j: (i, 0)),
                  pl.BlockSpec((k, bn), lambda i, j: (0, j))],
        out_specs=pl.BlockSpec((bm, bn), lambda i, j: (i, j)),
        out_shape=jax.ShapeDtypeStruct((m, n), out_dtype),
        compiler_params=_cparams(("parallel", "parallel")),
        name=name,
    )(x, w)


def _mm_residual_kernel(r_ref, x_ref, w_ref, o_ref, acc_ref, *, nk):
    k = pl.program_id(2)
    part = jnp.dot(x_ref[...], w_ref[...], preferred_element_type=jnp.float32)

    @pl.when(k == 0)
    def _():
        acc_ref[...] = r_ref[...] + part

    @pl.when(k > 0)
    def _():
        acc_ref[...] += part

    @pl.when(k == nk - 1)
    def _():
        o_ref[...] = acc_ref[...]


def matmul_residual(r, x, w, bm=1024, bn=512, bk=None, name="matmul_residual"):
    m, kdim = x.shape
    _, n = w.shape
    bk = kdim if bk is None else bk
    nk = kdim // bk
    return pl.pallas_call(
        functools.partial(_mm_residual_kernel, nk=nk),
        grid=(m // bm, n // bn, nk),
        in_specs=[pl.BlockSpec((bm, bn), lambda i, j, k: (i, j)),
                  pl.BlockSpec((bm, bk), lambda i, j, k: (i, k)),
                  pl.BlockSpec((bk, bn), lambda i, j, k: (k, j))],
        out_specs=pl.BlockSpec((bm, bn), lambda i, j, k: (i, j)),
        out_shape=jax.ShapeDtypeStruct((m, n), jnp.float32),
        scratch_shapes=[pltpu.VMEM((bm, bn), jnp.float32)],
        compiler_params=_cparams(("parallel", "parallel", "arbitrary")),
        name=name,
    )(r, x, w)


def _merge_kernel(h_ref, wg_ref, b_ref, o_ref, wb_ref, out_ref, acc_ref):
    n = pl.program_id(2)
    gate = jax.nn.sigmoid(jnp.dot(h_ref[...], wg_ref[...], preferred_element_type=jnp.float32) + b_ref[...])
    term = gate * jnp.dot(o_ref[...], wb_ref[...], preferred_element_type=jnp.float32)

    @pl.when(n == 0)
    def _():
        acc_ref[...] = term

    @pl.when(n > 0)
    def _():
        acc_ref[...] += term

    @pl.when(n == N_BRANCH - 1)
    def _():
        out_ref[...] = acc_ref[...].astype(out_ref.dtype)


def gated_merge(h, w_bgate, b_bgate, o_all, w_branch, bm=1024, bn=512):
    m, d = h.shape
    return pl.pallas_call(
        _merge_kernel,
        grid=(m // bm, d // bn, N_BRANCH),
        in_specs=[pl.BlockSpec((bm, d), lambda i, j, n: (i, 0)),
                  pl.BlockSpec((None, d, bn), lambda i, j, n: (n, 0, j)),
                  pl.BlockSpec((None, 1, bn), lambda i, j, n: (n, 0, j)),
                  pl.BlockSpec((None, bm, BRANCH_W), lambda i, j, n: (n, i, 0)),
                  pl.BlockSpec((None, BRANCH_W, bn), lambda i, j, n: (n, 0, j))],
        out_specs=pl.BlockSpec((bm, bn), lambda i, j, n: (i, j)),
        out_shape=jax.ShapeDtypeStruct((m, d), jnp.bfloat16),
        scratch_shapes=[pltpu.VMEM((bm, bn), jnp.float32)],
        compiler_params=_cparams(("parallel", "parallel", "arbitrary")),
        name="gated_merge",
    )(h, w_bgate, b_bgate, o_all, w_branch)


def _swiglu_kernel(x_ref, wg_ref, wu_ref, o_ref):
    x = x_ref[...]
    g = jnp.dot(x, wg_ref[...], preferred_element_type=jnp.float32)
    u = jnp.dot(x, wu_ref[...], preferred_element_type=jnp.float32)
    o_ref[...] = (g * jax.nn.sigmoid(g) * u).astype(o_ref.dtype)


def swiglu_in(x, wg, wu, bm=1024, bn=512):
    m, k = x.shape
    _, n = wg.shape
    return pl.pallas_call(
        _swiglu_kernel,
        grid=(m // bm, n // bn),
        in_specs=[pl.BlockSpec((bm, k), lambda i, j: (i, 0)),
                  pl.BlockSpec((k, bn), lambda i, j: (0, j)),
                  pl.BlockSpec((k, bn), lambda i, j: (0, j))],
        out_specs=pl.BlockSpec((bm, bn), lambda i, j: (i, j)),
        out_shape=jax.ShapeDtypeStruct((m, n), jnp.bfloat16),
        compiler_params=_cparams(("parallel", "parallel")),
        name="swiglu_in",
    )(x, wg, wu)


def _ple_kernel(r_ref, p_ref, wp_ref, h_ref, wg_ref, o_ref):
    e = jnp.dot(p_ref[...], wp_ref[...], preferred_element_type=jnp.float32)
    g = jnp.dot(h_ref[...], wg_ref[...], preferred_element_type=jnp.float32)
    o_ref[...] = r_ref[...] + e * jax.nn.sigmoid(g)


def ple_update(r, p, w_ple, hp, w_gate, bm=1024, bn=512):
    m, d = r.shape
    kp = p.shape[1]
    return pl.pallas_call(
        _ple_kernel,
        grid=(m // bm, d // bn),
        in_specs=[pl.BlockSpec((bm, bn), lambda i, j: (i, j)),
                  pl.BlockSpec((bm, kp), lambda i, j: (i, 0)),
                  pl.BlockSpec((kp, bn), lambda i, j: (0, j)),
                  pl.BlockSpec((bm, d), lambda i, j: (i, 0)),
                  pl.BlockSpec((d, bn), lambda i, j: (0, j))],
        out_specs=pl.BlockSpec((bm, bn), lambda i, j: (i, j)),
        out_shape=jax.ShapeDtypeStruct((m, d), jnp.float32),
        compiler_params=_cparams(("parallel", "parallel")),
        name="ple_update",
    )(r, p, w_ple, hp, w_gate)


def _rms(x, g):
    xf = x.astype(jnp.float32)
    y = xf * lax.rsqrt(jnp.mean(xf * xf, axis=-1, keepdims=True) + EPS)
    return (y * g.astype(jnp.float32)).astype(x.dtype)


def _l2(x):
    xf = x.astype(jnp.float32)
    return xf * lax.rsqrt(jnp.sum(xf * xf, axis=-1, keepdims=True) + EPS)


def _rel_bucket(dist):
    max_exact = N_BUCKETS // 2
    d = jnp.maximum(dist, 0)
    large = max_exact + (jnp.log(jnp.maximum(d, 1).astype(jnp.float32) / max_exact)
                         / math.log(MAX_DIST / max_exact) * (N_BUCKETS - max_exact)).astype(jnp.int32)
    large = jnp.minimum(large, N_BUCKETS - 1)
    return jnp.where(d < max_exact, d, large)


def _rope(x, pos):
    half = x.shape[-1] // 2
    inv = ROPE_THETA ** (-jnp.arange(half, dtype=jnp.float32) / half)
    ang = pos.astype(jnp.float32)[:, None] * inv[None, :]
    cos = jnp.cos(ang)[None, :, None, :]
    sin = jnp.sin(ang)[None, :, None, :]
    xf = x.astype(jnp.float32)
    x1, x2 = xf[..., :half], xf[..., half:]
    return jnp.concatenate([x1 * cos - x2 * sin, x2 * cos + x1 * sin], axis=-1).astype(x.dtype)


def _diff_attention(z, lam, lam_init, q_g, k_g, o_g, table):
    b, s, _ = z.shape
    nqk = HEADS * 2 * DA_QK_DIM
    q = _rms(z[..., :nqk].reshape(b, s, HEADS, 2, DA_QK_DIM), q_g)
    k = _rms(z[..., nqk:2 * nqk].reshape(b, s, HEADS, 2, DA_QK_DIM), k_g)
    v = z[..., 2 * nqk:].reshape(b, s, HEADS, DA_V_DIM)
    scale = DA_QK_DIM ** -0.5
    kpos = jnp.arange(s)

    def block(i):
        start = i * Q_BLOCK
        qb = lax.dynamic_slice_in_dim(q, start, Q_BLOCK, axis=1)
        dist = (start + jnp.arange(Q_BLOCK))[:, None] - kpos[None, :]
        bias = jnp.transpose(table[_rel_bucket(dist)], (2, 0, 1)).astype(jnp.float32)
        logits = jnp.einsum('bqhmd,bkhmd->bmhqk', qb, k).astype(jnp.float32) * scale + bias
        logits = jnp.where(dist >= 0, logits, -jnp.inf)
        probs = jax.nn.softmax(logits, axis=-1)
        attn = probs[:, 0] - lam * probs[:, 1]
        return jnp.einsum('bhqk,bkhd->bqhd', attn.astype(v.dtype), v)

    o = lax.map(block, jnp.arange(s // Q_BLOCK))
    o = jnp.moveaxis(o, 0, 1).reshape(b, s, HEADS, DA_V_DIM)
    o = _rms(o, o_g) * (1.0 - lam_init)
    return o.reshape(b, s, HEADS * DA_V_DIM)


def _mla_attention(z, pos, w_uq, w_ukv, cq_g, ckv_g, qn_g, kn_g, qr_g, kr_g):
    b, s, _ = z.shape
    c_q = _rms(z[..., :MLA_Q_RANK], cq_g)
    c_kv = _rms(z[..., MLA_Q_RANK:MLA_Q_RANK + MLA_KV_RANK], ckv_g)
    k_r = z[..., MLA_Q_RANK + MLA_KV_RANK:]
    q = (c_q @ w_uq).reshape(b, s, HEADS, MLA_NOPE + MLA_ROPE)
    kv = (c_kv @ w_ukv).reshape(b, s, HEADS, MLA_NOPE + MLA_V)
    q_n = _rms(q[..., :MLA_NOPE], qn_g)
    q_r = _rope(_rms(q[..., MLA_NOPE:], qr_g), pos)
    k_n = _rms(kv[..., :MLA_NOPE], kn_g)
    v = kv[..., MLA_NOPE:]
    k_r = _rope(_rms(k_r, kr_g)[:, :, None, :], pos)[:, :, 0]
    scale = (MLA_NOPE + MLA_ROPE) ** -0.5
    kpos = jnp.arange(s)

    def block(i):
        start = i * Q_BLOCK
        qn_b = lax.dynamic_slice_in_dim(q_n, start, Q_BLOCK, axis=1)
        qr_b = lax.dynamic_slice_in_dim(q_r, start, Q_BLOCK, axis=1)
        logits = (jnp.einsum('bqhd,bkhd->bhqk', qn_b, k_n)
                  + jnp.einsum('bqhd,bkd->bhqk', qr_b, k_r)).astype(jnp.float32) * scale
        causal = (start + jnp.arange(Q_BLOCK))[:, None] >= kpos[None, :]
        probs = jax.nn.softmax(jnp.where(causal, logits, -jnp.inf), axis=-1)
        return jnp.einsum('bhqk,bkhd->bqhd', probs.astype(v.dtype), v)

    o = lax.map(block, jnp.arange(s // Q_BLOCK))
    return jnp.moveaxis(o, 0, 1).reshape(b, s, HEADS * MLA_V)


def _dilated_group(q, k, v, window, dil, table):
    b, s, h, d = q.shape
    steps = window // dil
    length = s // dil
    lp = -(-length // Q_BLOCK) * Q_BLOCK
    nb = lp // Q_BLOCK

    def to_sub(t):
        t = t.reshape(b, length, dil, h, d).transpose(0, 2, 1, 3, 4)
        return jnp.pad(t, ((0, 0), (0, 0), (0, lp - length), (0, 0), (0, 0)))

    def band(t):
        tb = t.reshape(b, dil, nb, Q_BLOCK, h, d)
        prev = jnp.pad(tb, ((0, 0), (0, 0), (1, 0), (0, 0), (0, 0), (0, 0)))[:, :, :-1]
        return jnp.concatenate([prev, tb], axis=3)

    qb = to_sub(q).reshape(b, dil, nb, Q_BLOCK, h, d)
    kb = band(to_sub(k))
    vb = band(to_sub(v))
    qi = jnp.arange(Q_BLOCK)[:, None]
    kj = jnp.arange(2 * Q_BLOCK)[None, :]
    step = qi + Q_BLOCK - kj
    key_sub = jnp.arange(nb)[:, None, None] * Q_BLOCK + kj[None] - Q_BLOCK
    valid = (step >= 0) & (step <= steps) & (key_sub >= 0)
    bias = jnp.transpose(table[_rel_bucket(step * dil)], (2, 0, 1)).astype(jnp.float32)
    logits = jnp.einsum('brnqhd,brnkhd->brnhqk', qb, kb).astype(jnp.float32) * (d ** -0.5) + bias
    logits = jnp.where(valid[None, None, :, None], logits, -jnp.inf)
    m = jnp.max(logits, axis=-1, keepdims=True)
    e = jnp.exp(logits - m)
    den = jnp.sum(e, axis=-1, keepdims=True)
    o = jnp.einsum('brnhqk,brnkhd->brnqhd', (e / den).astype(v.dtype), vb)
    lse = (m + jnp.log(den))[..., 0]
    o = o.reshape(b, dil, lp, h, d)[:, :, :length].transpose(0, 2, 1, 3, 4).reshape(b, s, h, d)
    lse = lse.transpose(0, 1, 2, 4, 3).reshape(b, dil, lp, h)[:, :, :length]
    lse = lse.transpose(0, 2, 1, 3).reshape(b, s, h)
    return o, lse


def _dilated_attention(z, q_g, k_g, table):
    b, s, _ = z.shape
    zc = z.reshape(b, s, len(DIL_GROUPS), 3, HEADS, DIL_DIM)
    q = _rms(zc[:, :, :, 0], q_g)
    k = _rms(zc[:, :, :, 1], k_g)
    v = zc[:, :, :, 2]
    outs, lses = [], []
    for gi, (window, dil) in enumerate(DIL_GROUPS):
        lo = HEADS + gi * HEADS
        o_g, lse_g = _dilated_group(q[:, :, gi], k[:, :, gi], v[:, :, gi], window, dil,
                                    table[:, lo:lo + HEADS])
        outs.append(o_g)
        lses.append(lse_g)
    w = jax.nn.softmax(jnp.stack(lses, axis=0), axis=0)
    o = sum(w[gi][..., None] * outs[gi].astype(jnp.float32) for gi in range(len(DIL_GROUPS)))
    return o.astype(z.dtype).reshape(b, s, HEADS * DIL_DIM)


def _causal_conv(t, w):
    c = t.shape[-1]
    return lax.conv_general_dilated(t, w[:, None, :].astype(t.dtype), window_strides=(1,),
                                    padding=[(CONV_W - 1, 0)],
                                    dimension_numbers=('NWC', 'WIO', 'NWC'),
                                    feature_group_count=c)


def _chunk_gated_delta_rule(q, k, v, g, beta):
    b, s, h, dk = q.shape
    dv = v.shape[-1]
    c = GDN_CHUNK
    n = s // c

    def chunks(t):
        return t.astype(jnp.float32).reshape(b, n, c, h, -1).transpose(0, 3, 1, 2, 4)

    q, k, v = chunks(q), chunks(k), chunks(v)
    beta = beta.astype(jnp.float32).reshape(b, n, c, h).transpose(0, 3, 1, 2)
    g = jnp.cumsum(g.astype(jnp.float32).reshape(b, n, c, h).transpose(0, 3, 1, 2), axis=-1)
    tril = jnp.tril(jnp.ones((c, c), dtype=bool))
    strict = jnp.tril(jnp.ones((c, c), dtype=bool), -1)
    decay = jnp.exp(jnp.where(tril, g[..., :, None] - g[..., None, :], -jnp.inf))
    kb = k * beta[..., None]
    vb = v * beta[..., None]
    lower = jnp.where(strict, jnp.einsum('bhncd,bhnkd->bhnck', kb, k) * decay, 0.0)
    eye = jnp.eye(c, dtype=jnp.float32)
    tmat = lax.linalg.triangular_solve(eye + lower, jnp.broadcast_to(eye, lower.shape),
                                       left_side=True, lower=True, unit_diagonal=True)
    u = jnp.einsum('bhnck,bhnkv->bhncv', tmat, vb)
    w = jnp.einsum('bhnck,bhnkd->bhncd', tmat, kb * jnp.exp(g)[..., None])
    qg = q * jnp.exp(g)[..., None]
    kdec = k * jnp.exp(g[..., -1:] - g)[..., None]
    intra = jnp.einsum('bhncd,bhnkd->bhnck', q, k) * decay
    glast = g[..., -1]

    def step(state, xs):
        qg_c, kd_c, u_c, w_c, a_c, gl_c = xs
        v_new = u_c - jnp.einsum('bhck,bhkv->bhcv', w_c, state)
        o_c = jnp.einsum('bhck,bhkv->bhcv', qg_c, state) + jnp.einsum('bhcj,bhjv->bhcv', a_c, v_new)
        state = state * jnp.exp(gl_c)[..., None, None] + jnp.einsum('bhck,bhcv->bhkv', kd_c, v_new)
        return state, o_c

    xs = tuple(jnp.moveaxis(t, 2, 0) for t in (qg, kdec, u, w, intra, glast))
    state0 = jnp.zeros((b, h, dk, dv), jnp.float32)
    _, o = lax.scan(step, state0, xs)
    return o.transpose(1, 0, 3, 2, 4).reshape(b, s, h, dv)


def _gated_deltanet(qkv_in, a, bt, gate, conv_w, a_log, dt_bias, o_g):
    b, s, _ = qkv_in.shape
    nq = HEADS * GDN_DK
    qkv = jax.nn.silu(_causal_conv(qkv_in, conv_w))
    gate = gate.reshape(b, s, HEADS, GDN_DV)
    q = _l2(qkv[..., :nq].reshape(b, s, HEADS, GDN_DK)) * (GDN_DK ** -0.5)
    k = _l2(qkv[..., nq:2 * nq].reshape(b, s, HEADS, GDN_DK))
    v = qkv[..., 2 * nq:].reshape(b, s, HEADS, GDN_DV)
    beta = jax.nn.sigmoid(bt.astype(jnp.float32))
    g = -jnp.exp(a_log.astype(jnp.float32)) * jax.nn.softplus(a.astype(jnp.float32) + dt_bias.astype(jnp.float32))
    o = _chunk_gated_delta_rule(q, k, v, g, beta).astype(qkv_in.dtype)
    o = _rms(o, o_g) * jax.nn.silu(gate)
    return o.reshape(b, s, HEADS * GDN_DV)


def _pack_w_in(w):
    d = w.shape[0]
    c0 = DA_COLS + MLA_COLS + DIL_COLS
    qkv = w[:, c0:c0 + GDN_QKV]
    ab = w[:, c0 + GDN_QKV:c0 + GDN_QKV + 2 * HEADS]
    gate = w[:, c0 + GDN_QKV + 2 * HEADS:]
    used = Z_D + GDN_QKV + HEADS * GDN_DV + 2 * HEADS
    parts = [w[:, :DA_COLS + MLA_COLS], jnp.zeros((d, 64), w.dtype), w[:, DA_COLS + MLA_COLS:c0],
             qkv, gate, ab, jnp.zeros((d, Z_COLS - used), w.dtype)]
    return jnp.concatenate(parts, axis=1).astype(jnp.bfloat16)


def kernel(x, p, rel_bias, norm_mix_g, w_in, da_lambda, da_q_g, da_k_g, da_o_g, mla_w_uq, mla_w_ukv, mla_cq_g, mla_ckv_g, mla_qn_g, mla_kn_g, mla_qr_g, mla_kr_g, dil_q_g, dil_k_g, gdn_conv_w, gdn_a_log, gdn_dt_bias, gdn_o_g, w_bgate, b_bgate, w_branch, w_out, norm_ffn_g, w_ffn_in, w_ffn_out, norm_ple_g, w_ple, w_ple_gate):
    b, s, d = x.shape
    m = b * s
    bf = jnp.bfloat16
    pos = jnp.arange(s)
    da_table = rel_bias[:, :HEADS]
    xr = x.reshape(m, d)
    for i in range(DEPTH):
        h = rmsnorm_rows(xr, norm_mix_g[i])
        z = matmul(h, _pack_w_in(w_in[i]), jnp.float32, name="in_proj")
        z3 = z.reshape(b, s, Z_COLS)
        lam_init = 0.8 - 0.6 * math.exp(-0.3 * i)
        lq1, lk1, lq2, lk2 = (da_lambda[i, j].astype(jnp.float32) for j in range(4))
        lam = jnp.exp(jnp.sum(lq1 * lk1)) - jnp.exp(jnp.sum(lq2 * lk2)) + lam_init
        o_a = _diff_attention(z3[..., Z_A:Z_A + DA_COLS], lam, lam_init, da_q_g[i], da_k_g[i], da_o_g[i], da_table)
        o_b = _mla_attention(z3[..., Z_B:Z_B + MLA_COLS], pos, mla_w_uq[i], mla_w_ukv[i], mla_cq_g[i],
                             mla_ckv_g[i], mla_qn_g[i], mla_kn_g[i], mla_qr_g[i], mla_kr_g[i])
        o_c = _dilated_attention(z3[..., Z_C:Z_C + DIL_COLS], dil_q_g[i], dil_k_g[i], rel_bias)
        zd = z3[..., Z_D:]
        o_d = _gated_deltanet(zd[..., :GDN_QKV], zd[..., GDN_QKV + 1024:GDN_QKV + 1024 + HEADS],
                              zd[..., GDN_QKV + 1024 + HEADS:GDN_QKV + 1024 + 2 * HEADS],
                              zd[..., GDN_QKV:GDN_QKV + 1024],
                              gdn_conv_w[i], gdn_a_log[i], gdn_dt_bias[i], gdn_o_g[i])
        o_all = jnp.stack([o.reshape(m, BRANCH_W).astype(bf) for o in (o_a, o_b, o_c, o_d)], axis=0)
        merged = gated_merge(h, w_bgate[i].astype(bf), b_bgate[i].reshape(N_BRANCH, 1, d), o_all,
                             w_branch[i].astype(bf))
        xr = matmul_residual(xr, merged, w_out[i].astype(bf), name="out_proj")
        hf = rmsnorm_rows(xr, norm_ffn_g[i])
        wf = w_ffn_in[i]
        padc = ((0, 0), (0, D_FF_PAD - D_FF))
        act = swiglu_in(hf, jnp.pad(wf[:, :D_FF], padc).astype(bf), jnp.pad(wf[:, D_FF:], padc).astype(bf))
        w2 = jnp.pad(w_ffn_out[i], ((0, D_FF_PAD - D_FF), (0, 0))).astype(bf)
        xr = matmul_residual(xr, act, w2, bm=1024, bn=512, bk=D_FF_PAD // 2, name="ffn_out")
        hp = rmsnorm_rows(xr, norm_ple_g[i])
        xr = ple_update(xr, p[i].reshape(m, PLE_DIM).astype(bf), w_ple[i].astype(bf), hp, w_ple_gate[i].astype(bf))
    return xr.reshape(b, s, d)
```

```python
import functools
import math

import numpy as np
import jax
import jax.numpy as jnp
from jax import lax
from jax.experimental import pallas as pl
from jax.experimental.pallas import tpu as pltpu

D_MODEL = 4096
DEPTH = 2
EPS = 1e-6
PLE_DIM = 256
HEADS = 8
N_BRANCH = 4
BRANCH_W = 1024
DA_QK_DIM = 64
MLA_Q_RANK = 1024
MLA_KV_RANK = 512
MLA_NOPE = 128
MLA_ROPE = 64
ROPE_THETA = 10000.0
DIL_GROUPS = ((128, 1), (512, 4), (2048, 16))
DIL_STEPS = 128
GDN_CHUNK = 64
CONV_W = 4
N_BUCKETS = 32
MAX_DIST = 2048
D_FF = 11008

DA_COLS = 3072
MLA_COLS = 1600
DIL_COLS = 9216
GDN_QKV = 3072

LANE = 128
Z_A = 0
Z_B = Z_A + DA_COLS
Z_C = Z_B + MLA_COLS + 64
Z_D = Z_C + DIL_COLS
Z_COLS = 18 * 1024
D_FF_PAD = 11 * 1024

NEG = -1e30
ATT_T = 256
VMEM_LIMIT = 56 * 1024 * 1024
BF = jnp.bfloat16
F32 = jnp.float32


def _cparams(sem):
    return pltpu.CompilerParams(dimension_semantics=sem, vmem_limit_bytes=VMEM_LIMIT)


def _rmsnorm_kernel(x_ref, g_ref, o_ref):
    x = x_ref[...]
    ms = jnp.mean(x * x, axis=-1, keepdims=True)
    o_ref[...] = (x * lax.rsqrt(ms + EPS) * g_ref[...]).astype(o_ref.dtype)


def rmsnorm_rows(x, g, col_block=0, bm=512):
    m = x.shape[0]
    d = g.shape[-1]
    return pl.pallas_call(
        _rmsnorm_kernel,
        grid=(m // bm,),
        in_specs=[pl.BlockSpec((bm, d), lambda i: (i, col_block)),
                  pl.BlockSpec((1, d), lambda i: (0, 0))],
        out_specs=pl.BlockSpec((bm, d), lambda i: (i, 0)),
        out_shape=jax.ShapeDtypeStruct((m, d), BF),
        compiler_params=_cparams(("parallel",)),
        name="rmsnorm_rows",
    )(x, g.reshape(1, d))


def _mm_kernel(x_ref, w_ref, o_ref):
    o_ref[...] = jnp.dot(x_ref[...], w_ref[...], preferred_element_type=F32).astype(o_ref.dtype)


def matmul(x, w, out_dtype, bm=1024, bn=1024, name="matmul"):
    m, k = x.shape
    _, n = w.shape
    bn = min(bn, n)
    bm = min(bm, m)
    return pl.pallas_call(
        _mm_kernel,
        grid=(m // bm, n // bn),
        in_specs=[pl.BlockSpec((bm, k), lambda i, j: (i, 0)),
                  pl.BlockSpec((k, bn), lambda i, j: (0, j))],
        out_specs=pl.BlockSpec((bm, bn), lambda i, j: (i, j)),
        out_shape=jax.ShapeDtypeStruct((m, n), out_dtype),
        compiler_params=_cparams(("parallel", "parallel")),
        name=name,
    )(x, w)


def _mm_residual_kernel(r_ref, x_ref, w_ref, o_ref, acc_ref, *, nk):
    k = pl.program_id(2)
    part = jnp.dot(x_ref[...], w_ref[...], preferred_element_type=F32)

    @pl.when(k == 0)
    def _():
        acc_ref[...] = r_ref[...] + part

    @pl.when(k > 0)
    def _():
        acc_ref[...] += part

    @pl.when(k == nk - 1)
    def _():
        o_ref[...] = acc_ref[...]


def matmul_residual(r, x, w, bm=1024, bn=512, bk=None, name="matmul_residual"):
    m, kdim = x.shape
    _, n = w.shape
    bk = kdim if bk is None else bk
    bm = min(bm, m)
    nk = kdim // bk
    return pl.pallas_call(
        functools.partial(_mm_residual_kernel, nk=nk),
        grid=(m // bm, n // bn, nk),
        in_specs=[pl.BlockSpec((bm, bn), lambda i, j, k: (i, j)),
                  pl.BlockSpec((bm, bk), lambda i, j, k: (i, k)),
                  pl.BlockSpec((bk, bn), lambda i, j, k: (k, j))],
        out_specs=pl.BlockSpec((bm, bn), lambda i, j, k: (i, j)),
        out_shape=jax.ShapeDtypeStruct((m, n), F32),
        scratch_shapes=[pltpu.VMEM((bm, bn), F32)],
        compiler_params=_cparams(("parallel", "parallel", "arbitrary")),
        name=name,
    )(r, x, w)


def _merge_kernel(h_ref, wg_ref, b_ref, o_ref, wb_ref, out_ref, acc_ref):
    n = pl.program_id(2)
    gate = jax.nn.sigmoid(jnp.dot(h_ref[...], wg_ref[...], preferred_element_type=F32) + b_ref[...])
    term = gate * jnp.dot(o_ref[...], wb_ref[...], preferred_element_type=F32)

    @pl.when(n == 0)
    def _():
        acc_ref[...] = term

    @pl.when(n > 0)
    def _():
        acc_ref[...] += term

    @pl.when(n == N_BRANCH - 1)
    def _():
        out_ref[...] = acc_ref[...].astype(out_ref.dtype)


def gated_merge(h, w_bgate, b_bgate, o_all, w_branch, bm=1024, bn=512):
    m, d = h.shape
    bm = min(bm, m)
    return pl.pallas_call(
        _merge_kernel,
        grid=(m // bm, d // bn, N_BRANCH),
        in_specs=[pl.BlockSpec((bm, d), lambda i, j, n: (i, 0)),
                  pl.BlockSpec((None, d, bn), lambda i, j, n: (n, 0, j)),
                  pl.BlockSpec((None, 1, bn), lambda i, j, n: (n, 0, j)),
                  pl.BlockSpec((None, bm, BRANCH_W), lambda i, j, n: (n, i, 0)),
                  pl.BlockSpec((None, BRANCH_W, bn), lambda i, j, n: (n, 0, j))],
        out_specs=pl.BlockSpec((bm, bn), lambda i, j, n: (i, j)),
        out_shape=jax.ShapeDtypeStruct((m, d), BF),
        scratch_shapes=[pltpu.VMEM((bm, bn), F32)],
        compiler_params=_cparams(("parallel", "parallel", "arbitrary")),
        name="gated_merge",
    )(h, w_bgate, b_bgate, o_all, w_branch)


def _swiglu_kernel(x_ref, wg_ref, wu_ref, o_ref):
    x = x_ref[...]
    g = jnp.dot(x, wg_ref[...], preferred_element_type=F32)
    u = jnp.dot(x, wu_ref[...], preferred_element_type=F32)
    o_ref[...] = (g * jax.nn.sigmoid(g) * u).astype(o_ref.dtype)


def swiglu_in(x, wg, wu, bm=1024, bn=512):
    m, k = x.shape
    _, n = wg.shape
    bm = min(bm, m)
    return pl.pallas_call(
        _swiglu_kernel,
        grid=(m // bm, n // bn),
        in_specs=[pl.BlockSpec((bm, k), lambda i, j: (i, 0)),
                  pl.BlockSpec((k, bn), lambda i, j: (0, j)),
                  pl.BlockSpec((k, bn), lambda i, j: (0, j))],
        out_specs=pl.BlockSpec((bm, bn), lambda i, j: (i, j)),
        out_shape=jax.ShapeDtypeStruct((m, n), BF),
        compiler_params=_cparams(("parallel", "parallel")),
        name="swiglu_in",
    )(x, wg, wu)


def _ple_kernel(r_ref, p_ref, wp_ref, h_ref, wg_ref, o_ref):
    e = jnp.dot(p_ref[...], wp_ref[...], preferred_element_type=F32)
    g = jnp.dot(h_ref[...], wg_ref[...], preferred_element_type=F32)
    o_ref[...] = r_ref[...] + e * jax.nn.sigmoid(g)


def ple_update(r, p, w_ple, hp, w_gate, bm=1024, bn=512):
    m, d = r.shape
    kp = p.shape[1]
    bm = min(bm, m)
    return pl.pallas_call(
        _ple_kernel,
        grid=(m // bm, d // bn),
        in_specs=[pl.BlockSpec((bm, bn), lambda i, j: (i, j)),
                  pl.BlockSpec((bm, kp), lambda i, j: (i, 0)),
                  pl.BlockSpec((kp, bn), lambda i, j: (0, j)),
                  pl.BlockSpec((bm, d), lambda i, j: (i, 0)),
                  pl.BlockSpec((d, bn), lambda i, j: (0, j))],
        out_specs=pl.BlockSpec((bm, bn), lambda i, j: (i, j)),
        out_shape=jax.ShapeDtypeStruct((m, d), F32),
        compiler_params=_cparams(("parallel", "parallel")),
        name="ple_update",
    )(r, p, w_ple, hp, w_gate)


def _static_buckets(dist):
    max_exact = N_BUCKETS // 2
    d = np.maximum(np.asarray(dist), 0)
    large = max_exact + (np.log(np.maximum(d, 1).astype(np.float32) / np.float32(max_exact))
                         / np.float32(math.log(MAX_DIST / max_exact))
                         * np.float32(N_BUCKETS - max_exact)).astype(np.int32)
    large = np.minimum(large, N_BUCKETS - 1)
    return np.where(d < max_exact, d, large).astype(np.int32)


def _toeplitz_tiles(f, t):
    hh, n = f.shape
    big = n + t - 1
    fpad = jnp.concatenate([jnp.full((hh, t - 1), NEG, f.dtype), f], axis=1)
    flat = jnp.tile(fpad, (1, t + 1))[:, :t * (big + 1)]
    w = flat.reshape(hh, t, big + 1)[:, :, :n]
    w = jnp.flip(w.reshape(hh, t, n // t, t), axis=3)
    return w.transpose(0, 2, 1, 3)


def diff_bias_tiles(rel_bias, s):
    f = jnp.take(rel_bias[:, :HEADS], _static_buckets(np.arange(s)), axis=0).T
    return _toeplitz_tiles(f.astype(F32), ATT_T)


def dil_bias_tiles(rel_bias):
    out = []
    for gi, (_, dil) in enumerate(DIL_GROUPS):
        lo = HEADS + gi * HEADS
        steps = np.arange(2 * DIL_STEPS)
        f = jnp.take(rel_bias[:, lo:lo + HEADS], _static_buckets(steps * dil), axis=0).T.astype(F32)
        f = jnp.where(steps[None, :] <= DIL_STEPS, f, NEG)
        tiles = _toeplitz_tiles(f, DIL_STEPS)
        general = jnp.concatenate([tiles[:, 1], tiles[:, 0]], axis=-1)
        first = jnp.concatenate([jnp.full_like(tiles[:, 1], NEG), tiles[:, 0]], axis=-1)
        out.append(jnp.stack([general, first], axis=1))
    return jnp.stack(out, axis=0)


def rope_tables(s):
    half = MLA_ROPE // 2
    inv = ROPE_THETA ** (-jnp.arange(half, dtype=F32) / half)
    ang = jnp.arange(s).astype(F32)[:, None] * inv[None, :]
    cos, sin = jnp.cos(ang), jnp.sin(ang)
    zero = jnp.zeros((s, LANE - MLA_ROPE), F32)
    return (jnp.concatenate([cos, cos, zero], axis=1), jnp.concatenate([-sin, sin, zero], axis=1))


def _rope_tile(t, c, s):
    half = MLA_ROPE // 2
    swapped = pltpu.roll(t, half, axis=1) + pltpu.roll(t, LANE - half, axis=1)
    return t * c + swapped * s


def _flash_kernel(lam_ref, q_ref, k_ref, v_ref, *rest, nmaps, t, use_bias, scale, post_scale):
    if use_bias:
        bias_ref, g_ref, o_ref, m_sc, l_sc, acc_sc = rest
    else:
        o_ref, m_sc, l_sc, acc_sc = rest
    i = pl.program_id(2)
    rows = nmaps * t
    q = q_ref[...].reshape(rows, q_ref.shape[-1])
    m_sc[...] = jnp.full(m_sc.shape, NEG, F32)
    l_sc[...] = jnp.zeros(l_sc.shape, F32)
    acc_sc[...] = jnp.zeros(acc_sc.shape, F32)

    def step(j, diagonal):
        start = pl.multiple_of(j * t, t)
        k = k_ref[pl.ds(start, t), :]
        v = v_ref[pl.ds(start, t), :]
        s = lax.dot_general(q, k, (((1,), (1,)), ((), ())), preferred_element_type=F32)
        if scale != 1.0:
            s = s * scale
        if use_bias:
            s = (s.reshape(nmaps, t, t) + bias_ref[i - j][None]).reshape(rows, t)
        elif diagonal:
            r_i = lax.broadcasted_iota(jnp.int32, (t, t), 0)
            c_i = lax.broadcasted_iota(jnp.int32, (t, t), 1)
            s = jnp.where((r_i >= c_i)[None], s.reshape(nmaps, t, t), NEG).reshape(rows, t)
        m_prev = m_sc[...]
        m_new = jnp.maximum(m_prev, jnp.max(s, axis=1, keepdims=True))
        alpha = jnp.exp(m_prev - m_new)
        p = jnp.exp(s - m_new)
        l_sc[...] = alpha * l_sc[...] + jnp.sum(p, axis=1, keepdims=True)
        acc_sc[...] = alpha * acc_sc[...] + jnp.dot(p.astype(BF), v, preferred_element_type=F32)
        m_sc[...] = m_new

    def body(j, carry):
        step(j, False)
        return carry

    lax.fori_loop(0, i, body, 0)
    step(i, True)

    o = acc_sc[...] / l_sc[...]
    if nmaps == 2:
        o = o[:t] - lam_ref[0] * o[t:]
        ms = jnp.mean(o * o, axis=-1, keepdims=True)
        o = o * lax.rsqrt(ms + EPS) * g_ref[...] * post_scale
    o_ref[...] = o.astype(o_ref.dtype)


def flash_attention(q, k, v, s, *, nmaps, dk, scale, bias=None, lam=None, gain=None, post_scale=1.0, name):
    m = k.shape[0]
    b = m // s
    t = min(ATT_T, s)
    nq = s // t
    use_bias = bias is not None
    lam = jnp.zeros((1,), F32) if lam is None else lam.reshape(1).astype(F32)
    in_specs = [pl.BlockSpec(memory_space=pltpu.SMEM),
                pl.BlockSpec((nmaps, t, dk), lambda bb, h, i: (0, bb * nq + i, h)),
                pl.BlockSpec((s, dk), lambda bb, h, i: (bb, h)),
                pl.BlockSpec((s, LANE), lambda bb, h, i: (bb, h))]
    args = [lam, q, k, v]
    if use_bias:
        in_specs += [pl.BlockSpec((None, nq, t, t), lambda bb, h, i: (h, 0, 0, 0)),
                     pl.BlockSpec((1, LANE), lambda bb, h, i: (0, 0))]
        args += [bias, gain.reshape(1, LANE).astype(F32)]
    rows = nmaps * t
    return pl.pallas_call(
        functools.partial(_flash_kernel, nmaps=nmaps, t=t, use_bias=use_bias, scale=scale, post_scale=post_scale),
        grid=(b, HEADS, nq),
        in_specs=in_specs,
        out_specs=pl.BlockSpec((t, LANE), lambda bb, h, i: (bb * nq + i, h)),
        out_shape=jax.ShapeDtypeStruct((m, HEADS * LANE), BF),
        scratch_shapes=[pltpu.VMEM((rows, 1), F32), pltpu.VMEM((rows, 1), F32), pltpu.VMEM((rows, LANE), F32)],
        compiler_params=_cparams(("parallel", "parallel", "arbitrary")),
        name=name,
    )(*args)


def _prep_a_kernel(z_ref, qg_ref, kg_ref, qz_ref, kn_ref, v_ref):
    lane = lax.broadcasted_iota(jnp.int32, (1, LANE), 1)
    lo = lane < DA_QK_DIM

    def norm_halves(x, g):
        x2 = x * x
        s_lo = jnp.sum(jnp.where(lo, x2, 0.0), axis=-1, keepdims=True)
        s_hi = jnp.sum(jnp.where(lo, 0.0, x2), axis=-1, keepdims=True)
        inv = lax.rsqrt(jnp.where(lo, s_lo, s_hi) * (1.0 / DA_QK_DIM) + EPS)
        return x * inv * g

    scale = DA_QK_DIM ** -0.5
    for h in range(HEADS):
        cq = slice(h * LANE, (h + 1) * LANE)
        ck = slice(HEADS * LANE + h * LANE, HEADS * LANE + (h + 1) * LANE)
        q = norm_halves(z_ref[:, cq], qg_ref[...]) * scale
        qz_ref[0, :, cq] = jnp.where(lo, q, 0.0).astype(BF)
        qz_ref[1, :, cq] = jnp.where(lo, 0.0, q).astype(BF)
        kn_ref[:, cq] = norm_halves(z_ref[:, ck], kg_ref[...]).astype(BF)
    v_ref[...] = z_ref[:, 2 * HEADS * LANE:].astype(BF)


def prep_a(z, q_g, k_g, bm=256):
    m = z.shape[0]
    bm = min(bm, m)
    w = HEADS * LANE
    g2 = lambda g: jnp.concatenate([g, g]).reshape(1, LANE).astype(F32)
    return pl.pallas_call(
        _prep_a_kernel,
        grid=(m // bm,),
        in_specs=[pl.BlockSpec((bm, DA_COLS), lambda i: (i, Z_A // DA_COLS)),
                  pl.BlockSpec((1, LANE), lambda i: (0, 0)),
                  pl.BlockSpec((1, LANE), lambda i: (0, 0))],
        out_specs=[pl.BlockSpec((2, bm, w), lambda i: (0, i, 0)),
                   pl.BlockSpec((bm, w), lambda i: (i, 0)),
                   pl.BlockSpec((bm, w), lambda i: (i, 0))],
        out_shape=[jax.ShapeDtypeStruct((2, m, w), BF), jax.ShapeDtypeStruct((m, w), BF),
                   jax.ShapeDtypeStruct((m, w), BF)],
        compiler_params=_cparams(("parallel",)),
        name="prep_a",
    )(z, g2(q_g), g2(k_g))


def mixer_a(z, s, lam, lam_init, q_g, k_g, o_g, bias_tiles):
    qz, kn, v = prep_a(z, q_g, k_g)
    return flash_attention(qz, kn, v, s, nmaps=2, dk=LANE, scale=1.0, bias=bias_tiles, lam=lam, gain=o_g,
                           post_scale=1.0 - lam_init, name="diff_attention")


def _prep_kr_kernel(z_ref, g_ref, c_ref, s_ref, o_ref):
    x = z_ref[...]
    ms = jnp.sum(x * x, axis=-1, keepdims=True) * (1.0 / MLA_ROPE)
    o_ref[...] = _rope_tile(x * lax.rsqrt(ms + EPS) * g_ref[...], c_ref[...], s_ref[...]).astype(o_ref.dtype)


def prep_kr(z, kr_g, cos_t, sin_t, s, bm=512):
    m = z.shape[0]
    bm = min(bm, s)
    nsb = s // bm
    g = jnp.concatenate([kr_g, jnp.zeros((LANE - MLA_ROPE,), kr_g.dtype)]).reshape(1, LANE).astype(F32)
    return pl.pallas_call(
        _prep_kr_kernel,
        grid=(m // bm,),
        in_specs=[pl.BlockSpec((bm, LANE), lambda i: (i, (Z_B + MLA_Q_RANK + MLA_KV_RANK) // LANE)),
                  pl.BlockSpec((1, LANE), lambda i: (0, 0)),
                  pl.BlockSpec((bm, LANE), lambda i: (i % nsb, 0)),
                  pl.BlockSpec((bm, LANE), lambda i: (i % nsb, 0))],
        out_specs=pl.BlockSpec((bm, LANE), lambda i: (i, 0)),
        out_shape=jax.ShapeDtypeStruct((m, LANE), BF),
        compiler_params=_cparams(("parallel",)),
        name="prep_kr",
    )(z, g, cos_t, sin_t)


def _prep_b_kernel(q_ref, kv_ref, kr_ref, qn_g, qr_g, kn_g, c_ref, s_ref, qc_ref, kc_ref, v_ref):
    def norm(x, g, width):
        ms = jnp.sum(x * x, axis=-1, keepdims=True) * (1.0 / width)
        return x * lax.rsqrt(ms + EPS) * g

    kr = kr_ref[...]
    for h in range(HEADS):
        c0 = slice(2 * h * LANE, (2 * h + 1) * LANE)
        c1 = slice((2 * h + 1) * LANE, (2 * h + 2) * LANE)
        qc_ref[:, c0] = norm(q_ref[:, c0], qn_g[...], MLA_NOPE).astype(BF)
        qc_ref[:, c1] = _rope_tile(norm(q_ref[:, c1], qr_g[...], MLA_ROPE), c_ref[...], s_ref[...]).astype(BF)
        kc_ref[:, c0] = norm(kv_ref[:, c0], kn_g[...], MLA_NOPE).astype(BF)
        kc_ref[:, c1] = kr
        v_ref[:, h * LANE:(h + 1) * LANE] = kv_ref[:, c1].astype(BF)


def prep_b(q_up, kv_up, kr, qn_g, qr_g, kn_g, cos_t, sin_t, s, bm=256):
    m = q_up.shape[0]
    bm = min(bm, s)
    nsb = s // bm
    w2 = 2 * HEADS * LANE
    row = lambda g: g.reshape(1, LANE).astype(F32)
    qr_pad = jnp.concatenate([qr_g, jnp.zeros((LANE - MLA_ROPE,), qr_g.dtype)])
    return pl.pallas_call(
        _prep_b_kernel,
        grid=(m // bm,),
        in_specs=[pl.BlockSpec((bm, w2), lambda i: (i, 0)),
                  pl.BlockSpec((bm, w2), lambda i: (i, 0)),
                  pl.BlockSpec((bm, LANE), lambda i: (i, 0)),
                  pl.BlockSpec((1, LANE), lambda i: (0, 0)),
                  pl.BlockSpec((1, LANE), lambda i: (0, 0)),
                  pl.BlockSpec((1, LANE), lambda i: (0, 0)),
                  pl.BlockSpec((bm, LANE), lambda i: (i % nsb, 0)),
                  pl.BlockSpec((bm, LANE), lambda i: (i % nsb, 0))],
        out_specs=[pl.BlockSpec((bm, w2), lambda i: (i, 0)),
                   pl.BlockSpec((bm, w2), lambda i: (i, 0)),
                   pl.BlockSpec((bm, HEADS * LANE), lambda i: (i, 0))],
        out_shape=[jax.ShapeDtypeStruct((m, w2), BF), jax.ShapeDtypeStruct((m, w2), BF),
                   jax.ShapeDtypeStruct((m, HEADS * LANE), BF)],
        compiler_params=_cparams(("parallel",)),
        name="prep_b",
    )(q_up, kv_up, kr, row(qn_g), row(qr_pad), row(kn_g), cos_t, sin_t)


def pack_w_uq(w):
    w = w.reshape(MLA_Q_RANK, HEADS, MLA_NOPE + MLA_ROPE)
    w = jnp.pad(w, ((0, 0), (0, 0), (0, 2 * LANE - MLA_NOPE - MLA_ROPE)))
    return w.reshape(MLA_Q_RANK, HEADS * 2 * LANE).astype(BF)


def mixer_b(z, s, w_uq, w_ukv, cq_g, ckv_g, qn_g, kn_g, qr_g, kr_g, cos_t, sin_t):
    c_q = rmsnorm_rows(z, cq_g, col_block=Z_B // MLA_Q_RANK)
    c_kv = rmsnorm_rows(z, ckv_g, col_block=(Z_B + MLA_Q_RANK) // MLA_KV_RANK)
    kr = prep_kr(z, kr_g, cos_t, sin_t, s)
    q_up = matmul(c_q, pack_w_uq(w_uq), F32, name="mla_q_up")
    kv_up = matmul(c_kv, w_ukv.astype(BF), F32, name="mla_kv_up")
    qc, kc, v = prep_b(q_up, kv_up, kr, qn_g, qr_g, kn_g, cos_t, sin_t, s)
    return flash_attention(qc[None], kc, v, s, nmaps=1, dk=2 * LANE, scale=(MLA_NOPE + MLA_ROPE) ** -0.5,
                           name="mla_attention")


DIL_PAD = DIL_STEPS * max(d for _, d in DIL_GROUPS)


def _dil_kernel(zq_ref, zk_ref, zv_ref, bias_ref, qg_ref, kg_ref, o_ref, q_sc, k_sc, v_sc, og_sc, lse_sc, *, s):
    g = pl.program_id(2)
    ngroups = len(DIL_GROUPS)

    def norm(x, gain):
        ms = jnp.mean(x * x, axis=-1, keepdims=True)
        return x * lax.rsqrt(ms + EPS) * gain

    zeros = jnp.zeros((DIL_PAD, LANE), F32)
    k_sc[0:DIL_PAD, :] = zeros
    v_sc[0:DIL_PAD, :] = zeros
    q_sc[...] = norm(zq_ref[...], qg_ref[...])
    k_sc[DIL_PAD:, :] = norm(zk_ref[...], kg_ref[...])
    v_sc[DIL_PAD:, :] = zv_ref[...]
    scale = LANE ** -0.5

    def group(gi, dil):
        nsub = s // DIL_STEPS

        def body(tt, carry):
            c = tt % dil
            n = tt // dil
            q0 = c + dil * DIL_STEPS * n
            rows = pl.ds(q0, DIL_STEPS, stride=dil) if dil > 1 else pl.ds(q0, DIL_STEPS)
            k0 = q0 + DIL_PAD - dil * DIL_STEPS
            band = pl.ds(k0, 2 * DIL_STEPS, stride=dil) if dil > 1 else pl.ds(k0, 2 * DIL_STEPS)
            qs = q_sc[rows, :].astype(BF)
            ks = k_sc[band, :].astype(BF)
            vs = v_sc[band, :].astype(BF)
            first = (n == 0).astype(jnp.int32)
            logits = lax.dot_general(qs, ks, (((1,), (1,)), ((), ())), preferred_element_type=F32) * scale
            logits = logits + bias_ref[first]
            mx = jnp.max(logits, axis=-1, keepdims=True)
            e = jnp.exp(logits - mx)
            den = jnp.sum(e, axis=-1, keepdims=True)
            o = jnp.dot((e / den).astype(BF), vs, preferred_element_type=F32)
            og_sc[gi, rows, :] = o
            lse_sc[gi, rows, :] = jnp.broadcast_to(mx + jnp.log(den), (DIL_STEPS, LANE))
            return carry

        lax.fori_loop(0, nsub, body, 0)

    for gi, (_, dil) in enumerate(DIL_GROUPS):
        pl.when(g == gi)(functools.partial(group, gi, dil))

    @pl.when(g == ngroups - 1)
    def _():
        lses = [lse_sc[gi] for gi in range(ngroups)]
        mx = functools.reduce(jnp.maximum, lses)
        ws = [jnp.exp(l - mx) for l in lses]
        tot = functools.reduce(lambda a, b2: a + b2, ws)
        acc = ws[0] * og_sc[0]
        for gi in range(1, ngroups):
            acc = acc + ws[gi] * og_sc[gi]
        o_ref[...] = (acc / tot).astype(o_ref.dtype)


def mixer_c(z, s, q_g, k_g, bias_c):
    m = z.shape[0]
    b = m // s
    ngroups = len(DIL_GROUPS)
    cb = Z_C // LANE

    def col(which):
        return lambda bb, h, g: (bb, cb + (g * 3 + which) * HEADS + h)

    row = lambda g: g.reshape(1, LANE).astype(F32)
    return pl.pallas_call(
        functools.partial(_dil_kernel, s=s),
        grid=(b, HEADS, ngroups),
        in_specs=[pl.BlockSpec((s, LANE), col(0)),
                  pl.BlockSpec((s, LANE), col(1)),
                  pl.BlockSpec((s, LANE), col(2)),
                  pl.BlockSpec((None, None, 2, DIL_STEPS, 2 * DIL_STEPS), lambda bb, h, g: (g, h, 0, 0, 0)),
                  pl.BlockSpec((1, LANE), lambda bb, h, g: (0, 0)),
                  pl.BlockSpec((1, LANE), lambda bb, h, g: (0, 0))],
        out_specs=pl.BlockSpec((s, LANE), lambda bb, h, g: (bb, h)),
        out_shape=jax.ShapeDtypeStruct((m, HEADS * LANE), BF),
        scratch_shapes=[pltpu.VMEM((s, LANE), F32), pltpu.VMEM((DIL_PAD + s, LANE), F32),
                        pltpu.VMEM((DIL_PAD + s, LANE), F32), pltpu.VMEM((ngroups, s, LANE), F32),
                        pltpu.VMEM((ngroups, s, LANE), F32)],
        compiler_params=_cparams(("parallel", "parallel", "arbitrary")),
        name="dilated_attention",
    )(z, z, z, bias_c, row(q_g), row(k_g))


CONV_PAD = 8
HI = lax.Precision.HIGHEST


def _gdn_kernel(par_ref, zq_ref, zk_ref, zv_ref, zg_ref, zab_ref, wq_ref, wk_ref, wv_ref, og_ref, o_ref,
                x_sc, q_sc, k_sc, v_sc, g_sc, b_sc, st_sc, *, s):
    h = pl.program_id(1)
    c = GDN_CHUNK

    def conv_silu(z_ref, w_ref):
        x_sc[0:CONV_PAD, :] = jnp.zeros((CONV_PAD, LANE), F32)
        x_sc[CONV_PAD:, :] = z_ref[...]
        y = x_sc[CONV_PAD:, :] * w_ref[CONV_W - 1:CONV_W, :]
        for i in range(CONV_W - 1):
            off = CONV_PAD - (CONV_W - 1) + i
            y = y + x_sc[off:off + s, :] * w_ref[i:i + 1, :]
        return y * jax.nn.sigmoid(y)

    def l2(x):
        return x * lax.rsqrt(jnp.sum(x * x, axis=-1, keepdims=True) + EPS)

    q_sc[...] = l2(conv_silu(zq_ref, wq_ref)) * (LANE ** -0.5)
    k_sc[...] = l2(conv_silu(zk_ref, wk_ref))
    v_sc[...] = conv_silu(zv_ref, wv_ref)
    lane = lax.broadcasted_iota(jnp.int32, (1, LANE), 1)
    ab = zab_ref[...]
    a_col = jnp.sum(jnp.where(lane == h, ab, 0.0), axis=-1, keepdims=True)
    b_col = jnp.sum(jnp.where(lane == h + HEADS, ab, 0.0), axis=-1, keepdims=True)
    a_neg_exp = par_ref[0, h]
    dt_bias = par_ref[1, h]
    g_sc[...] = jnp.broadcast_to(a_neg_exp * jax.nn.softplus(a_col + dt_bias), (s, LANE))
    b_sc[...] = jnp.broadcast_to(jax.nn.sigmoid(b_col), (s, LANE))
    st_sc[...] = jnp.zeros(st_sc.shape, F32)

    r_i = lax.broadcasted_iota(jnp.int32, (c, c), 0)
    c_i = lax.broadcasted_iota(jnp.int32, (c, c), 1)
    tril = r_i >= c_i
    strict = r_i > c_i
    tril_f = tril.astype(F32)
    eye = (r_i == c_i).astype(F32)
    lane0 = (lax.broadcasted_iota(jnp.int32, (c, LANE), 1) == 0).astype(F32)

    def mm(a, b2):
        return jnp.dot(a.astype(BF), b2.astype(BF), preferred_element_type=F32)

    def mm_t(a, b2):
        return lax.dot_general(a.astype(BF), b2.astype(BF), (((1,), (1,)), ((), ())), preferred_element_type=F32)

    def chunk(n, carry):
        rows = pl.ds(pl.multiple_of(n * c, c), c)
        q = q_sc[rows, :]
        k = k_sc[rows, :]
        v = v_sc[rows, :]
        beta = b_sc[rows, :]
        gc = jnp.dot(tril_f, g_sc[rows, :], preferred_element_type=F32, precision=HI)
        g_row = lax.dot_general(lane0, gc, (((1,), (1,)), ((), ())), preferred_element_type=F32,
                                precision=HI)
        decay = jnp.exp(jnp.where(tril, gc[:, :c] - g_row, NEG))
        kb = k * beta
        vb = v * beta
        lower = jnp.where(strict, mm_t(kb, k) * decay, 0.0)
        pw = -lower
        tmat = eye + pw
        for _ in range(5):
            pw = jnp.dot(pw, pw, preferred_element_type=F32, precision=HI)
            tmat = tmat + jnp.dot(tmat, pw, preferred_element_type=F32, precision=HI)
        eg = jnp.exp(gc)
        u = mm(tmat, vb)
        w = mm(tmat, kb * eg)
        g_last = gc[c - 1:c, :]
        qg = q * eg
        kdec = k * jnp.exp(g_last - gc)
        intra = mm_t(q, k) * decay
        state = st_sc[...]
        v_new = u - mm(w, state)
        o = mm(qg, state) + mm(intra, v_new)
        st_sc[...] = state * jnp.exp(g_last) + lax.dot_general(
            kdec.astype(BF), v_new.astype(BF), (((0,), (0,)), ((), ())), preferred_element_type=F32)
        ms = jnp.mean(o * o, axis=-1, keepdims=True)
        gate = zg_ref[rows, :]
        o_ref[rows, :] = (o * lax.rsqrt(ms + EPS) * og_ref[...] * (gate * jax.nn.sigmoid(gate))).astype(o_ref.dtype)
        return carry

    lax.fori_loop(0, s // c, chunk, 0)


def mixer_d(z, s, conv_w, a_log, dt_bias, o_g):
    m = z.shape[0]
    b = m // s
    cb = Z_D // LANE
    par = jnp.stack([-jnp.exp(a_log.astype(F32)), dt_bias.astype(F32)], axis=0)
    cw = conv_w.astype(F32)

    def col(which):
        return lambda bb, h: (bb, cb + which * HEADS + h)

    return pl.pallas_call(
        functools.partial(_gdn_kernel, s=s),
        grid=(b, HEADS),
        in_specs=[pl.BlockSpec(memory_space=pltpu.SMEM),
                  pl.BlockSpec((s, LANE), col(0)),
                  pl.BlockSpec((s, LANE), col(1)),
                  pl.BlockSpec((s, LANE), col(2)),
                  pl.BlockSpec((s, LANE), col(3)),
                  pl.BlockSpec((s, LANE), lambda bb, h: (bb, cb + 4 * HEADS)),
                  pl.BlockSpec((CONV_W, LANE), lambda bb, h: (0, h)),
                  pl.BlockSpec((CONV_W, LANE), lambda bb, h: (0, HEADS + h)),
                  pl.BlockSpec((CONV_W, LANE), lambda bb, h: (0, 2 * HEADS + h)),
                  pl.BlockSpec((1, LANE), lambda bb, h: (0, 0))],
        out_specs=pl.BlockSpec((s, LANE), lambda bb, h: (bb, h)),
        out_shape=jax.ShapeDtypeStruct((m, HEADS * LANE), BF),
        scratch_shapes=[pltpu.VMEM((CONV_PAD + s, LANE), F32), pltpu.VMEM((s, LANE), F32),
                        pltpu.VMEM((s, LANE), F32), pltpu.VMEM((s, LANE), F32), pltpu.VMEM((s, LANE), F32),
                        pltpu.VMEM((s, LANE), F32), pltpu.VMEM((LANE, LANE), F32)],
        compiler_params=_cparams(("parallel", "arbitrary")),
        name="gated_deltanet",
    )(par, z, z, z, z, z, cw, cw, cw, o_g.reshape(1, LANE).astype(F32))


def pack_w_in(w):
    d = w.shape[0]
    c0 = DA_COLS + MLA_COLS + DIL_COLS
    qkv = w[:, c0:c0 + GDN_QKV]
    ab = w[:, c0 + GDN_QKV:c0 + GDN_QKV + 2 * HEADS]
    gate = w[:, c0 + GDN_QKV + 2 * HEADS:]
    used = Z_D + GDN_QKV + HEADS * LANE + 2 * HEADS
    parts = [w[:, :DA_COLS + MLA_COLS], jnp.zeros((d, 64), w.dtype), w[:, DA_COLS + MLA_COLS:c0],
             qkv, gate, ab, jnp.zeros((d, Z_COLS - used), w.dtype)]
    return jnp.concatenate(parts, axis=1).astype(BF)


def kernel(x, p, rel_bias, norm_mix_g, w_in, da_lambda, da_q_g, da_k_g, da_o_g, mla_w_uq, mla_w_ukv, mla_cq_g, mla_ckv_g, mla_qn_g, mla_kn_g, mla_qr_g, mla_kr_g, dil_q_g, dil_k_g, gdn_conv_w, gdn_a_log, gdn_dt_bias, gdn_o_g, w_bgate, b_bgate, w_branch, w_out, norm_ffn_g, w_ffn_in, w_ffn_out, norm_ple_g, w_ple, w_ple_gate):
    b, s, d = x.shape
    m = b * s
    bias_a = diff_bias_tiles(rel_bias, s)
    bias_c = dil_bias_tiles(rel_bias)
    cos_t, sin_t = rope_tables(s)
    xr = x.reshape(m, d)
    for i in range(DEPTH):
        h = rmsnorm_rows(xr, norm_mix_g[i])
        z = matmul(h, pack_w_in(w_in[i]), F32, name="in_proj")
        lam_init = 0.8 - 0.6 * math.exp(-0.3 * i)
        lq1, lk1, lq2, lk2 = (da_lambda[i, j].astype(F32) for j in range(4))
        lam = jnp.exp(jnp.sum(lq1 * lk1)) - jnp.exp(jnp.sum(lq2 * lk2)) + lam_init
        o_a = mixer_a(z, s, lam, lam_init, da_q_g[i], da_k_g[i], da_o_g[i], bias_a)
        o_b = mixer_b(z, s, mla_w_uq[i], mla_w_ukv[i], mla_cq_g[i], mla_ckv_g[i], mla_qn_g[i], mla_kn_g[i],
                      mla_qr_g[i], mla_kr_g[i], cos_t, sin_t)
        o_c = mixer_c(z, s, dil_q_g[i], dil_k_g[i], bias_c)
        o_d = mixer_d(z, s, gdn_conv_w[i], gdn_a_log[i], gdn_dt_bias[i], gdn_o_g[i])
        o_all = jnp.stack([o_a, o_b, o_c, o_d], axis=0)
        merged = gated_merge(h, w_bgate[i].astype(BF), b_bgate[i].reshape(N_BRANCH, 1, d), o_all,
                             w_branch[i].astype(BF))
        xr = matmul_residual(xr, merged, w_out[i].astype(BF), name="out_proj")
        hf = rmsnorm_rows(xr, norm_ffn_g[i])
        wf = w_ffn_in[i]
        padc = ((0, 0), (0, D_FF_PAD - D_FF))
        act = swiglu_in(hf, jnp.pad(wf[:, :D_FF], padc).astype(BF), jnp.pad(wf[:, D_FF:], padc).astype(BF))
        w2 = jnp.pad(w_ffn_out[i], ((0, D_FF_PAD - D_FF), (0, 0))).astype(BF)
        xr = matmul_residual(xr, act, w2, bm=1024, bn=512, bk=D_FF_PAD // 2, name="ffn_out")
        hp = rmsnorm_rows(xr, norm_ple_g[i])
        xr = ple_update(xr, p[i].reshape(m, PLE_DIM).astype(BF), w_ple[i].astype(BF), hp, w_ple_gate[i].astype(BF))
    return xr.reshape(b, s, d)
```

```python
import functools
import math

import numpy as np
import jax
import jax.numpy as jnp
from jax import lax
from jax.experimental import pallas as pl
from jax.experimental.pallas import tpu as pltpu

D_MODEL = 4096
DEPTH = 2
EPS = 1e-6
PLE_DIM = 256
HEADS = 8
N_BRANCH = 4
BRANCH_W = 1024
DA_QK_DIM = 64
MLA_Q_RANK = 1024
MLA_KV_RANK = 512
MLA_NOPE = 128
MLA_ROPE = 64
ROPE_THETA = 10000.0
DIL_GROUPS = ((128, 1), (512, 4), (2048, 16))
DIL_STEPS = 128
GDN_CHUNK = 64
CONV_W = 4
N_BUCKETS = 32
MAX_DIST = 2048
D_FF = 11008

DA_COLS = 3072
MLA_COLS = 1600
DIL_COLS = 9216
GDN_QKV = 3072

LANE = 128
Z_A = 0
Z_B = Z_A + DA_COLS
Z_C = Z_B + MLA_COLS + 64
Z_D = Z_C + DIL_COLS
Z_COLS = 18 * 1024

NEG = -1e30
ATT_T = 256
VMEM_LIMIT = 56 * 1024 * 1024
BF = jnp.bfloat16
F32 = jnp.float32


def _cparams(sem):
    return pltpu.CompilerParams(dimension_semantics=sem, vmem_limit_bytes=VMEM_LIMIT)


def _rmsnorm_kernel(x_ref, g_ref, o_ref):
    x = x_ref[...]
    ms = jnp.mean(x * x, axis=-1, keepdims=True)
    o_ref[...] = (x * lax.rsqrt(ms + EPS) * g_ref[...]).astype(o_ref.dtype)


def rmsnorm_rows(x, g, col_block=0, bm=512):
    m = x.shape[0]
    d = g.shape[-1]
    return pl.pallas_call(
        _rmsnorm_kernel,
        grid=(m // bm,),
        in_specs=[pl.BlockSpec((bm, d), lambda i: (i, col_block)),
                  pl.BlockSpec((1, d), lambda i: (0, 0))],
        out_specs=pl.BlockSpec((bm, d), lambda i: (i, 0)),
        out_shape=jax.ShapeDtypeStruct((m, d), BF),
        compiler_params=_cparams(("parallel",)),
        name="rmsnorm_rows",
    )(x, g.reshape(1, d))


def _mm_kernel(x_ref, w_ref, o_ref):
    o_ref[...] = jnp.dot(x_ref[...], w_ref[...], preferred_element_type=F32).astype(o_ref.dtype)


def matmul(x, w, out_dtype, bm=1024, bn=1024, name="matmul"):
    m, k = x.shape
    _, n = w.shape
    bn = min(bn, n)
    bm = min(bm, m)
    return pl.pallas_call(
        _mm_kernel,
        grid=(m // bm, n // bn),
        in_specs=[pl.BlockSpec((bm, k), lambda i, j: (i, 0)),
                  pl.BlockSpec((k, bn), lambda i, j: (0, j))],
        out_specs=pl.BlockSpec((bm, bn), lambda i, j: (i, j)),
        out_shape=jax.ShapeDtypeStruct((m, n), out_dtype),
        compiler_params=_cparams(("parallel", "parallel")),
        name=name,
    )(x, w)


def _mm_residual_kernel(r_ref, x_ref, w_ref, o_ref, acc_ref, *, nk):
    k = pl.program_id(2)
    part = jnp.dot(x_ref[...], w_ref[...], preferred_element_type=F32)

    @pl.when(k == 0)
    def _():
        acc_ref[...] = r_ref[...] + part

    @pl.when(k > 0)
    def _():
        acc_ref[...] += part

    @pl.when(k == nk - 1)
    def _():
        o_ref[...] = acc_ref[...]


def matmul_residual(r, x, w, bm=1024, bn=512, bk=None, name="matmul_residual"):
    m, kdim = x.shape
    _, n = w.shape
    bk = kdim if bk is None else bk
    bm = min(bm, m)
    nk = kdim // bk
    return pl.pallas_call(
        functools.partial(_mm_residual_kernel, nk=nk),
        grid=(m // bm, n // bn, nk),
        in_specs=[pl.BlockSpec((bm, bn), lambda i, j, k: (i, j)),
                  pl.BlockSpec((bm, bk), lambda i, j, k: (i, k)),
                  pl.BlockSpec((bk, bn), lambda i, j, k: (k, j))],
        out_specs=pl.BlockSpec((bm, bn), lambda i, j, k: (i, j)),
        out_shape=jax.ShapeDtypeStruct((m, n), F32),
        scratch_shapes=[pltpu.VMEM((bm, bn), F32)],
        compiler_params=_cparams(("parallel", "parallel", "arbitrary")),
        name=name,
    )(r, x, w)


def _merge_kernel(h_ref, wg_ref, b_ref, o_ref, wb_ref, out_ref, acc_ref):
    n = pl.program_id(2)
    gate = jax.nn.sigmoid(jnp.dot(h_ref[...], wg_ref[...], preferred_element_type=F32) + b_ref[...])
    term = gate * jnp.dot(o_ref[...], wb_ref[...], preferred_element_type=F32)

    @pl.when(n == 0)
    def _():
        acc_ref[...] = term

    @pl.when(n > 0)
    def _():
        acc_ref[...] += term

    @pl.when(n == N_BRANCH - 1)
    def _():
        out_ref[...] = acc_ref[...].astype(out_ref.dtype)


def gated_merge(h, w_bgate, b_bgate, o_all, w_branch, bm=1024, bn=512):
    m, d = h.shape
    bm = min(bm, m)
    return pl.pallas_call(
        _merge_kernel,
        grid=(m // bm, d // bn, N_BRANCH),
        in_specs=[pl.BlockSpec((bm, d), lambda i, j, n: (i, 0)),
                  pl.BlockSpec((None, d, bn), lambda i, j, n: (n, 0, j)),
                  pl.BlockSpec((None, 1, bn), lambda i, j, n: (n, 0, j)),
                  pl.BlockSpec((None, bm, BRANCH_W), lambda i, j, n: (n, i, 0)),
                  pl.BlockSpec((None, BRANCH_W, bn), lambda i, j, n: (n, 0, j))],
        out_specs=pl.BlockSpec((bm, bn), lambda i, j, n: (i, j)),
        out_shape=jax.ShapeDtypeStruct((m, d), BF),
        scratch_shapes=[pltpu.VMEM((bm, bn), F32)],
        compiler_params=_cparams(("parallel", "parallel", "arbitrary")),
        name="gated_merge",
    )(h, w_bgate, b_bgate, o_all, w_branch)


def _swiglu_kernel(x_ref, wg_ref, wu_ref, o_ref):
    x = x_ref[...]
    g = jnp.dot(x, wg_ref[...], preferred_element_type=F32)
    u = jnp.dot(x, wu_ref[...], preferred_element_type=F32)
    o_ref[...] = (g * jax.nn.sigmoid(g) * u).astype(o_ref.dtype)


def swiglu_in(x, wg, wu, bm=2048, bn=256):
    m, k = x.shape
    _, n = wg.shape
    bm = min(bm, m)
    return pl.pallas_call(
        _swiglu_kernel,
        grid=(m // bm, n // bn),
        in_specs=[pl.BlockSpec((bm, k), lambda i, j: (i, 0)),
                  pl.BlockSpec((k, bn), lambda i, j: (0, j)),
                  pl.BlockSpec((k, bn), lambda i, j: (0, j))],
        out_specs=pl.BlockSpec((bm, bn), lambda i, j: (i, j)),
        out_shape=jax.ShapeDtypeStruct((m, n), BF),
        compiler_params=_cparams(("parallel", "parallel")),
        name="swiglu_in",
    )(x, wg, wu)


def _ple_kernel(r_ref, p_ref, wp_ref, h_ref, wg_ref, o_ref):
    e = jnp.dot(p_ref[...], wp_ref[...], preferred_element_type=F32)
    g = jnp.dot(h_ref[...], wg_ref[...], preferred_element_type=F32)
    o_ref[...] = r_ref[...] + e * jax.nn.sigmoid(g)


def ple_update(r, p, w_ple, hp, w_gate, bm=1024, bn=512):
    m, d = r.shape
    kp = p.shape[1]
    bm = min(bm, m)
    return pl.pallas_call(
        _ple_kernel,
        grid=(m // bm, d // bn),
        in_specs=[pl.BlockSpec((bm, bn), lambda i, j: (i, j)),
                  pl.BlockSpec((bm, kp), lambda i, j: (i, 0)),
                  pl.BlockSpec((kp, bn), lambda i, j: (0, j)),
                  pl.BlockSpec((bm, d), lambda i, j: (i, 0)),
                  pl.BlockSpec((d, bn), lambda i, j: (0, j))],
        out_specs=pl.BlockSpec((bm, bn), lambda i, j: (i, j)),
        out_shape=jax.ShapeDtypeStruct((m, d), F32),
        compiler_params=_cparams(("parallel", "parallel")),
        name="ple_update",
    )(r, p, w_ple, hp, w_gate)


def _static_buckets(dist):
    max_exact = N_BUCKETS // 2
    d = np.maximum(np.asarray(dist), 0)
    large = max_exact + (np.log(np.maximum(d, 1).astype(np.float32) / np.float32(max_exact))
                         / np.float32(math.log(MAX_DIST / max_exact))
                         * np.float32(N_BUCKETS - max_exact)).astype(np.int32)
    large = np.minimum(large, N_BUCKETS - 1)
    return np.where(d < max_exact, d, large).astype(np.int32)


def _toeplitz_tiles(f, t):
    hh, n = f.shape
    big = n + t - 1
    fpad = jnp.concatenate([jnp.full((hh, t - 1), NEG, f.dtype), f], axis=1)
    flat = jnp.tile(fpad, (1, t + 1))[:, :t * (big + 1)]
    w = flat.reshape(hh, t, big + 1)[:, :, :n]
    w = jnp.flip(w.reshape(hh, t, n // t, t), axis=3)
    return w.transpose(0, 2, 1, 3)


def _bias_tile_kernel(prev_ref, cur_ref, o_ref, *, t):
    row = jnp.concatenate([prev_ref[...], cur_ref[...]], axis=1)
    rolled = pltpu.roll(jnp.broadcast_to(row, (t, 2 * t)), 0, 1, stride=1, stride_axis=0)
    o_ref[...] = rolled[:, t:]


def diff_bias_tiles(rel_bias, s):
    t = min(ATT_T, s)
    nq = s // t
    f = jnp.take(rel_bias[:, :HEADS], _static_buckets(np.arange(s)), axis=0).T.astype(F32)
    fb = jnp.concatenate([jnp.full((HEADS, t), NEG, F32), f], axis=1).reshape(HEADS, nq + 1, 1, t)
    return pl.pallas_call(
        functools.partial(_bias_tile_kernel, t=t),
        grid=(HEADS, nq),
        in_specs=[pl.BlockSpec((None, None, 1, t), lambda h, d: (h, d, 0, 0)),
                  pl.BlockSpec((None, None, 1, t), lambda h, d: (h, d + 1, 0, 0))],
        out_specs=pl.BlockSpec((None, None, t, t), lambda h, d: (h, d, 0, 0)),
        out_shape=jax.ShapeDtypeStruct((HEADS, nq, t, t), F32),
        compiler_params=_cparams(("parallel", "parallel")),
        name="diff_bias_tiles",
    )(fb, fb)


def dil_bias_tiles(rel_bias):
    out = []
    for gi, (_, dil) in enumerate(DIL_GROUPS):
        lo = HEADS + gi * HEADS
        steps = np.arange(2 * DIL_STEPS)
        f = jnp.take(rel_bias[:, lo:lo + HEADS], _static_buckets(steps * dil), axis=0).T.astype(F32)
        f = jnp.where(steps[None, :] <= DIL_STEPS, f, NEG)
        tiles = _toeplitz_tiles(f, DIL_STEPS)
        general = jnp.concatenate([tiles[:, 1], tiles[:, 0]], axis=-1)
        first = jnp.concatenate([jnp.full_like(tiles[:, 1], NEG), tiles[:, 0]], axis=-1)
        out.append(jnp.stack([general, first], axis=1))
    return jnp.stack(out, axis=0)


def rope_tables(s):
    half = MLA_ROPE // 2
    inv = ROPE_THETA ** (-jnp.arange(half, dtype=F32) / half)
    ang = jnp.arange(s).astype(F32)[:, None] * inv[None, :]
    cos, sin = jnp.cos(ang), jnp.sin(ang)
    zero = jnp.zeros((s, LANE - MLA_ROPE), F32)
    return (jnp.concatenate([cos, cos, zero], axis=1), jnp.concatenate([-sin, sin, zero], axis=1))


def _rope_tile(t, c, s):
    half = MLA_ROPE // 2
    swapped = pltpu.roll(t, half, axis=1) + pltpu.roll(t, LANE - half, axis=1)
    return t * c + swapped * s


def _flash_kernel(lam_ref, qt_ref, k_ref, vt_ref, *rest, diff, t, scale, post_scale):
    if diff:
        bias_ref, g_ref, o_ref, m_sc, l_sc, acc_sc = rest
    else:
        o_ref, m_sc, l_sc, acc_sc = rest
    i = pl.program_id(2)
    m_sc[...] = jnp.full(m_sc.shape, NEG, F32)
    l_sc[...] = jnp.zeros(l_sc.shape, F32)
    acc_sc[...] = jnp.zeros(acc_sc.shape, F32)
    qts = [qt_ref[0], qt_ref[1]] if diff else [qt_ref[:, 0:t], qt_ref[:, t:2 * t]]

    def load_kv(j):
        start = pl.multiple_of(j * t, t)
        return k_ref[pl.ds(start, t), :], vt_ref[:, pl.ds(start, t)]

    def step(work):
        scores = [jnp.dot(kv[0], qts[c], preferred_element_type=F32) for c, kv, _, _ in work]
        for (c, kv, bias, diagonal), s in zip(work, scores):
            if scale != 1.0:
                s = s * scale
            if bias is not None:
                s = bias + s
            elif diagonal:
                key_i = lax.broadcasted_iota(jnp.int32, (t, t), 0)
                qry_i = lax.broadcasted_iota(jnp.int32, (t, t), 1)
                s = jnp.where(key_i <= qry_i, s, NEG)
            m_prev = m_sc[c]
            m_new = jnp.maximum(m_prev, jnp.max(s, axis=0, keepdims=True))
            alpha = jnp.exp(m_prev - m_new)
            p = jnp.exp(s - m_new)
            l_sc[c] = alpha * l_sc[c] + jnp.sum(p, axis=0, keepdims=True)
            m_sc[c] = m_new
            acc_sc[c] = alpha * acc_sc[c] + jnp.dot(kv[1], p.astype(BF), preferred_element_type=F32)

    def sweep(n, work):
        def body(jj, carry):
            step(work(4 * jj) + work(4 * jj + 1) + work(4 * jj + 2) + work(4 * jj + 3))
            return carry

        lax.fori_loop(0, n // 4, body, 0)
        done = (n // 4) * 4

        @pl.when(n % 4 >= 2)
        def _():
            step(work(done) + work(done + 1))

        @pl.when(n % 2 == 1)
        def _():
            step(work(n - 1))

    if diff:
        def work(j):
            kv = load_kv(j)
            bias = bias_ref[i - j]
            return [(0, kv, bias, False), (1, kv, bias, False)]

        sweep(i + 1, work)
        out_t = acc_sc[0] / l_sc[0] - lam_ref[0] * (acc_sc[1] / l_sc[1])
        o = out_t.T
        ms = jnp.mean(o * o, axis=-1, keepdims=True)
        o_ref[...] = (o * lax.rsqrt(ms + EPS) * g_ref[...] * post_scale).astype(o_ref.dtype)
    else:
        def work(j):
            kv = load_kv(j)
            return [(0, kv, None, False), (1, kv, None, False)]

        sweep(2 * i, work)
        kv = load_kv(2 * i)
        step([(0, kv, None, True), (1, kv, None, False)])
        step([(1, load_kv(2 * i + 1), None, True)])
        o_ref[0:t, :] = (acc_sc[0] / l_sc[0]).T.astype(o_ref.dtype)
        o_ref[t:2 * t, :] = (acc_sc[1] / l_sc[1]).T.astype(o_ref.dtype)


def flash_attention(q, k, vt, s, *, diff, dk, scale, bias=None, lam=None, gain=None, post_scale=1.0, name):
    m = k.shape[0]
    b = m // s
    t = min(ATT_T, s)
    lam = jnp.zeros((1,), F32) if lam is None else lam.reshape(1).astype(F32)
    if diff:
        nq = s // t
        q_spec = pl.BlockSpec((2, dk, t), lambda bb, h, i: (0, h, bb * nq + i))
        o_rows = t
    else:
        nq = s // (2 * t)
        q_spec = pl.BlockSpec((dk, 2 * t), lambda bb, h, i: (h, bb * nq + i))
        o_rows = 2 * t
    in_specs = [pl.BlockSpec(memory_space=pltpu.SMEM), q_spec,
                pl.BlockSpec((s, dk), lambda bb, h, i: (bb, h)),
                pl.BlockSpec((LANE, s), lambda bb, h, i: (h, bb))]
    args = [lam, q, k, vt]
    if diff:
        in_specs += [pl.BlockSpec((None, nq, t, t), lambda bb, h, i: (h, 0, 0, 0)),
                     pl.BlockSpec((1, LANE), lambda bb, h, i: (0, 0))]
        args += [bias, gain.reshape(1, LANE).astype(F32)]
    return pl.pallas_call(
        functools.partial(_flash_kernel, diff=diff, t=t, scale=scale, post_scale=post_scale),
        grid=(b, HEADS, nq),
        in_specs=in_specs,
        out_specs=pl.BlockSpec((o_rows, LANE), lambda bb, h, i: (bb * nq + i, h)),
        out_shape=jax.ShapeDtypeStruct((m, HEADS * LANE), BF),
        scratch_shapes=[pltpu.VMEM((2, 1, t), F32), pltpu.VMEM((2, 1, t), F32), pltpu.VMEM((2, LANE, t), F32)],
        compiler_params=_cparams(("parallel", "parallel", "arbitrary")),
        name=name,
    )(*args)


def _prep_a_kernel(z_ref, qg_ref, kg_ref, qz_ref, kn_ref, vt_ref):
    lane = lax.broadcasted_iota(jnp.int32, (1, LANE), 1)
    lo = lane < DA_QK_DIM

    def norm_halves(x, g):
        x2 = x * x
        s_lo = jnp.sum(jnp.where(lo, x2, 0.0), axis=-1, keepdims=True)
        s_hi = jnp.sum(jnp.where(lo, 0.0, x2), axis=-1, keepdims=True)
        inv = lax.rsqrt(jnp.where(lo, s_lo, s_hi) * (1.0 / DA_QK_DIM) + EPS)
        return x * inv * g

    scale = DA_QK_DIM ** -0.5
    for h in range(HEADS):
        cq = slice(h * LANE, (h + 1) * LANE)
        ck = slice(HEADS * LANE + h * LANE, HEADS * LANE + (h + 1) * LANE)
        q = norm_halves(z_ref[:, cq], qg_ref[...]) * scale
        qt = q.T
        sub_lo = lax.broadcasted_iota(jnp.int32, (LANE, 1), 0) < DA_QK_DIM
        qz_ref[0, cq, :] = jnp.where(sub_lo, qt, 0.0).astype(BF)
        qz_ref[1, cq, :] = jnp.where(sub_lo, 0.0, qt).astype(BF)
        kn_ref[:, cq] = norm_halves(z_ref[:, ck], kg_ref[...]).astype(BF)
        cv = slice(2 * HEADS * LANE + h * LANE, 2 * HEADS * LANE + (h + 1) * LANE)
        vt_ref[cq, :] = z_ref[:, cv].T.astype(BF)


def prep_a(z, q_g, k_g, bm=256):
    m = z.shape[0]
    bm = min(bm, m)
    w = HEADS * LANE
    g2 = lambda g: jnp.concatenate([g, g]).reshape(1, LANE).astype(F32)
    return pl.pallas_call(
        _prep_a_kernel,
        grid=(m // bm,),
        in_specs=[pl.BlockSpec((bm, DA_COLS), lambda i: (i, Z_A // DA_COLS)),
                  pl.BlockSpec((1, LANE), lambda i: (0, 0)),
                  pl.BlockSpec((1, LANE), lambda i: (0, 0))],
        out_specs=[pl.BlockSpec((2, w, bm), lambda i: (0, 0, i)),
                   pl.BlockSpec((bm, w), lambda i: (i, 0)),
                   pl.BlockSpec((w, bm), lambda i: (0, i))],
        out_shape=[jax.ShapeDtypeStruct((2, w, m), BF), jax.ShapeDtypeStruct((m, w), BF),
                   jax.ShapeDtypeStruct((w, m), BF)],
        compiler_params=_cparams(("parallel",)),
        name="prep_a",
    )(z, g2(q_g), g2(k_g))


def mixer_a(z, s, lam, lam_init, q_g, k_g, o_g, bias_tiles):
    qz, kn, vt = prep_a(z, q_g, k_g)
    return flash_attention(qz, kn, vt, s, diff=True, dk=LANE, scale=1.0, bias=bias_tiles, lam=lam, gain=o_g,
                           post_scale=1.0 - lam_init, name="diff_attention")


def _prep_kr_kernel(z_ref, g_ref, c_ref, s_ref, o_ref):
    x = z_ref[...]
    ms = jnp.sum(x * x, axis=-1, keepdims=True) * (1.0 / MLA_ROPE)
    o_ref[...] = _rope_tile(x * lax.rsqrt(ms + EPS) * g_ref[...], c_ref[...], s_ref[...]).astype(o_ref.dtype)


def prep_kr(z, kr_g, cos_t, sin_t, s, bm=512):
    m = z.shape[0]
    bm = min(bm, s)
    nsb = s // bm
    g = jnp.concatenate([kr_g, jnp.zeros((LANE - MLA_ROPE,), kr_g.dtype)]).reshape(1, LANE).astype(F32)
    return pl.pallas_call(
        _prep_kr_kernel,
        grid=(m // bm,),
        in_specs=[pl.BlockSpec((bm, LANE), lambda i: (i, (Z_B + MLA_Q_RANK + MLA_KV_RANK) // LANE)),
                  pl.BlockSpec((1, LANE), lambda i: (0, 0)),
                  pl.BlockSpec((bm, LANE), lambda i: (i % nsb, 0)),
                  pl.BlockSpec((bm, LANE), lambda i: (i % nsb, 0))],
        out_specs=pl.BlockSpec((bm, LANE), lambda i: (i, 0)),
        out_shape=jax.ShapeDtypeStruct((m, LANE), BF),
        compiler_params=_cparams(("parallel",)),
        name="prep_kr",
    )(z, g, cos_t, sin_t)


def _prep_b_kernel(q_ref, kv_ref, kr_ref, qn_g, qr_g, kn_g, c_ref, s_ref, qc_ref, kc_ref, vt_ref):
    def norm(x, g, width):
        ms = jnp.sum(x * x, axis=-1, keepdims=True) * (1.0 / width)
        return x * lax.rsqrt(ms + EPS) * g

    kr = kr_ref[...]
    for h in range(HEADS):
        c0 = slice(2 * h * LANE, (2 * h + 1) * LANE)
        c1 = slice((2 * h + 1) * LANE, (2 * h + 2) * LANE)
        qc_ref[c0, :] = norm(q_ref[:, c0], qn_g[...], MLA_NOPE).T.astype(BF)
        qc_ref[c1, :] = _rope_tile(norm(q_ref[:, c1], qr_g[...], MLA_ROPE), c_ref[...], s_ref[...]).T.astype(BF)
        kc_ref[:, c0] = norm(kv_ref[:, c0], kn_g[...], MLA_NOPE).astype(BF)
        kc_ref[:, c1] = kr
        vt_ref[h * LANE:(h + 1) * LANE, :] = kv_ref[:, c1].T.astype(BF)


def prep_b(q_up, kv_up, kr, qn_g, qr_g, kn_g, cos_t, sin_t, s, bm=256):
    m = q_up.shape[0]
    bm = min(bm, s)
    nsb = s // bm
    w2 = 2 * HEADS * LANE
    row = lambda g: g.reshape(1, LANE).astype(F32)
    qr_pad = jnp.concatenate([qr_g, jnp.zeros((LANE - MLA_ROPE,), qr_g.dtype)])
    return pl.pallas_call(
        _prep_b_kernel,
        grid=(m // bm,),
        in_specs=[pl.BlockSpec((bm, w2), lambda i: (i, 0)),
                  pl.BlockSpec((bm, w2), lambda i: (i, 0)),
                  pl.BlockSpec((bm, LANE), lambda i: (i, 0)),
                  pl.BlockSpec((1, LANE), lambda i: (0, 0)),
                  pl.BlockSpec((1, LANE), lambda i: (0, 0)),
                  pl.BlockSpec((1, LANE), lambda i: (0, 0)),
                  pl.BlockSpec((bm, LANE), lambda i: (i % nsb, 0)),
                  pl.BlockSpec((bm, LANE), lambda i: (i % nsb, 0))],
        out_specs=[pl.BlockSpec((w2, bm), lambda i: (0, i)),
                   pl.BlockSpec((bm, w2), lambda i: (i, 0)),
                   pl.BlockSpec((HEADS * LANE, bm), lambda i: (0, i))],
        out_shape=[jax.ShapeDtypeStruct((w2, m), BF), jax.ShapeDtypeStruct((m, w2), BF),
                   jax.ShapeDtypeStruct((HEADS * LANE, m), BF)],
        compiler_params=_cparams(("parallel",)),
        name="prep_b",
    )(q_up, kv_up, kr, row(qn_g), row(qr_pad), row(kn_g), cos_t, sin_t)


def pack_w_uq(w):
    w = w.reshape(MLA_Q_RANK, HEADS, MLA_NOPE + MLA_ROPE)
    w = jnp.pad(w, ((0, 0), (0, 0), (0, 2 * LANE - MLA_NOPE - MLA_ROPE)))
    return w.reshape(MLA_Q_RANK, HEADS * 2 * LANE).astype(BF)


def mixer_b(z, s, w_uq, w_ukv, cq_g, ckv_g, qn_g, kn_g, qr_g, kr_g, cos_t, sin_t):
    c_q = rmsnorm_rows(z, cq_g, col_block=Z_B // MLA_Q_RANK)
    c_kv = rmsnorm_rows(z, ckv_g, col_block=(Z_B + MLA_Q_RANK) // MLA_KV_RANK)
    kr = prep_kr(z, kr_g, cos_t, sin_t, s)
    q_up = matmul(c_q, pack_w_uq(w_uq), F32, name="mla_q_up")
    kv_up = matmul(c_kv, w_ukv.astype(BF), F32, name="mla_kv_up")
    qc, kc, vt = prep_b(q_up, kv_up, kr, qn_g, qr_g, kn_g, cos_t, sin_t, s)
    return flash_attention(qc, kc, vt, s, diff=False, dk=2 * LANE, scale=(MLA_NOPE + MLA_ROPE) ** -0.5,
                           name="mla_attention")


DIL_PAD = DIL_STEPS * max(d for _, d in DIL_GROUPS)


def _dil_kernel(zq_ref, zk_ref, zv_ref, bias_ref, qg_ref, kg_ref, o_ref, q_sc, k_sc, v_sc, og_sc, lse_sc, *, s):
    g = pl.program_id(2)
    ngroups = len(DIL_GROUPS)

    def norm(x, gain):
        ms = jnp.mean(x * x, axis=-1, keepdims=True)
        return x * lax.rsqrt(ms + EPS) * gain

    zeros = jnp.zeros((DIL_PAD, LANE), F32)
    k_sc[0:DIL_PAD, :] = zeros
    v_sc[0:DIL_PAD, :] = zeros
    q_sc[...] = norm(zq_ref[...], qg_ref[...])
    k_sc[DIL_PAD:, :] = norm(zk_ref[...], kg_ref[...])
    v_sc[DIL_PAD:, :] = zv_ref[...]
    scale = LANE ** -0.5

    def group(gi, dil):
        nsub = s // DIL_STEPS

        def body(tt, carry):
            c = tt % dil
            n = tt // dil
            q0 = c + dil * DIL_STEPS * n
            rows = pl.ds(q0, DIL_STEPS, stride=dil) if dil > 1 else pl.ds(q0, DIL_STEPS)
            k0 = q0 + DIL_PAD - dil * DIL_STEPS
            band = pl.ds(k0, 2 * DIL_STEPS, stride=dil) if dil > 1 else pl.ds(k0, 2 * DIL_STEPS)
            qs = q_sc[rows, :].astype(BF)
            ks = k_sc[band, :].astype(BF)
            vs = v_sc[band, :].astype(BF)
            first = jnp.where(n == 0, 1, 0)
            logits = lax.dot_general(qs, ks, (((1,), (1,)), ((), ())), preferred_element_type=F32) * scale
            logits = logits + bias_ref[first]
            mx = jnp.max(logits, axis=-1, keepdims=True)
            e = jnp.exp(logits - mx)
            den = jnp.sum(e, axis=-1, keepdims=True)
            o = jnp.dot((e / den).astype(BF), vs, preferred_element_type=F32)
            og_sc[gi, rows, :] = o
            lse_sc[gi, rows, :] = jnp.broadcast_to(mx + jnp.log(den), (DIL_STEPS, LANE))
            return carry

        lax.fori_loop(0, nsub, body, 0, unroll=8)

    for gi, (_, dil) in enumerate(DIL_GROUPS):
        pl.when(g == gi)(functools.partial(group, gi, dil))

    @pl.when(g == ngroups - 1)
    def _():
        lses = [lse_sc[gi] for gi in range(ngroups)]
        mx = functools.reduce(jnp.maximum, lses)
        ws = [jnp.exp(l - mx) for l in lses]
        tot = functools.reduce(lambda a, b2: a + b2, ws)
        acc = ws[0] * og_sc[0]
        for gi in range(1, ngroups):
            acc = acc + ws[gi] * og_sc[gi]
        o_ref[...] = (acc / tot).astype(o_ref.dtype)


def mixer_c(z, s, q_g, k_g, bias_c):
    m = z.shape[0]
    b = m // s
    ngroups = len(DIL_GROUPS)
    cb = Z_C // LANE

    def col(which):
        return lambda bb, h, g: (bb, cb + (g * 3 + which) * HEADS + h)

    row = lambda g: g.reshape(1, LANE).astype(F32)
    return pl.pallas_call(
        functools.partial(_dil_kernel, s=s),
        grid=(b, HEADS, ngroups),
        in_specs=[pl.BlockSpec((s, LANE), col(0)),
                  pl.BlockSpec((s, LANE), col(1)),
                  pl.BlockSpec((s, LANE), col(2)),
                  pl.BlockSpec((None, None, 2, DIL_STEPS, 2 * DIL_STEPS), lambda bb, h, g: (g, h, 0, 0, 0)),
                  pl.BlockSpec((1, LANE), lambda bb, h, g: (0, 0)),
                  pl.BlockSpec((1, LANE), lambda bb, h, g: (0, 0))],
        out_specs=pl.BlockSpec((s, LANE), lambda bb, h, g: (bb, h)),
        out_shape=jax.ShapeDtypeStruct((m, HEADS * LANE), BF),
        scratch_shapes=[pltpu.VMEM((s, LANE), F32), pltpu.VMEM((DIL_PAD + s, LANE), F32),
                        pltpu.VMEM((DIL_PAD + s, LANE), F32), pltpu.VMEM((ngroups, s, LANE), F32),
                        pltpu.VMEM((ngroups, s, LANE), F32)],
        compiler_params=_cparams(("parallel", "parallel", "arbitrary")),
        name="dilated_attention",
    )(z, z, z, bias_c, row(q_g), row(k_g))


CONV_PAD = 8
GDN_GROUP = 4


def _gdn_kernel(par_ref, zq_ref, zk_ref, zv_ref, zg_ref, zab_ref, wq_ref, wk_ref, wv_ref, og_ref, o_ref,
                x_sc, q_sc, k_sc, v_sc, g_sc, b_sc, u_sc, w_sc, a_sc, st_sc, *, s):
    h = pl.program_id(1)
    c = GDN_CHUNK

    def conv_silu(z_ref, w_ref):
        x_sc[0:CONV_PAD, :] = jnp.zeros((CONV_PAD, LANE), F32)
        x_sc[CONV_PAD:, :] = z_ref[...]
        y = x_sc[CONV_PAD:, :] * w_ref[CONV_W - 1:CONV_W, :]
        for i in range(CONV_W - 1):
            off = CONV_PAD - (CONV_W - 1) + i
            y = y + x_sc[off:off + s, :] * w_ref[i:i + 1, :]
        return y * jax.nn.sigmoid(y)

    def l2(x):
        return x * lax.rsqrt(jnp.sum(x * x, axis=-1, keepdims=True) + EPS)

    q_sc[...] = l2(conv_silu(zq_ref, wq_ref)) * (LANE ** -0.5)
    k_sc[...] = l2(conv_silu(zk_ref, wk_ref))
    v_sc[...] = conv_silu(zv_ref, wv_ref)
    lane = lax.broadcasted_iota(jnp.int32, (1, LANE), 1)
    ab = zab_ref[...]
    a_col = jnp.sum(jnp.where(lane == h, ab, 0.0), axis=-1, keepdims=True)
    b_col = jnp.sum(jnp.where(lane == h + HEADS, ab, 0.0), axis=-1, keepdims=True)
    a_neg_exp = par_ref[0, h]
    dt_bias = par_ref[1, h]
    g_sc[...] = jnp.broadcast_to(a_neg_exp * jax.nn.softplus(a_col + dt_bias), (s, LANE))
    b_sc[...] = jnp.broadcast_to(jax.nn.sigmoid(b_col), (s, LANE))
    st_sc[...] = jnp.zeros(st_sc.shape, F32)

    gr = GDN_GROUP * c
    r_i = lax.broadcasted_iota(jnp.int32, (gr, gr), 0)
    c_i = lax.broadcasted_iota(jnp.int32, (gr, gr), 1)
    same = (r_i // c) == (c_i // c)
    tril = same & (r_i >= c_i)
    strict = same & (r_i > c_i)
    tril_b = tril.astype(BF)
    triu_b = (same & (r_i <= c_i)).astype(BF)
    eye = (r_i == c_i).astype(F32)

    def mm(a, b2):
        return jnp.dot(a.astype(BF), b2.astype(BF), preferred_element_type=F32)

    def mm_t(a, b2):
        return lax.dot_general(a.astype(BF), b2.astype(BF), (((1,), (1,)), ((), ())), preferred_element_type=F32)

    def split2(x):
        hi = x.astype(BF)
        return hi, (x - hi.astype(F32)).astype(BF)

    def split3(x):
        hi = x.astype(BF)
        r1 = x - hi.astype(F32)
        mid = r1.astype(BF)
        return hi, mid, (r1 - mid.astype(F32)).astype(BF)

    def mm_hi(a, b2):
        a_hi, a_lo = split2(a)
        b_hi, b_lo = split2(b2)
        d = functools.partial(jnp.dot, preferred_element_type=F32)
        return d(a_hi, b_hi) + (d(a_hi, b_lo) + d(a_lo, b_hi))

    def widen(x):
        return jnp.concatenate([x] * (gr // LANE), axis=1)

    def local_group(gi, carry):
        base = pl.multiple_of(gi * gr, gr)
        rows = pl.ds(base, gr)
        q = q_sc[rows, :]
        k = k_sc[rows, :]
        beta = b_sc[rows, :]
        parts = split3(g_sc[rows, :])
        gc = sum(jnp.dot(tril_b, part, preferred_element_type=F32) for part in parts)
        g_row = sum(lax.dot_general(widen(part), triu_b, (((0,), (0,)), ((), ())), preferred_element_type=F32)
                    for part in parts)
        decay = jnp.exp(jnp.where(tril, widen(gc) - g_row, NEG))
        kb = k * beta
        vb = v_sc[rows, :] * beta
        lower = jnp.where(strict, mm_t(kb, k) * decay, 0.0)
        pw = -lower
        tmat = eye + pw
        for _ in range(5):
            pw = mm_hi(pw, pw)
            tmat = tmat + mm_hi(tmat, pw)
        eg = jnp.exp(gc)
        u_sc[rows, :] = mm(tmat, vb)
        w_sc[rows, :] = mm(tmat, kb * eg)
        intra = mm_t(q, k) * decay
        q_sc[rows, :] = q * eg
        for uu in range(GDN_GROUP):
            sl = slice(uu * c, (uu + 1) * c)
            crow = pl.ds(base + uu * c, c)
            g_last = gc[(uu + 1) * c - 1:(uu + 1) * c, :]
            a_sc[crow, 0:c] = intra[sl, sl]
            k_sc[crow, :] = k[sl] * jnp.exp(g_last - gc[sl])
            g_sc[crow, :] = jnp.broadcast_to(jnp.exp(g_last), (c, LANE))
        return carry

    lax.fori_loop(0, s // gr, local_group, 0)

    def scan(n, carry):
        rows = pl.ds(pl.multiple_of(n * c, c), c)
        state = st_sc[...]
        v_new = u_sc[rows, :] - mm(w_sc[rows, :], state)
        o = mm(q_sc[rows, :], state) + mm(a_sc[rows, 0:c], v_new)
        decay_last = g_sc[pl.ds(pl.multiple_of(n * c, c), 1), :]
        st_sc[...] = state * decay_last + lax.dot_general(
            k_sc[rows, :].astype(BF), v_new.astype(BF), (((0,), (0,)), ((), ())), preferred_element_type=F32)
        ms = jnp.mean(o * o, axis=-1, keepdims=True)
        gate = zg_ref[rows, :]
        o_ref[rows, :] = (o * lax.rsqrt(ms + EPS) * og_ref[...] * (gate * jax.nn.sigmoid(gate))).astype(o_ref.dtype)
        return carry

    lax.fori_loop(0, s // c, scan, 0)


def mixer_d(z, s, conv_w, a_log, dt_bias, o_g):
    m = z.shape[0]
    b = m // s
    cb = Z_D // LANE
    par = jnp.stack([-jnp.exp(a_log.astype(F32)), dt_bias.astype(F32)], axis=0)
    cw = conv_w.astype(F32)

    def col(which):
        return lambda bb, h: (bb, cb + which * HEADS + h)

    return pl.pallas_call(
        functools.partial(_gdn_kernel, s=s),
        grid=(b, HEADS),
        in_specs=[pl.BlockSpec(memory_space=pltpu.SMEM),
                  pl.BlockSpec((s, LANE), col(0)),
                  pl.BlockSpec((s, LANE), col(1)),
                  pl.BlockSpec((s, LANE), col(2)),
                  pl.BlockSpec((s, LANE), col(3)),
                  pl.BlockSpec((s, LANE), lambda bb, h: (bb, cb + 4 * HEADS)),
                  pl.BlockSpec((CONV_W, LANE), lambda bb, h: (0, h)),
                  pl.BlockSpec((CONV_W, LANE), lambda bb, h: (0, HEADS + h)),
                  pl.BlockSpec((CONV_W, LANE), lambda bb, h: (0, 2 * HEADS + h)),
                  pl.BlockSpec((1, LANE), lambda bb, h: (0, 0))],
        out_specs=pl.BlockSpec((s, LANE), lambda bb, h: (bb, h)),
        out_shape=jax.ShapeDtypeStruct((m, HEADS * LANE), BF),
        scratch_shapes=[pltpu.VMEM((CONV_PAD + s, LANE), F32), pltpu.VMEM((s, LANE), F32),
                        pltpu.VMEM((s, LANE), F32), pltpu.VMEM((s, LANE), F32), pltpu.VMEM((s, LANE), F32),
                        pltpu.VMEM((s, LANE), F32), pltpu.VMEM((s, LANE), F32), pltpu.VMEM((s, LANE), F32),
                        pltpu.VMEM((s, LANE), F32), pltpu.VMEM((LANE, LANE), F32)],
        compiler_params=_cparams(("parallel", "arbitrary")),
        name="gated_deltanet",
    )(par, z, z, z, z, z, cw, cw, cw, o_g.reshape(1, LANE).astype(F32))


def pack_w_in(w):
    d = w.shape[0]
    c0 = DA_COLS + MLA_COLS + DIL_COLS
    qkv = w[:, c0:c0 + GDN_QKV]
    ab = w[:, c0 + GDN_QKV:c0 + GDN_QKV + 2 * HEADS]
    gate = w[:, c0 + GDN_QKV + 2 * HEADS:]
    used = Z_D + GDN_QKV + HEADS * LANE + 2 * HEADS
    parts = [w[:, :DA_COLS + MLA_COLS], jnp.zeros((d, 64), w.dtype), w[:, DA_COLS + MLA_COLS:c0],
             qkv, gate, ab, jnp.zeros((d, Z_COLS - used), w.dtype)]
    return jnp.concatenate([part.astype(BF) for part in parts], axis=1)


def kernel(x, p, rel_bias, norm_mix_g, w_in, da_lambda, da_q_g, da_k_g, da_o_g, mla_w_uq, mla_w_ukv, mla_cq_g, mla_ckv_g, mla_qn_g, mla_kn_g, mla_qr_g, mla_kr_g, dil_q_g, dil_k_g, gdn_conv_w, gdn_a_log, gdn_dt_bias, gdn_o_g, w_bgate, b_bgate, w_branch, w_out, norm_ffn_g, w_ffn_in, w_ffn_out, norm_ple_g, w_ple, w_ple_gate):
    b, s, d = x.shape
    m = b * s
    bias_a = diff_bias_tiles(rel_bias, s)
    bias_c = dil_bias_tiles(rel_bias)
    cos_t, sin_t = rope_tables(s)
    xr = x.reshape(m, d)
    for i in range(DEPTH):
        h = rmsnorm_rows(xr, norm_mix_g[i])
        z = matmul(h, pack_w_in(w_in[i]), F32, name="in_proj")
        lam_init = 0.8 - 0.6 * math.exp(-0.3 * i)
        lq1, lk1, lq2, lk2 = (da_lambda[i, j].astype(F32) for j in range(4))
        lam = jnp.exp(jnp.sum(lq1 * lk1)) - jnp.exp(jnp.sum(lq2 * lk2)) + lam_init
        o_a = mixer_a(z, s, lam, lam_init, da_q_g[i], da_k_g[i], da_o_g[i], bias_a)
        o_b = mixer_b(z, s, mla_w_uq[i], mla_w_ukv[i], mla_cq_g[i], mla_ckv_g[i], mla_qn_g[i], mla_kn_g[i],
                      mla_qr_g[i], mla_kr_g[i], cos_t, sin_t)
        o_c = mixer_c(z, s, dil_q_g[i], dil_k_g[i], bias_c)
        o_d = mixer_d(z, s, gdn_conv_w[i], gdn_a_log[i], gdn_dt_bias[i], gdn_o_g[i])
        o_all = jnp.stack([o_a, o_b, o_c, o_d], axis=0)
        merged = gated_merge(h, w_bgate[i].astype(BF), b_bgate[i].reshape(N_BRANCH, 1, d), o_all,
                             w_branch[i].astype(BF))
        xr = matmul_residual(xr, merged, w_out[i].astype(BF), name="out_proj")
        hf = rmsnorm_rows(xr, norm_ffn_g[i])
        wf = w_ffn_in[i]
        act = swiglu_in(hf, wf[:, :D_FF].astype(BF), wf[:, D_FF:].astype(BF))
        xr = matmul_residual(xr, act, w_ffn_out[i].astype(BF), bm=1024, bn=512, bk=D_FF // 2, name="ffn_out")
        hp = rmsnorm_rows(xr, norm_ple_g[i])
        xr = ple_update(xr, p[i].reshape(m, PLE_DIM).astype(BF), w_ple[i].astype(BF), hp, w_ple_gate[i].astype(BF))
    return xr.reshape(b, s, d)
```

```python
import functools
import math

import numpy as np
import jax
import jax.numpy as jnp
from jax import lax
from jax.experimental import pallas as pl
from jax.experimental.pallas import tpu as pltpu

D_MODEL = 4096
DEPTH = 2
EPS = 1e-6
PLE_DIM = 256
HEADS = 8
N_BRANCH = 4
BRANCH_W = 1024
DA_QK_DIM = 64
MLA_Q_RANK = 1024
MLA_KV_RANK = 512
MLA_NOPE = 128
MLA_ROPE = 64
ROPE_THETA = 10000.0
DIL_GROUPS = ((128, 1), (512, 4), (2048, 16))
DIL_STEPS = 128
GDN_CHUNK = 64
CONV_W = 4
N_BUCKETS = 32
MAX_DIST = 2048
D_FF = 11008

DA_COLS = 3072
MLA_COLS = 1600
DIL_COLS = 9216
GDN_QKV = 3072

LANE = 128
Z_A = 0
Z_B = Z_A + DA_COLS
Z_C = Z_B + MLA_COLS + 64
Z_D = Z_C + DIL_COLS
Z_COLS = 18 * 1024

NEG = -1e30
ATT_T = 256
VMEM_LIMIT = 56 * 1024 * 1024
BF = jnp.bfloat16
F32 = jnp.float32


def _cparams(sem):
    return pltpu.CompilerParams(dimension_semantics=sem, vmem_limit_bytes=VMEM_LIMIT)


def _rmsnorm_kernel(x_ref, g_ref, o_ref):
    x = x_ref[...]
    ms = jnp.mean(x * x, axis=-1, keepdims=True)
    o_ref[...] = (x * lax.rsqrt(ms + EPS) * g_ref[...]).astype(o_ref.dtype)


def rmsnorm_rows(x, g, col_block=0, bm=512):
    m = x.shape[0]
    d = g.shape[-1]
    return pl.pallas_call(
        _rmsnorm_kernel,
        grid=(m // bm,),
        in_specs=[pl.BlockSpec((bm, d), lambda i: (i, col_block)),
                  pl.BlockSpec((1, d), lambda i: (0, 0))],
        out_specs=pl.BlockSpec((bm, d), lambda i: (i, 0)),
        out_shape=jax.ShapeDtypeStruct((m, d), BF),
        compiler_params=_cparams(("parallel",)),
        name="rmsnorm_rows",
    )(x, g.reshape(1, d))


def _mm_kernel(x_ref, w_ref, o_ref):
    o_ref[...] = jnp.dot(x_ref[...], w_ref[...], preferred_element_type=F32).astype(o_ref.dtype)


def matmul(x, w, out_dtype, bm=1024, bn=1024, name="matmul"):
    m, k = x.shape
    _, n = w.shape
    bn = min(bn, n)
    bm = min(bm, m)
    return pl.pallas_call(
        _mm_kernel,
        grid=(m // bm, n // bn),
        in_specs=[pl.BlockSpec((bm, k), lambda i, j: (i, 0)),
                  pl.BlockSpec((k, bn), lambda i, j: (0, j))],
        out_specs=pl.BlockSpec((bm, bn), lambda i, j: (i, j)),
        out_shape=jax.ShapeDtypeStruct((m, n), out_dtype),
        compiler_params=_cparams(("parallel", "parallel")),
        name=name,
    )(x, w)


def _mm_residual_kernel(r_ref, x_ref, w_ref, o_ref, *acc, nk):
    part = jnp.dot(x_ref[...], w_ref[...], preferred_element_type=F32)
    if nk == 1:
        o_ref[...] = r_ref[...] + part
    else:
        acc_ref, = acc

        @pl.when(pl.program_id(2) == 0)
        def _():
            acc_ref[...] = r_ref[...]

        total = acc_ref[...] + part
        acc_ref[...] = total
        o_ref[...] = total


def matmul_residual(r, x, w, bm=1024, bn=512, bk=None, name="matmul_residual"):
    m, kdim = x.shape
    _, n = w.shape
    bk = kdim if bk is None else bk
    bm = min(bm, m)
    nk = kdim // bk
    return pl.pallas_call(
        functools.partial(_mm_residual_kernel, nk=nk),
        grid=(m // bm, n // bn, nk),
        in_specs=[pl.BlockSpec((bm, bn), lambda i, j, k: (i, j)),
                  pl.BlockSpec((bm, bk), lambda i, j, k: (i, k)),
                  pl.BlockSpec((bk, bn), lambda i, j, k: (k, j))],
        out_specs=pl.BlockSpec((bm, bn), lambda i, j, k: (i, j)),
        out_shape=jax.ShapeDtypeStruct((m, n), F32),
        scratch_shapes=[pltpu.VMEM((bm, bn), F32)] if nk > 1 else [],
        compiler_params=_cparams(("parallel", "parallel", "arbitrary")),
        name=name,
    )(r, x, w)


def _merge_kernel(h_ref, wg_ref, b_ref, o_ref, wb_ref, out_ref, acc_ref):
    n = pl.program_id(2)
    gate = jax.nn.sigmoid(jnp.dot(h_ref[...], wg_ref[...], preferred_element_type=F32) + b_ref[...])
    term = gate * jnp.dot(o_ref[...], wb_ref[...], preferred_element_type=F32)

    @pl.when(n == 0)
    def _():
        acc_ref[...] = jnp.zeros(acc_ref.shape, F32)

    total = acc_ref[...] + term
    acc_ref[...] = total
    out_ref[...] = total.astype(out_ref.dtype)


def gated_merge(h, w_bgate, b_bgate, o_all, w_branch, bm=1024, bn=512):
    m, d = h.shape
    bm = min(bm, m)
    return pl.pallas_call(
        _merge_kernel,
        grid=(m // bm, d // bn, N_BRANCH),
        in_specs=[pl.BlockSpec((bm, d), lambda i, j, n: (i, 0)),
                  pl.BlockSpec((None, d, bn), lambda i, j, n: (n, 0, j)),
                  pl.BlockSpec((None, 1, bn), lambda i, j, n: (n, 0, j)),
                  pl.BlockSpec((None, bm, BRANCH_W), lambda i, j, n: (n, i, 0)),
                  pl.BlockSpec((None, BRANCH_W, bn), lambda i, j, n: (n, 0, j))],
        out_specs=pl.BlockSpec((bm, bn), lambda i, j, n: (i, j)),
        out_shape=jax.ShapeDtypeStruct((m, d), BF),
        scratch_shapes=[pltpu.VMEM((bm, bn), F32)],
        compiler_params=_cparams(("parallel", "parallel", "arbitrary")),
        name="gated_merge",
    )(h, w_bgate, b_bgate, o_all, w_branch)


def _swiglu_kernel(x_ref, wg_ref, wu_ref, o_ref):
    x = x_ref[...]
    g = jnp.dot(x, wg_ref[...], preferred_element_type=F32)
    u = jnp.dot(x, wu_ref[...], preferred_element_type=F32)
    o_ref[...] = (g * jax.nn.sigmoid(g) * u).astype(o_ref.dtype)


def swiglu_in(x, w, bm=2048, bn=256):
    m, k = x.shape
    n = w.shape[1] // 2
    bm = min(bm, m)
    nb = n // bn
    return pl.pallas_call(
        _swiglu_kernel,
        grid=(m // bm, nb),
        in_specs=[pl.BlockSpec((bm, k), lambda i, j: (i, 0)),
                  pl.BlockSpec((k, bn), lambda i, j: (0, j)),
                  pl.BlockSpec((k, bn), lambda i, j: (0, nb + j))],
        out_specs=pl.BlockSpec((bm, bn), lambda i, j: (i, j)),
        out_shape=jax.ShapeDtypeStruct((m, n), BF),
        compiler_params=_cparams(("parallel", "parallel")),
        name="swiglu_in",
    )(x, w, w)


def _ple_kernel(r_ref, p_ref, wp_ref, h_ref, wg_ref, o_ref):
    e = jnp.dot(p_ref[...], wp_ref[...], preferred_element_type=F32)
    g = jnp.dot(h_ref[...], wg_ref[...], preferred_element_type=F32)
    o_ref[...] = r_ref[...] + e * jax.nn.sigmoid(g)


def ple_update(r, p, w_ple, hp, w_gate, bm=1024, bn=512):
    m, d = r.shape
    kp = p.shape[1]
    bm = min(bm, m)
    return pl.pallas_call(
        _ple_kernel,
        grid=(m // bm, d // bn),
        in_specs=[pl.BlockSpec((bm, bn), lambda i, j: (i, j)),
                  pl.BlockSpec((bm, kp), lambda i, j: (i, 0)),
                  pl.BlockSpec((kp, bn), lambda i, j: (0, j)),
                  pl.BlockSpec((bm, d), lambda i, j: (i, 0)),
                  pl.BlockSpec((d, bn), lambda i, j: (0, j))],
        out_specs=pl.BlockSpec((bm, bn), lambda i, j: (i, j)),
        out_shape=jax.ShapeDtypeStruct((m, d), F32),
        compiler_params=_cparams(("parallel", "parallel")),
        name="ple_update",
    )(r, p, w_ple, hp, w_gate)


def _static_buckets(dist):
    max_exact = N_BUCKETS // 2
    d = np.maximum(np.asarray(dist), 0)
    large = max_exact + (np.log(np.maximum(d, 1).astype(np.float32) / np.float32(max_exact))
                         / np.float32(math.log(MAX_DIST / max_exact))
                         * np.float32(N_BUCKETS - max_exact)).astype(np.int32)
    large = np.minimum(large, N_BUCKETS - 1)
    return np.where(d < max_exact, d, large).astype(np.int32)


def _toeplitz_tiles(f, t):
    hh, n = f.shape
    big = n + t - 1
    fpad = jnp.concatenate([jnp.full((hh, t - 1), NEG, f.dtype), f], axis=1)
    flat = jnp.tile(fpad, (1, t + 1))[:, :t * (big + 1)]
    w = flat.reshape(hh, t, big + 1)[:, :, :n]
    w = jnp.flip(w.reshape(hh, t, n // t, t), axis=3)
    return w.transpose(0, 2, 1, 3)


def _bias_tile_kernel(prev_ref, cur_ref, o_ref, *, t):
    row = jnp.concatenate([prev_ref[...], cur_ref[...]], axis=1)
    rolled = pltpu.roll(jnp.broadcast_to(row, (t, 2 * t)), 0, 1, stride=1, stride_axis=0)
    o_ref[...] = rolled[:, t:]


def diff_bias_tiles(rel_bias, s):
    t = min(ATT_T, s)
    nq = s // t
    f = jnp.take(rel_bias[:, :HEADS], _static_buckets(np.arange(s)), axis=0).T.astype(F32)
    fb = jnp.concatenate([jnp.full((HEADS, t), NEG, F32), f], axis=1).reshape(HEADS, nq + 1, 1, t)
    return pl.pallas_call(
        functools.partial(_bias_tile_kernel, t=t),
        grid=(HEADS, nq),
        in_specs=[pl.BlockSpec((None, None, 1, t), lambda h, d: (h, d, 0, 0)),
                  pl.BlockSpec((None, None, 1, t), lambda h, d: (h, d + 1, 0, 0))],
        out_specs=pl.BlockSpec((None, None, t, t), lambda h, d: (h, d, 0, 0)),
        out_shape=jax.ShapeDtypeStruct((HEADS, nq, t, t), F32),
        compiler_params=_cparams(("parallel", "parallel")),
        name="diff_bias_tiles",
    )(fb, fb)


def dil_bias_tiles(rel_bias):
    out = []
    for gi, (_, dil) in enumerate(DIL_GROUPS):
        lo = HEADS + gi * HEADS
        steps = np.arange(2 * DIL_STEPS)
        f = jnp.take(rel_bias[:, lo:lo + HEADS], _static_buckets(steps * dil), axis=0).T.astype(F32)
        f = jnp.where(steps[None, :] <= DIL_STEPS, f, NEG)
        tiles = _toeplitz_tiles(f, DIL_STEPS)
        general = jnp.concatenate([tiles[:, 1], tiles[:, 0]], axis=-1)
        first = jnp.concatenate([jnp.full_like(tiles[:, 1], NEG), tiles[:, 0]], axis=-1)
        out.append(jnp.stack([general, first], axis=1))
    return jnp.stack(out, axis=0)


def rope_tables(s):
    half = MLA_ROPE // 2
    inv = ROPE_THETA ** (-jnp.arange(half, dtype=F32) / half)
    ang = jnp.arange(s).astype(F32)[:, None] * inv[None, :]
    cos, sin = jnp.cos(ang), jnp.sin(ang)
    zero = jnp.zeros((s, LANE - MLA_ROPE), F32)
    return (jnp.concatenate([cos, cos, zero], axis=1), jnp.concatenate([-sin, sin, zero], axis=1))


def _rope_tile(t, c, s):
    half = MLA_ROPE // 2
    swapped = pltpu.roll(t, half, axis=1) + pltpu.roll(t, LANE - half, axis=1)
    return t * c + swapped * s


def _flash_kernel(lam_ref, qt_ref, k_ref, vt_ref, *rest, diff, t, scale, post_scale):
    if diff:
        bias_ref, g_ref, o_ref, m_sc, l_sc, acc_sc = rest
    else:
        o_ref, m_sc, l_sc, acc_sc = rest
    i = pl.program_id(2)
    m_sc[...] = jnp.full(m_sc.shape, NEG, F32)
    l_sc[...] = jnp.zeros(l_sc.shape, F32)
    acc_sc[...] = jnp.zeros(acc_sc.shape, F32)
    qts = [qt_ref[0], qt_ref[1]] if diff else [qt_ref[:, 0:t], qt_ref[:, t:2 * t]]

    def load_kv(j):
        start = pl.multiple_of(j * t, t)
        return k_ref[pl.ds(start, t), :], vt_ref[:, pl.ds(start, t)]

    def step(work):
        scores = [jnp.dot(kv[0], qts[c], preferred_element_type=F32) for c, kv, _, _ in work]
        for (c, kv, bias, diagonal), s in zip(work, scores):
            if scale != 1.0:
                s = s * scale
            if bias is not None:
                s = bias + s
            elif diagonal:
                key_i = lax.broadcasted_iota(jnp.int32, (t, t), 0)
                qry_i = lax.broadcasted_iota(jnp.int32, (t, t), 1)
                s = jnp.where(key_i <= qry_i, s, NEG)
            m_prev = m_sc[c]
            m_new = jnp.maximum(m_prev, jnp.max(s, axis=0, keepdims=True))
            alpha = jnp.exp(m_prev - m_new)
            p = jnp.exp(s - m_new)
            l_sc[c] = alpha * l_sc[c] + jnp.sum(p, axis=0, keepdims=True)
            m_sc[c] = m_new
            acc_sc[c] = alpha * acc_sc[c] + jnp.dot(kv[1], p.astype(BF), preferred_element_type=F32)

    def sweep(n, work):
        def body(jj, carry):
            step(work(4 * jj) + work(4 * jj + 1) + work(4 * jj + 2) + work(4 * jj + 3))
            return carry

        lax.fori_loop(0, n // 4, body, 0)
        done = (n // 4) * 4

        @pl.when(n % 4 >= 2)
        def _():
            step(work(done) + work(done + 1))

        @pl.when(n % 2 == 1)
        def _():
            step(work(n - 1))

    if diff:
        def work(j):
            kv = load_kv(j)
            bias = bias_ref[i - j]
            return [(0, kv, bias, False), (1, kv, bias, False)]

        sweep(i + 1, work)
        out_t = acc_sc[0] / l_sc[0] - lam_ref[0] * (acc_sc[1] / l_sc[1])
        o = out_t.T
        ms = jnp.mean(o * o, axis=-1, keepdims=True)
        o_ref[...] = (o * lax.rsqrt(ms + EPS) * g_ref[...] * post_scale).astype(o_ref.dtype)
    else:
        def work(j):
            kv = load_kv(j)
            return [(0, kv, None, False), (1, kv, None, False)]

        sweep(2 * i, work)
        kv = load_kv(2 * i)
        step([(0, kv, None, True), (1, kv, None, False)])
        step([(1, load_kv(2 * i + 1), None, True)])
        o_ref[0:t, :] = (acc_sc[0] / l_sc[0]).T.astype(o_ref.dtype)
        o_ref[t:2 * t, :] = (acc_sc[1] / l_sc[1]).T.astype(o_ref.dtype)


def flash_attention(q, k, vt, s, *, diff, dk, scale, bias=None, lam=None, gain=None, post_scale=1.0, name):
    m = k.shape[0]
    b = m // s
    t = min(ATT_T, s)
    lam = jnp.zeros((1,), F32) if lam is None else lam.reshape(1).astype(F32)
    if diff:
        nq = s // t
        q_spec = pl.BlockSpec((2, dk, t), lambda bb, h, i: (0, h, bb * nq + i))
        o_rows = t
    else:
        nq = s // (2 * t)
        q_spec = pl.BlockSpec((dk, 2 * t), lambda bb, h, i: (h, bb * nq + i))
        o_rows = 2 * t
    in_specs = [pl.BlockSpec(memory_space=pltpu.SMEM), q_spec,
                pl.BlockSpec((s, dk), lambda bb, h, i: (bb, h)),
                pl.BlockSpec((LANE, s), lambda bb, h, i: (h, bb))]
    args = [lam, q, k, vt]
    if diff:
        in_specs += [pl.BlockSpec((None, nq, t, t), lambda bb, h, i: (h, 0, 0, 0)),
                     pl.BlockSpec((1, LANE), lambda bb, h, i: (0, 0))]
        args += [bias, gain.reshape(1, LANE).astype(F32)]
    return pl.pallas_call(
        functools.partial(_flash_kernel, diff=diff, t=t, scale=scale, post_scale=post_scale),
        grid=(b, HEADS, nq),
        in_specs=in_specs,
        out_specs=pl.BlockSpec((o_rows, LANE), lambda bb, h, i: (bb * nq + i, h)),
        out_shape=jax.ShapeDtypeStruct((m, HEADS * LANE), BF),
        scratch_shapes=[pltpu.VMEM((2, 1, t), F32), pltpu.VMEM((2, 1, t), F32), pltpu.VMEM((2, LANE, t), F32)],
        compiler_params=_cparams(("parallel", "parallel", "arbitrary")),
        name=name,
    )(*args)


def _prep_a_kernel(z_ref, qg_ref, kg_ref, qz_ref, kn_ref, vt_ref):
    lane = lax.broadcasted_iota(jnp.int32, (1, LANE), 1)
    lo = lane < DA_QK_DIM

    def norm_halves(x, g):
        x2 = x * x
        s_lo = jnp.sum(jnp.where(lo, x2, 0.0), axis=-1, keepdims=True)
        s_hi = jnp.sum(jnp.where(lo, 0.0, x2), axis=-1, keepdims=True)
        inv = lax.rsqrt(jnp.where(lo, s_lo, s_hi) * (1.0 / DA_QK_DIM) + EPS)
        return x * inv * g

    scale = DA_QK_DIM ** -0.5
    for h in range(HEADS):
        cq = slice(h * LANE, (h + 1) * LANE)
        ck = slice(HEADS * LANE + h * LANE, HEADS * LANE + (h + 1) * LANE)
        q = norm_halves(z_ref[:, cq], qg_ref[...]) * scale
        qt = q.T
        sub_lo = lax.broadcasted_iota(jnp.int32, (LANE, 1), 0) < DA_QK_DIM
        qz_ref[0, cq, :] = jnp.where(sub_lo, qt, 0.0).astype(BF)
        qz_ref[1, cq, :] = jnp.where(sub_lo, 0.0, qt).astype(BF)
        kn_ref[:, cq] = norm_halves(z_ref[:, ck], kg_ref[...]).astype(BF)
        cv = slice(2 * HEADS * LANE + h * LANE, 2 * HEADS * LANE + (h + 1) * LANE)
        vt_ref[cq, :] = z_ref[:, cv].T.astype(BF)


def prep_a(z, q_g, k_g, bm=256):
    m = z.shape[0]
    bm = min(bm, m)
    w = HEADS * LANE
    g2 = lambda g: jnp.concatenate([g, g]).reshape(1, LANE).astype(F32)
    return pl.pallas_call(
        _prep_a_kernel,
        grid=(m // bm,),
        in_specs=[pl.BlockSpec((bm, DA_COLS), lambda i: (i, Z_A // DA_COLS)),
                  pl.BlockSpec((1, LANE), lambda i: (0, 0)),
                  pl.BlockSpec((1, LANE), lambda i: (0, 0))],
        out_specs=[pl.BlockSpec((2, w, bm), lambda i: (0, 0, i)),
                   pl.BlockSpec((bm, w), lambda i: (i, 0)),
                   pl.BlockSpec((w, bm), lambda i: (0, i))],
        out_shape=[jax.ShapeDtypeStruct((2, w, m), BF), jax.ShapeDtypeStruct((m, w), BF),
                   jax.ShapeDtypeStruct((w, m), BF)],
        compiler_params=_cparams(("parallel",)),
        name="prep_a",
    )(z, g2(q_g), g2(k_g))


def mixer_a(z, s, lam, lam_init, q_g, k_g, o_g, bias_tiles):
    qz, kn, vt = prep_a(z, q_g, k_g)
    return flash_attention(qz, kn, vt, s, diff=True, dk=LANE, scale=1.0, bias=bias_tiles, lam=lam, gain=o_g,
                           post_scale=1.0 - lam_init, name="diff_attention")


def _prep_kr_kernel(z_ref, g_ref, c_ref, s_ref, o_ref):
    x = z_ref[...]
    ms = jnp.sum(x * x, axis=-1, keepdims=True) * (1.0 / MLA_ROPE)
    o_ref[...] = _rope_tile(x * lax.rsqrt(ms + EPS) * g_ref[...], c_ref[...], s_ref[...]).astype(o_ref.dtype)


def prep_kr(z, kr_g, cos_t, sin_t, s, bm=512):
    m = z.shape[0]
    bm = min(bm, s)
    nsb = s // bm
    g = jnp.concatenate([kr_g, jnp.zeros((LANE - MLA_ROPE,), kr_g.dtype)]).reshape(1, LANE).astype(F32)
    return pl.pallas_call(
        _prep_kr_kernel,
        grid=(m // bm,),
        in_specs=[pl.BlockSpec((bm, LANE), lambda i: (i, (Z_B + MLA_Q_RANK + MLA_KV_RANK) // LANE)),
                  pl.BlockSpec((1, LANE), lambda i: (0, 0)),
                  pl.BlockSpec((bm, LANE), lambda i: (i % nsb, 0)),
                  pl.BlockSpec((bm, LANE), lambda i: (i % nsb, 0))],
        out_specs=pl.BlockSpec((bm, LANE), lambda i: (i, 0)),
        out_shape=jax.ShapeDtypeStruct((m, LANE), BF),
        compiler_params=_cparams(("parallel",)),
        name="prep_kr",
    )(z, g, cos_t, sin_t)


def _prep_b_kernel(q_ref, kv_ref, kr_ref, qn_g, qr_g, kn_g, c_ref, s_ref, qc_ref, kc_ref, vt_ref):
    def norm(x, g, width):
        ms = jnp.sum(x * x, axis=-1, keepdims=True) * (1.0 / width)
        return x * lax.rsqrt(ms + EPS) * g

    kr = kr_ref[...]
    for h in range(HEADS):
        c0 = slice(2 * h * LANE, (2 * h + 1) * LANE)
        c1 = slice((2 * h + 1) * LANE, (2 * h + 2) * LANE)
        qc_ref[c0, :] = norm(q_ref[:, c0], qn_g[...], MLA_NOPE).T.astype(BF)
        qc_ref[c1, :] = _rope_tile(norm(q_ref[:, c1], qr_g[...], MLA_ROPE), c_ref[...], s_ref[...]).T.astype(BF)
        kc_ref[:, c0] = norm(kv_ref[:, c0], kn_g[...], MLA_NOPE).astype(BF)
        kc_ref[:, c1] = kr
        vt_ref[h * LANE:(h + 1) * LANE, :] = kv_ref[:, c1].T.astype(BF)


def prep_b(q_up, kv_up, kr, qn_g, qr_g, kn_g, cos_t, sin_t, s, bm=256):
    m = q_up.shape[0]
    bm = min(bm, s)
    nsb = s // bm
    w2 = 2 * HEADS * LANE
    row = lambda g: g.reshape(1, LANE).astype(F32)
    qr_pad = jnp.concatenate([qr_g, jnp.zeros((LANE - MLA_ROPE,), qr_g.dtype)])
    return pl.pallas_call(
        _prep_b_kernel,
        grid=(m // bm,),
        in_specs=[pl.BlockSpec((bm, w2), lambda i: (i, 0)),
                  pl.BlockSpec((bm, w2), lambda i: (i, 0)),
                  pl.BlockSpec((bm, LANE), lambda i: (i, 0)),
                  pl.BlockSpec((1, LANE), lambda i: (0, 0)),
                  pl.BlockSpec((1, LANE), lambda i: (0, 0)),
                  pl.BlockSpec((1, LANE), lambda i: (0, 0)),
                  pl.BlockSpec((bm, LANE), lambda i: (i % nsb, 0)),
                  pl.BlockSpec((bm, LANE), lambda i: (i % nsb, 0))],
        out_specs=[pl.BlockSpec((w2, bm), lambda i: (0, i)),
                   pl.BlockSpec((bm, w2), lambda i: (i, 0)),
                   pl.BlockSpec((HEADS * LANE, bm), lambda i: (0, i))],
        out_shape=[jax.ShapeDtypeStruct((w2, m), BF), jax.ShapeDtypeStruct((m, w2), BF),
                   jax.ShapeDtypeStruct((HEADS * LANE, m), BF)],
        compiler_params=_cparams(("parallel",)),
        name="prep_b",
    )(q_up, kv_up, kr, row(qn_g), row(qr_pad), row(kn_g), cos_t, sin_t)


def pack_w_uq(w):
    w = w.reshape(MLA_Q_RANK, HEADS, MLA_NOPE + MLA_ROPE)
    w = jnp.pad(w, ((0, 0), (0, 0), (0, 2 * LANE - MLA_NOPE - MLA_ROPE)))
    return w.reshape(MLA_Q_RANK, HEADS * 2 * LANE).astype(BF)


def mixer_b(z, s, w_uq, w_ukv, cq_g, ckv_g, qn_g, kn_g, qr_g, kr_g, cos_t, sin_t):
    c_q = rmsnorm_rows(z, cq_g, col_block=Z_B // MLA_Q_RANK)
    c_kv = rmsnorm_rows(z, ckv_g, col_block=(Z_B + MLA_Q_RANK) // MLA_KV_RANK)
    kr = prep_kr(z, kr_g, cos_t, sin_t, s)
    q_up = matmul(c_q, pack_w_uq(w_uq), F32, name="mla_q_up")
    kv_up = matmul(c_kv, w_ukv, F32, name="mla_kv_up")
    qc, kc, vt = prep_b(q_up, kv_up, kr, qn_g, qr_g, kn_g, cos_t, sin_t, s)
    return flash_attention(qc, kc, vt, s, diff=False, dk=2 * LANE, scale=(MLA_NOPE + MLA_ROPE) ** -0.5,
                           name="mla_attention")


DIL_PAD = DIL_STEPS * max(d for _, d in DIL_GROUPS)


def _dil_kernel(zq_ref, zk_ref, zv_ref, bias_ref, qg_ref, kg_ref, o_ref, q_sc, k_sc, v_sc, og_sc, lse_sc, *, s):
    g = pl.program_id(2)
    ngroups = len(DIL_GROUPS)

    def norm(x, gain):
        ms = jnp.mean(x * x, axis=-1, keepdims=True)
        return x * lax.rsqrt(ms + EPS) * gain

    zeros = jnp.zeros((DIL_PAD, LANE), F32)
    k_sc[0:DIL_PAD, :] = zeros
    v_sc[0:DIL_PAD, :] = zeros
    q_sc[...] = norm(zq_ref[...], qg_ref[...])
    k_sc[DIL_PAD:, :] = norm(zk_ref[...], kg_ref[...])
    v_sc[DIL_PAD:, :] = zv_ref[...]
    scale = LANE ** -0.5

    def group(gi, dil):
        nsub = s // DIL_STEPS

        def body(tt, carry):
            c = tt % dil
            n = tt // dil
            q0 = c + dil * DIL_STEPS * n
            rows = pl.ds(q0, DIL_STEPS, stride=dil) if dil > 1 else pl.ds(q0, DIL_STEPS)
            k0 = q0 + DIL_PAD - dil * DIL_STEPS
            band = pl.ds(k0, 2 * DIL_STEPS, stride=dil) if dil > 1 else pl.ds(k0, 2 * DIL_STEPS)
            qs = q_sc[rows, :].astype(BF)
            ks = k_sc[band, :].astype(BF)
            vs = v_sc[band, :].astype(BF)
            first = jnp.where(n == 0, 1, 0)
            logits = lax.dot_general(qs, ks, (((1,), (1,)), ((), ())), preferred_element_type=F32) * scale
            logits = logits + bias_ref[first]
            mx = jnp.max(logits, axis=-1, keepdims=True)
            e = jnp.exp(logits - mx)
            den = jnp.sum(e, axis=-1, keepdims=True)
            o = jnp.dot((e / den).astype(BF), vs, preferred_element_type=F32)
            og_sc[gi, rows, :] = o
            lse_sc[gi, rows, :] = jnp.broadcast_to(mx + jnp.log(den), (DIL_STEPS, LANE))
            return carry

        lax.fori_loop(0, nsub, body, 0, unroll=8)

    for gi, (_, dil) in enumerate(DIL_GROUPS):
        pl.when(g == gi)(functools.partial(group, gi, dil))

    @pl.when(g == ngroups - 1)
    def _():
        lses = [lse_sc[gi] for gi in range(ngroups)]
        mx = functools.reduce(jnp.maximum, lses)
        ws = [jnp.exp(l - mx) for l in lses]
        tot = functools.reduce(lambda a, b2: a + b2, ws)
        acc = ws[0] * og_sc[0]
        for gi in range(1, ngroups):
            acc = acc + ws[gi] * og_sc[gi]
        o_ref[...] = (acc / tot).astype(o_ref.dtype)


def mixer_c(z, s, q_g, k_g, bias_c):
    m = z.shape[0]
    b = m // s
    ngroups = len(DIL_GROUPS)
    cb = Z_C // LANE

    def col(which):
        return lambda bb, h, g: (bb, cb + (g * 3 + which) * HEADS + h)

    row = lambda g: g.reshape(1, LANE).astype(F32)
    return pl.pallas_call(
        functools.partial(_dil_kernel, s=s),
        grid=(b, HEADS, ngroups),
        in_specs=[pl.BlockSpec((s, LANE), col(0)),
                  pl.BlockSpec((s, LANE), col(1)),
                  pl.BlockSpec((s, LANE), col(2)),
                  pl.BlockSpec((None, None, 2, DIL_STEPS, 2 * DIL_STEPS), lambda bb, h, g: (g, h, 0, 0, 0)),
                  pl.BlockSpec((1, LANE), lambda bb, h, g: (0, 0)),
                  pl.BlockSpec((1, LANE), lambda bb, h, g: (0, 0))],
        out_specs=pl.BlockSpec((s, LANE), lambda bb, h, g: (bb, h)),
        out_shape=jax.ShapeDtypeStruct((m, HEADS * LANE), BF),
        scratch_shapes=[pltpu.VMEM((s, LANE), F32), pltpu.VMEM((DIL_PAD + s, LANE), F32),
                        pltpu.VMEM((DIL_PAD + s, LANE), F32), pltpu.VMEM((ngroups, s, LANE), F32),
                        pltpu.VMEM((ngroups, s, LANE), F32)],
        compiler_params=_cparams(("parallel", "parallel", "arbitrary")),
        name="dilated_attention",
    )(z, z, z, bias_c, row(q_g), row(k_g))


CONV_PAD = 8
GDN_GROUP = 4


def _gdn_kernel(par_ref, zq_ref, zk_ref, zv_ref, zg_ref, zab_ref, wq_ref, wk_ref, wv_ref, og_ref, o_ref,
                x_sc, q_sc, k_sc, v_sc, g_sc, b_sc, u_sc, w_sc, a_sc, st_sc, *, s):
    h = pl.program_id(1)
    c = GDN_CHUNK

    def conv_silu(z_ref, w_ref):
        x_sc[0:CONV_PAD, :] = jnp.zeros((CONV_PAD, LANE), F32)
        x_sc[CONV_PAD:, :] = z_ref[...]
        y = x_sc[CONV_PAD:, :] * w_ref[CONV_W - 1:CONV_W, :]
        for i in range(CONV_W - 1):
            off = CONV_PAD - (CONV_W - 1) + i
            y = y + x_sc[off:off + s, :] * w_ref[i:i + 1, :]
        return y * jax.nn.sigmoid(y)

    def l2(x):
        return x * lax.rsqrt(jnp.sum(x * x, axis=-1, keepdims=True) + EPS)

    q_sc[...] = l2(conv_silu(zq_ref, wq_ref)) * (LANE ** -0.5)
    k_sc[...] = l2(conv_silu(zk_ref, wk_ref))
    v_sc[...] = conv_silu(zv_ref, wv_ref)
    lane = lax.broadcasted_iota(jnp.int32, (1, LANE), 1)
    ab = zab_ref[...]
    a_col = jnp.sum(jnp.where(lane == h, ab, 0.0), axis=-1, keepdims=True)
    b_col = jnp.sum(jnp.where(lane == h + HEADS, ab, 0.0), axis=-1, keepdims=True)
    a_neg_exp = par_ref[0, h]
    dt_bias = par_ref[1, h]
    g_sc[...] = jnp.broadcast_to(a_neg_exp * jax.nn.softplus(a_col + dt_bias), (s, LANE))
    b_sc[...] = jnp.broadcast_to(jax.nn.sigmoid(b_col), (s, LANE))
    st_sc[...] = jnp.zeros(st_sc.shape, F32)

    gr = GDN_GROUP * c
    r_i = lax.broadcasted_iota(jnp.int32, (gr, gr), 0)
    c_i = lax.broadcasted_iota(jnp.int32, (gr, gr), 1)
    same = (r_i // c) == (c_i // c)
    tril = same & (r_i >= c_i)
    strict = same & (r_i > c_i)
    tril_b = tril.astype(BF)
    triu_b = (same & (r_i <= c_i)).astype(BF)
    eye = (r_i == c_i).astype(F32)

    def mm(a, b2):
        return jnp.dot(a.astype(BF), b2.astype(BF), preferred_element_type=F32)

    def mm_t(a, b2):
        return lax.dot_general(a.astype(BF), b2.astype(BF), (((1,), (1,)), ((), ())), preferred_element_type=F32)

    def split2(x):
        hi = x.astype(BF)
        return hi, (x - hi.astype(F32)).astype(BF)

    def split3(x):
        hi = x.astype(BF)
        r1 = x - hi.astype(F32)
        mid = r1.astype(BF)
        return hi, mid, (r1 - mid.astype(F32)).astype(BF)

    def mm_hi(a, b2):
        a_hi, a_lo = split2(a)
        b_hi, b_lo = split2(b2)
        d = functools.partial(jnp.dot, preferred_element_type=F32)
        return d(a_hi, b_hi) + (d(a_hi, b_lo) + d(a_lo, b_hi))

    def widen(x):
        return jnp.concatenate([x] * (gr // LANE), axis=1)

    def local_group(gi, carry):
        base = pl.multiple_of(gi * gr, gr)
        rows = pl.ds(base, gr)
        q = q_sc[rows, :]
        k = k_sc[rows, :]
        beta = b_sc[rows, :]
        parts = split3(g_sc[rows, :])
        gc = sum(jnp.dot(tril_b, part, preferred_element_type=F32) for part in parts)
        g_row = sum(lax.dot_general(widen(part), triu_b, (((0,), (0,)), ((), ())), preferred_element_type=F32)
                    for part in parts)
        decay = jnp.exp(jnp.where(tril, widen(gc) - g_row, NEG))
        kb = k * beta
        vb = v_sc[rows, :] * beta
        lower = jnp.where(strict, mm_t(kb, k) * decay, 0.0)
        pw = -lower
        tmat = eye + pw
        for _ in range(5):
            pw = mm_hi(pw, pw)
            tmat = tmat + mm_hi(tmat, pw)
        eg = jnp.exp(gc)
        u_sc[rows, :] = mm(tmat, vb)
        w_sc[rows, :] = mm(tmat, kb * eg)
        intra = mm_t(q, k) * decay
        q_sc[rows, :] = q * eg
        for uu in range(GDN_GROUP):
            sl = slice(uu * c, (uu + 1) * c)
            crow = pl.ds(base + uu * c, c)
            g_last = gc[(uu + 1) * c - 1:(uu + 1) * c, :]
            a_sc[crow, 0:c] = intra[sl, sl]
            k_sc[crow, :] = k[sl] * jnp.exp(g_last - gc[sl])
            g_sc[crow, :] = jnp.broadcast_to(jnp.exp(g_last), (c, LANE))
        return carry

    lax.fori_loop(0, s // gr, local_group, 0)

    def scan(n, carry):
        rows = pl.ds(pl.multiple_of(n * c, c), c)
        state = st_sc[...]
        v_new = u_sc[rows, :] - mm(w_sc[rows, :], state)
        o = mm(q_sc[rows, :], state) + mm(a_sc[rows, 0:c], v_new)
        decay_last = g_sc[pl.ds(pl.multiple_of(n * c, c), 1), :]
        st_sc[...] = state * decay_last + lax.dot_general(
            k_sc[rows, :].astype(BF), v_new.astype(BF), (((0,), (0,)), ((), ())), preferred_element_type=F32)
        u_sc[rows, :] = o
        return carry

    lax.fori_loop(0, s // c, scan, 0)
    o = u_sc[...]
    ms = jnp.mean(o * o, axis=-1, keepdims=True)
    gate = zg_ref[...]
    o_ref[...] = (o * lax.rsqrt(ms + EPS) * og_ref[...] * (gate * jax.nn.sigmoid(gate))).astype(o_ref.dtype)


def mixer_d(z, s, conv_w, a_log, dt_bias, o_g):
    m = z.shape[0]
    b = m // s
    cb = Z_D // LANE
    par = jnp.stack([-jnp.exp(a_log.astype(F32)), dt_bias.astype(F32)], axis=0)
    cw = conv_w.astype(F32)

    def col(which):
        return lambda bb, h: (bb, cb + which * HEADS + h)

    return pl.pallas_call(
        functools.partial(_gdn_kernel, s=s),
        grid=(b, HEADS),
        in_specs=[pl.BlockSpec(memory_space=pltpu.SMEM),
                  pl.BlockSpec((s, LANE), col(0)),
                  pl.BlockSpec((s, LANE), col(1)),
                  pl.BlockSpec((s, LANE), col(2)),
                  pl.BlockSpec((s, LANE), col(3)),
                  pl.BlockSpec((s, LANE), lambda bb, h: (bb, cb + 4 * HEADS)),
                  pl.BlockSpec((CONV_W, LANE), lambda bb, h: (0, h)),
                  pl.BlockSpec((CONV_W, LANE), lambda bb, h: (0, HEADS + h)),
                  pl.BlockSpec((CONV_W, LANE), lambda bb, h: (0, 2 * HEADS + h)),
                  pl.BlockSpec((1, LANE), lambda bb, h: (0, 0))],
        out_specs=pl.BlockSpec((s, LANE), lambda bb, h: (bb, h)),
        out_shape=jax.ShapeDtypeStruct((m, HEADS * LANE), BF),
        scratch_shapes=[pltpu.VMEM((CONV_PAD + s, LANE), F32), pltpu.VMEM((s, LANE), F32),
                        pltpu.VMEM((s, LANE), F32), pltpu.VMEM((s, LANE), F32), pltpu.VMEM((s, LANE), F32),
                        pltpu.VMEM((s, LANE), F32), pltpu.VMEM((s, LANE), F32), pltpu.VMEM((s, LANE), F32),
                        pltpu.VMEM((s, LANE), F32), pltpu.VMEM((LANE, LANE), F32)],
        compiler_params=_cparams(("parallel", "arbitrary")),
        name="gated_deltanet",
    )(par, z, z, z, z, z, cw, cw, cw, o_g.reshape(1, LANE).astype(F32))


def _cast_kernel(x_ref, o_ref):
    o_ref[...] = x_ref[...].astype(o_ref.dtype)


def cast_bf16(w, bm=512):
    r, c = w.shape
    bm = min(bm, r)
    return pl.pallas_call(
        _cast_kernel,
        grid=(r // bm,),
        in_specs=[pl.BlockSpec((bm, c), lambda i: (i, 0))],
        out_specs=pl.BlockSpec((bm, c), lambda i: (i, 0)),
        out_shape=jax.ShapeDtypeStruct((r, c), BF),
        compiler_params=_cparams(("parallel",)),
        name="cast_bf16",
    )(w)


IN_COLS = DA_COLS + MLA_COLS + DIL_COLS + GDN_QKV + 2 * HEADS + HEADS * LANE


def _pack_w_in_kernel(w_ref, o_ref):
    rows = w_ref.shape[0]

    def put(dst, src, width):
        o_ref[:, dst:dst + width] = w_ref[:, src:src + width].astype(BF)

    def zero(dst, width):
        o_ref[:, dst:dst + width] = jnp.zeros((rows, width), BF)

    c0 = DA_COLS + MLA_COLS + DIL_COLS
    put(0, 0, DA_COLS + MLA_COLS)
    zero(Z_B + MLA_COLS, Z_C - Z_B - MLA_COLS)
    put(Z_C, DA_COLS + MLA_COLS, DIL_COLS)
    put(Z_D, c0, GDN_QKV)
    put(Z_D + GDN_QKV, c0 + GDN_QKV + 2 * HEADS, HEADS * LANE)
    put(Z_D + GDN_QKV + HEADS * LANE, c0 + GDN_QKV, 2 * HEADS)
    used = Z_D + GDN_QKV + HEADS * LANE + 2 * HEADS
    zero(used, Z_COLS - used)


def pack_w_in(w, bm=128):
    r = w.shape[0]
    return pl.pallas_call(
        _pack_w_in_kernel,
        grid=(r // bm,),
        in_specs=[pl.BlockSpec((bm, IN_COLS), lambda i: (i, 0))],
        out_specs=pl.BlockSpec((bm, Z_COLS), lambda i: (i, 0)),
        out_shape=jax.ShapeDtypeStruct((r, Z_COLS), BF),
        compiler_params=_cparams(("parallel",)),
        name="pack_w_in",
    )(w)


def _cast_stacked(w, bm=512):
    lead, c = w.shape[:-1], w.shape[-1]
    return cast_bf16(w.reshape(-1, c), bm).reshape(*lead, c)


def kernel(x, p, rel_bias, norm_mix_g, w_in, da_lambda, da_q_g, da_k_g, da_o_g, mla_w_uq, mla_w_ukv, mla_cq_g, mla_ckv_g, mla_qn_g, mla_kn_g, mla_qr_g, mla_kr_g, dil_q_g, dil_k_g, gdn_conv_w, gdn_a_log, gdn_dt_bias, gdn_o_g, w_bgate, b_bgate, w_branch, w_out, norm_ffn_g, w_ffn_in, w_ffn_out, norm_ple_g, w_ple, w_ple_gate):
    b, s, d = x.shape
    m = b * s
    bias_a = diff_bias_tiles(rel_bias, s)
    bias_c = dil_bias_tiles(rel_bias)
    cos_t, sin_t = rope_tables(s)
    w_in_p = pack_w_in(w_in.reshape(DEPTH * d, IN_COLS)).reshape(DEPTH, d, Z_COLS)
    w_bgate_b = _cast_stacked(w_bgate)
    w_branch_b = _cast_stacked(w_branch)
    w_out_b = _cast_stacked(w_out)
    w_ffn_in_b = _cast_stacked(w_ffn_in, bm=128)
    w_ffn_out_b = _cast_stacked(w_ffn_out)
    w_ple_gate_b = _cast_stacked(w_ple_gate)
    w_ple_b = _cast_stacked(w_ple)
    w_ukv_b = _cast_stacked(mla_w_ukv)
    p_b = _cast_stacked(p)
    xr = x.reshape(m, d)
    for i in range(DEPTH):
        h = rmsnorm_rows(xr, norm_mix_g[i])
        z = matmul(h, w_in_p[i], F32, name="in_proj")
        lam_init = 0.8 - 0.6 * math.exp(-0.3 * i)
        lq1, lk1, lq2, lk2 = (da_lambda[i, j].astype(F32) for j in range(4))
        lam = jnp.exp(jnp.sum(lq1 * lk1)) - jnp.exp(jnp.sum(lq2 * lk2)) + lam_init
        o_a = mixer_a(z, s, lam, lam_init, da_q_g[i], da_k_g[i], da_o_g[i], bias_a)
        o_b = mixer_b(z, s, mla_w_uq[i], w_ukv_b[i], mla_cq_g[i], mla_ckv_g[i], mla_qn_g[i], mla_kn_g[i],
                      mla_qr_g[i], mla_kr_g[i], cos_t, sin_t)
        o_c = mixer_c(z, s, dil_q_g[i], dil_k_g[i], bias_c)
        o_d = mixer_d(z, s, gdn_conv_w[i], gdn_a_log[i], gdn_dt_bias[i], gdn_o_g[i])
        o_all = jnp.stack([o_a, o_b, o_c, o_d], axis=0)
        merged = gated_merge(h, w_bgate_b[i], b_bgate[i].reshape(N_BRANCH, 1, d), o_all, w_branch_b[i])
        xr = matmul_residual(xr, merged, w_out_b[i], bn=1024, name="out_proj")
        hf = rmsnorm_rows(xr, norm_ffn_g[i])
        act = swiglu_in(hf, w_ffn_in_b[i])
        xr = matmul_residual(xr, act, w_ffn_out_b[i], bm=1024, bn=512, bk=D_FF // 2, name="ffn_out")
        hp = rmsnorm_rows(xr, norm_ple_g[i])
        xr = ple_update(xr, p_b[i].reshape(m, PLE_DIM), w_ple_b[i], hp, w_ple_gate_b[i])
    return xr.reshape(b, s, d)
```

```python
import functools
import math

import numpy as np
import jax
import jax.numpy as jnp
from jax import lax
from jax.experimental import pallas as pl
from jax.experimental.pallas import tpu as pltpu

D_MODEL = 4096
DEPTH = 2
EPS = 1e-6
PLE_DIM = 256
HEADS = 8
N_BRANCH = 4
BRANCH_W = 1024
DA_QK_DIM = 64
MLA_Q_RANK = 1024
MLA_KV_RANK = 512
MLA_NOPE = 128
MLA_ROPE = 64
ROPE_THETA = 10000.0
DIL_GROUPS = ((128, 1), (512, 4), (2048, 16))
DIL_STEPS = 128
GDN_CHUNK = 64
CONV_W = 4
N_BUCKETS = 32
MAX_DIST = 2048
D_FF = 11008

DA_COLS = 3072
MLA_COLS = 1600
DIL_COLS = 9216
GDN_QKV = 3072

LANE = 128
Z_A = 0
Z_B = Z_A + DA_COLS
Z_C = Z_B + MLA_COLS + 64
Z_D = Z_C + DIL_COLS
Z_COLS = 18 * 1024

NEG = -1e30
ATT_T = 256
VMEM_LIMIT = 56 * 1024 * 1024
BF = jnp.bfloat16
F32 = jnp.float32


def _cparams(sem):
    return pltpu.CompilerParams(dimension_semantics=sem, vmem_limit_bytes=VMEM_LIMIT)


def _rmsnorm_kernel(x_ref, g_ref, o_ref):
    x = x_ref[...]
    ms = jnp.mean(x * x, axis=-1, keepdims=True)
    o_ref[...] = (x * lax.rsqrt(ms + EPS) * g_ref[...]).astype(o_ref.dtype)


def rmsnorm_rows(x, g, col_block=0, bm=512):
    m = x.shape[0]
    d = g.shape[-1]
    return pl.pallas_call(
        _rmsnorm_kernel,
        grid=(m // bm,),
        in_specs=[pl.BlockSpec((bm, d), lambda i: (i, col_block)),
                  pl.BlockSpec((1, d), lambda i: (0, 0))],
        out_specs=pl.BlockSpec((bm, d), lambda i: (i, 0)),
        out_shape=jax.ShapeDtypeStruct((m, d), BF),
        compiler_params=_cparams(("parallel",)),
        name="rmsnorm_rows",
    )(x, g.reshape(1, d))


def _mm_kernel(x_ref, w_ref, o_ref):
    o_ref[...] = jnp.dot(x_ref[...], w_ref[...], preferred_element_type=F32).astype(o_ref.dtype)


def _layer_spec(w, block, index_map, layer):
    if layer is None:
        return pl.BlockSpec(block, index_map)
    return pl.BlockSpec((None,) + block, lambda *g: (layer,) + index_map(*g))


def matmul(x, w, out_dtype, bm=1024, bn=1024, layer=None, name="matmul"):
    m, k = x.shape
    n = w.shape[-1]
    bn = min(bn, n)
    bm = min(bm, m)
    return pl.pallas_call(
        _mm_kernel,
        grid=(m // bm, n // bn),
        in_specs=[pl.BlockSpec((bm, k), lambda i, j: (i, 0)),
                  _layer_spec(w, (k, bn), lambda i, j: (0, j), layer)],
        out_specs=pl.BlockSpec((bm, bn), lambda i, j: (i, j)),
        out_shape=jax.ShapeDtypeStruct((m, n), out_dtype),
        compiler_params=_cparams(("parallel", "parallel")),
        name=name,
    )(x, w)


def _mm_residual_kernel(r_ref, x_ref, w_ref, o_ref, *acc, nk):
    part = jnp.dot(x_ref[...], w_ref[...], preferred_element_type=F32)
    if nk == 1:
        o_ref[...] = r_ref[...] + part
    else:
        acc_ref, = acc

        @pl.when(pl.program_id(2) == 0)
        def _():
            acc_ref[...] = r_ref[...]

        total = acc_ref[...] + part
        acc_ref[...] = total
        o_ref[...] = total


def matmul_residual(r, x, w, bm=1024, bn=512, bk=None, layer=None, name="matmul_residual"):
    m, kdim = x.shape
    n = w.shape[-1]
    bk = kdim if bk is None else bk
    bm = min(bm, m)
    nk = kdim // bk
    return pl.pallas_call(
        functools.partial(_mm_residual_kernel, nk=nk),
        grid=(m // bm, n // bn, nk),
        in_specs=[pl.BlockSpec((bm, bn), lambda i, j, k: (i, j)),
                  pl.BlockSpec((bm, bk), lambda i, j, k: (i, k)),
                  _layer_spec(w, (bk, bn), lambda i, j, k: (k, j), layer)],
        out_specs=pl.BlockSpec((bm, bn), lambda i, j, k: (i, j)),
        out_shape=jax.ShapeDtypeStruct((m, n), F32),
        scratch_shapes=[pltpu.VMEM((bm, bn), F32)] if nk > 1 else [],
        compiler_params=_cparams(("parallel", "parallel", "arbitrary")),
        name=name,
    )(r, x, w)


def _merge_kernel(h_ref, wg_ref, b_ref, o_ref, wb_ref, out_ref, acc_ref):
    n = pl.program_id(2)
    gate = jax.nn.sigmoid(jnp.dot(h_ref[...], wg_ref[...], preferred_element_type=F32) + b_ref[...])
    term = gate * jnp.dot(o_ref[...], wb_ref[...], preferred_element_type=F32)

    @pl.when(n == 0)
    def _():
        acc_ref[...] = jnp.zeros(acc_ref.shape, F32)

    total = acc_ref[...] + term
    acc_ref[...] = total
    out_ref[...] = total.astype(out_ref.dtype)


def gated_merge(h, w_bgate, b_bgate, o_all, w_branch, layer, bm=1024, bn=512):
    m, d = h.shape
    bm = min(bm, m)
    return pl.pallas_call(
        _merge_kernel,
        grid=(m // bm, d // bn, N_BRANCH),
        in_specs=[pl.BlockSpec((bm, d), lambda i, j, n: (i, 0)),
                  pl.BlockSpec((None, None, d, bn), lambda i, j, n: (layer, n, 0, j)),
                  pl.BlockSpec((None, 1, bn), lambda i, j, n: (n, 0, j)),
                  pl.BlockSpec((None, bm, BRANCH_W), lambda i, j, n: (n, i, 0)),
                  pl.BlockSpec((None, None, BRANCH_W, bn), lambda i, j, n: (layer, n, 0, j))],
        out_specs=pl.BlockSpec((bm, bn), lambda i, j, n: (i, j)),
        out_shape=jax.ShapeDtypeStruct((m, d), BF),
        scratch_shapes=[pltpu.VMEM((bm, bn), F32)],
        compiler_params=_cparams(("parallel", "parallel", "arbitrary")),
        name="gated_merge",
    )(h, w_bgate, b_bgate, o_all, w_branch)


def _swiglu_kernel(x_ref, wg_ref, wu_ref, o_ref):
    x = x_ref[...]
    g = jnp.dot(x, wg_ref[...], preferred_element_type=F32)
    u = jnp.dot(x, wu_ref[...], preferred_element_type=F32)
    o_ref[...] = (g * jax.nn.sigmoid(g) * u).astype(o_ref.dtype)


def swiglu_in(x, w, layer, bm=2048, bn=256):
    m, k = x.shape
    n = w.shape[-1] // 2
    bm = min(bm, m)
    nb = n // bn
    return pl.pallas_call(
        _swiglu_kernel,
        grid=(m // bm, nb),
        in_specs=[pl.BlockSpec((bm, k), lambda i, j: (i, 0)),
                  pl.BlockSpec((None, k, bn), lambda i, j: (layer, 0, j)),
                  pl.BlockSpec((None, k, bn), lambda i, j: (layer, 0, nb + j))],
        out_specs=pl.BlockSpec((bm, bn), lambda i, j: (i, j)),
        out_shape=jax.ShapeDtypeStruct((m, n), BF),
        compiler_params=_cparams(("parallel", "parallel")),
        name="swiglu_in",
    )(x, w, w)


def _ple_kernel(r_ref, p_ref, wp_ref, h_ref, wg_ref, o_ref):
    e = jnp.dot(p_ref[...], wp_ref[...], preferred_element_type=F32)
    g = jnp.dot(h_ref[...], wg_ref[...], preferred_element_type=F32)
    o_ref[...] = r_ref[...] + e * jax.nn.sigmoid(g)


def ple_update(r, p, w_ple, hp, w_gate, layer, bm=1024, bn=512):
    m, d = r.shape
    kp = p.shape[-1]
    bm = min(bm, m)
    return pl.pallas_call(
        _ple_kernel,
        grid=(m // bm, d // bn),
        in_specs=[pl.BlockSpec((bm, bn), lambda i, j: (i, j)),
                  pl.BlockSpec((None, bm, kp), lambda i, j: (layer, i, 0)),
                  pl.BlockSpec((None, kp, bn), lambda i, j: (layer, 0, j)),
                  pl.BlockSpec((bm, d), lambda i, j: (i, 0)),
                  pl.BlockSpec((None, d, bn), lambda i, j: (layer, 0, j))],
        out_specs=pl.BlockSpec((bm, bn), lambda i, j: (i, j)),
        out_shape=jax.ShapeDtypeStruct((m, d), F32),
        compiler_params=_cparams(("parallel", "parallel")),
        name="ple_update",
    )(r, p, w_ple, hp, w_gate)


def _static_buckets(dist):
    max_exact = N_BUCKETS // 2
    d = np.maximum(np.asarray(dist), 0)
    large = max_exact + (np.log(np.maximum(d, 1).astype(np.float32) / np.float32(max_exact))
                         / np.float32(math.log(MAX_DIST / max_exact))
                         * np.float32(N_BUCKETS - max_exact)).astype(np.int32)
    large = np.minimum(large, N_BUCKETS - 1)
    return np.where(d < max_exact, d, large).astype(np.int32)


def _toeplitz_tiles(f, t):
    hh, n = f.shape
    big = n + t - 1
    fpad = jnp.concatenate([jnp.full((hh, t - 1), NEG, f.dtype), f], axis=1)
    flat = jnp.tile(fpad, (1, t + 1))[:, :t * (big + 1)]
    w = flat.reshape(hh, t, big + 1)[:, :, :n]
    w = jnp.flip(w.reshape(hh, t, n // t, t), axis=3)
    return w.transpose(0, 2, 1, 3)


def _bias_tile_kernel(prev_ref, cur_ref, o_ref, *, t):
    row = jnp.concatenate([prev_ref[...], cur_ref[...]], axis=1)
    rolled = pltpu.roll(jnp.broadcast_to(row, (t, 2 * t)), 0, 1, stride=1, stride_axis=0)
    o_ref[...] = rolled[:, t:]


def diff_bias_tiles(rel_bias, s):
    t = min(ATT_T, s)
    nq = s // t
    f = jnp.take(rel_bias[:, :HEADS], _static_buckets(np.arange(s)), axis=0).T.astype(F32)
    fb = jnp.concatenate([jnp.full((HEADS, t), NEG, F32), f], axis=1).reshape(HEADS, nq + 1, 1, t)
    return pl.pallas_call(
        functools.partial(_bias_tile_kernel, t=t),
        grid=(HEADS, nq),
        in_specs=[pl.BlockSpec((None, None, 1, t), lambda h, d: (h, d, 0, 0)),
                  pl.BlockSpec((None, None, 1, t), lambda h, d: (h, d + 1, 0, 0))],
        out_specs=pl.BlockSpec((None, None, t, t), lambda h, d: (h, d, 0, 0)),
        out_shape=jax.ShapeDtypeStruct((HEADS, nq, t, t), F32),
        compiler_params=_cparams(("parallel", "parallel")),
        name="diff_bias_tiles",
    )(fb, fb)


def dil_bias_tiles(rel_bias):
    out = []
    for gi, (_, dil) in enumerate(DIL_GROUPS):
        lo = HEADS + gi * HEADS
        steps = np.arange(2 * DIL_STEPS)
        f = jnp.take(rel_bias[:, lo:lo + HEADS], _static_buckets(steps * dil), axis=0).T.astype(F32)
        f = jnp.where(steps[None, :] <= DIL_STEPS, f, NEG)
        tiles = _toeplitz_tiles(f, DIL_STEPS)
        general = jnp.concatenate([tiles[:, 1], tiles[:, 0]], axis=-1)
        first = jnp.concatenate([jnp.full_like(tiles[:, 1], NEG), tiles[:, 0]], axis=-1)
        out.append(jnp.stack([general, first], axis=1))
    return jnp.stack(out, axis=0)


def rope_tables(s):
    half = MLA_ROPE // 2
    inv = ROPE_THETA ** (-jnp.arange(half, dtype=F32) / half)
    ang = jnp.arange(s).astype(F32)[:, None] * inv[None, :]
    cos, sin = jnp.cos(ang), jnp.sin(ang)
    zero = jnp.zeros((s, LANE - MLA_ROPE), F32)
    return (jnp.concatenate([cos, cos, zero], axis=1), jnp.concatenate([-sin, sin, zero], axis=1))


def _rope_tile(t, c, s):
    half = MLA_ROPE // 2
    swapped = pltpu.roll(t, half, axis=1) + pltpu.roll(t, LANE - half, axis=1)
    return t * c + swapped * s


def _flash_kernel(lam_ref, qt_ref, k_ref, vt_ref, *rest, diff, t, scale, post_scale):
    if diff:
        bias_ref, g_ref, o_ref, m_sc, l_sc, acc_sc = rest
    else:
        o_ref, m_sc, l_sc, acc_sc = rest
    i = pl.program_id(2)
    m_sc[...] = jnp.full(m_sc.shape, NEG, F32)
    l_sc[...] = jnp.zeros(l_sc.shape, F32)
    acc_sc[...] = jnp.zeros(acc_sc.shape, F32)
    qts = [qt_ref[0], qt_ref[1]] if diff else [qt_ref[:, 0:t], qt_ref[:, t:2 * t]]

    def load_kv(j):
        start = pl.multiple_of(j * t, t)
        return k_ref[pl.ds(start, t), :], vt_ref[:, pl.ds(start, t)]

    def step(work):
        scores = [jnp.dot(kv[0], qts[c], preferred_element_type=F32) for c, kv, _, _ in work]
        for (c, kv, bias, diagonal), s in zip(work, scores):
            if scale != 1.0:
                s = s * scale
            if bias is not None:
                s = bias + s
            elif diagonal:
                key_i = lax.broadcasted_iota(jnp.int32, (t, t), 0)
                qry_i = lax.broadcasted_iota(jnp.int32, (t, t), 1)
                s = jnp.where(key_i <= qry_i, s, NEG)
            m_prev = m_sc[c]
            m_new = jnp.maximum(m_prev, jnp.max(s, axis=0, keepdims=True))
            alpha = jnp.exp(m_prev - m_new)
            p = jnp.exp(s - m_new)
            l_sc[c] = alpha * l_sc[c] + jnp.sum(p, axis=0, keepdims=True)
            m_sc[c] = m_new
            acc_sc[c] = alpha * acc_sc[c] + jnp.dot(kv[1], p.astype(BF), preferred_element_type=F32)

    def sweep(n, work):
        def body(jj, carry):
            step(work(4 * jj) + work(4 * jj + 1) + work(4 * jj + 2) + work(4 * jj + 3))
            return carry

        lax.fori_loop(0, n // 4, body, 0)
        done = (n // 4) * 4

        @pl.when(n % 4 >= 2)
        def _():
            step(work(done) + work(done + 1))

        @pl.when(n % 2 == 1)
        def _():
            step(work(n - 1))

    if diff:
        def work(j):
            kv = load_kv(j)
            bias = bias_ref[i - j]
            return [(0, kv, bias, False), (1, kv, bias, False)]

        sweep(i + 1, work)
        out_t = acc_sc[0] / l_sc[0] - lam_ref[0] * (acc_sc[1] / l_sc[1])
        o = out_t.T
        ms = jnp.mean(o * o, axis=-1, keepdims=True)
        o_ref[...] = (o * lax.rsqrt(ms + EPS) * g_ref[...] * post_scale).astype(o_ref.dtype)
    else:
        def work(j):
            kv = load_kv(j)
            return [(0, kv, None, False), (1, kv, None, False)]

        sweep(2 * i, work)
        kv = load_kv(2 * i)
        step([(0, kv, None, True), (1, kv, None, False)])
        step([(1, load_kv(2 * i + 1), None, True)])
        o_ref[0:t, :] = (acc_sc[0] / l_sc[0]).T.astype(o_ref.dtype)
        o_ref[t:2 * t, :] = (acc_sc[1] / l_sc[1]).T.astype(o_ref.dtype)


def flash_attention(q, k, vt, s, *, diff, dk, scale, bias=None, lam=None, gain=None, post_scale=1.0, name):
    m = k.shape[0]
    b = m // s
    t = min(ATT_T, s)
    lam = jnp.zeros((1,), F32) if lam is None else lam.reshape(1).astype(F32)
    if diff:
        nq = s // t
        q_spec = pl.BlockSpec((2, dk, t), lambda bb, h, i: (0, h, bb * nq + i))
        o_rows = t
    else:
        nq = s // (2 * t)
        q_spec = pl.BlockSpec((dk, 2 * t), lambda bb, h, i: (h, bb * nq + i))
        o_rows = 2 * t
    in_specs = [pl.BlockSpec(memory_space=pltpu.SMEM), q_spec,
                pl.BlockSpec((s, dk), lambda bb, h, i: (bb, h)),
                pl.BlockSpec((LANE, s), lambda bb, h, i: (h, bb))]
    args = [lam, q, k, vt]
    if diff:
        in_specs += [pl.BlockSpec((None, nq, t, t), lambda bb, h, i: (h, 0, 0, 0)),
                     pl.BlockSpec((1, LANE), lambda bb, h, i: (0, 0))]
        args += [bias, gain.reshape(1, LANE).astype(F32)]
    return pl.pallas_call(
        functools.partial(_flash_kernel, diff=diff, t=t, scale=scale, post_scale=post_scale),
        grid=(b, HEADS, nq),
        in_specs=in_specs,
        out_specs=pl.BlockSpec((o_rows, LANE), lambda bb, h, i: (bb * nq + i, h)),
        out_shape=jax.ShapeDtypeStruct((m, HEADS * LANE), BF),
        scratch_shapes=[pltpu.VMEM((2, 1, t), F32), pltpu.VMEM((2, 1, t), F32), pltpu.VMEM((2, LANE, t), F32)],
        compiler_params=_cparams(("parallel", "parallel", "arbitrary")),
        name=name,
    )(*args)


def _prep_a_kernel(z_ref, qg_ref, kg_ref, qz_ref, kn_ref, vt_ref):
    lane = lax.broadcasted_iota(jnp.int32, (1, LANE), 1)
    lo = lane < DA_QK_DIM

    def norm_halves(x, g):
        x2 = x * x
        s_lo = jnp.sum(jnp.where(lo, x2, 0.0), axis=-1, keepdims=True)
        s_hi = jnp.sum(jnp.where(lo, 0.0, x2), axis=-1, keepdims=True)
        inv = lax.rsqrt(jnp.where(lo, s_lo, s_hi) * (1.0 / DA_QK_DIM) + EPS)
        return x * inv * g

    scale = DA_QK_DIM ** -0.5
    for h in range(HEADS):
        cq = slice(h * LANE, (h + 1) * LANE)
        ck = slice(HEADS * LANE + h * LANE, HEADS * LANE + (h + 1) * LANE)
        q = norm_halves(z_ref[:, cq], qg_ref[...]) * scale
        qt = q.T
        sub_lo = lax.broadcasted_iota(jnp.int32, (LANE, 1), 0) < DA_QK_DIM
        qz_ref[0, cq, :] = jnp.where(sub_lo, qt, 0.0).astype(BF)
        qz_ref[1, cq, :] = jnp.where(sub_lo, 0.0, qt).astype(BF)
        kn_ref[:, cq] = norm_halves(z_ref[:, ck], kg_ref[...]).astype(BF)
        cv = slice(2 * HEADS * LANE + h * LANE, 2 * HEADS * LANE + (h + 1) * LANE)
        vt_ref[cq, :] = z_ref[:, cv].T.astype(BF)


def prep_a(z, q_g, k_g, bm=256):
    m = z.shape[0]
    bm = min(bm, m)
    w = HEADS * LANE
    g2 = lambda g: jnp.concatenate([g, g]).reshape(1, LANE).astype(F32)
    return pl.pallas_call(
        _prep_a_kernel,
        grid=(m // bm,),
        in_specs=[pl.BlockSpec((bm, DA_COLS), lambda i: (i, Z_A // DA_COLS)),
                  pl.BlockSpec((1, LANE), lambda i: (0, 0)),
                  pl.BlockSpec((1, LANE), lambda i: (0, 0))],
        out_specs=[pl.BlockSpec((2, w, bm), lambda i: (0, 0, i)),
                   pl.BlockSpec((bm, w), lambda i: (i, 0)),
                   pl.BlockSpec((w, bm), lambda i: (0, i))],
        out_shape=[jax.ShapeDtypeStruct((2, w, m), BF), jax.ShapeDtypeStruct((m, w), BF),
                   jax.ShapeDtypeStruct((w, m), BF)],
        compiler_params=_cparams(("parallel",)),
        name="prep_a",
    )(z, g2(q_g), g2(k_g))


def mixer_a(z, s, lam, lam_init, q_g, k_g, o_g, bias_tiles):
    qz, kn, vt = prep_a(z, q_g, k_g)
    return flash_attention(qz, kn, vt, s, diff=True, dk=LANE, scale=1.0, bias=bias_tiles, lam=lam, gain=o_g,
                           post_scale=1.0 - lam_init, name="diff_attention")


def _prep_kr_kernel(z_ref, g_ref, c_ref, s_ref, o_ref):
    x = z_ref[...]
    ms = jnp.sum(x * x, axis=-1, keepdims=True) * (1.0 / MLA_ROPE)
    o_ref[...] = _rope_tile(x * lax.rsqrt(ms + EPS) * g_ref[...], c_ref[...], s_ref[...]).astype(o_ref.dtype)


def prep_kr(z, kr_g, cos_t, sin_t, s, bm=512):
    m = z.shape[0]
    bm = min(bm, s)
    nsb = s // bm
    g = jnp.concatenate([kr_g, jnp.zeros((LANE - MLA_ROPE,), kr_g.dtype)]).reshape(1, LANE).astype(F32)
    return pl.pallas_call(
        _prep_kr_kernel,
        grid=(m // bm,),
        in_specs=[pl.BlockSpec((bm, LANE), lambda i: (i, (Z_B + MLA_Q_RANK + MLA_KV_RANK) // LANE)),
                  pl.BlockSpec((1, LANE), lambda i: (0, 0)),
                  pl.BlockSpec((bm, LANE), lambda i: (i % nsb, 0)),
                  pl.BlockSpec((bm, LANE), lambda i: (i % nsb, 0))],
        out_specs=pl.BlockSpec((bm, LANE), lambda i: (i, 0)),
        out_shape=jax.ShapeDtypeStruct((m, LANE), BF),
        compiler_params=_cparams(("parallel",)),
        name="prep_kr",
    )(z, g, cos_t, sin_t)


def _prep_b_kernel(q_ref, kv_ref, kr_ref, qn_g, qr_g, kn_g, c_ref, s_ref, qc_ref, kc_ref, vt_ref):
    def norm(x, g, width):
        ms = jnp.sum(x * x, axis=-1, keepdims=True) * (1.0 / width)
        return x * lax.rsqrt(ms + EPS) * g

    kr = kr_ref[...]
    for h in range(HEADS):
        c0 = slice(2 * h * LANE, (2 * h + 1) * LANE)
        c1 = slice((2 * h + 1) * LANE, (2 * h + 2) * LANE)
        qc_ref[c0, :] = norm(q_ref[:, c0], qn_g[...], MLA_NOPE).T.astype(BF)
        qc_ref[c1, :] = _rope_tile(norm(q_ref[:, c1], qr_g[...], MLA_ROPE), c_ref[...], s_ref[...]).T.astype(BF)
        kc_ref[:, c0] = norm(kv_ref[:, c0], kn_g[...], MLA_NOPE).astype(BF)
        kc_ref[:, c1] = kr
        vt_ref[h * LANE:(h + 1) * LANE, :] = kv_ref[:, c1].T.astype(BF)


def prep_b(q_up, kv_up, kr, qn_g, qr_g, kn_g, cos_t, sin_t, s, bm=256):
    m = q_up.shape[0]
    bm = min(bm, s)
    nsb = s // bm
    w2 = 2 * HEADS * LANE
    row = lambda g: g.reshape(1, LANE).astype(F32)
    qr_pad = jnp.concatenate([qr_g, jnp.zeros((LANE - MLA_ROPE,), qr_g.dtype)])
    return pl.pallas_call(
        _prep_b_kernel,
        grid=(m // bm,),
        in_specs=[pl.BlockSpec((bm, w2), lambda i: (i, 0)),
                  pl.BlockSpec((bm, w2), lambda i: (i, 0)),
                  pl.BlockSpec((bm, LANE), lambda i: (i, 0)),
                  pl.BlockSpec((1, LANE), lambda i: (0, 0)),
                  pl.BlockSpec((1, LANE), lambda i: (0, 0)),
                  pl.BlockSpec((1, LANE), lambda i: (0, 0)),
                  pl.BlockSpec((bm, LANE), lambda i: (i % nsb, 0)),
                  pl.BlockSpec((bm, LANE), lambda i: (i % nsb, 0))],
        out_specs=[pl.BlockSpec((w2, bm), lambda i: (0, i)),
                   pl.BlockSpec((bm, w2), lambda i: (i, 0)),
                   pl.BlockSpec((HEADS * LANE, bm), lambda i: (0, i))],
        out_shape=[jax.ShapeDtypeStruct((w2, m), BF), jax.ShapeDtypeStruct((m, w2), BF),
                   jax.ShapeDtypeStruct((HEADS * LANE, m), BF)],
        compiler_params=_cparams(("parallel",)),
        name="prep_b",
    )(q_up, kv_up, kr, row(qn_g), row(qr_pad), row(kn_g), cos_t, sin_t)


def pack_w_uq(w):
    w = w.reshape(MLA_Q_RANK, HEADS, MLA_NOPE + MLA_ROPE)
    w = jnp.pad(w, ((0, 0), (0, 0), (0, 2 * LANE - MLA_NOPE - MLA_ROPE)))
    return w.reshape(MLA_Q_RANK, HEADS * 2 * LANE).astype(BF)


def mixer_b(z, s, w_uq, w_ukv, cq_g, ckv_g, qn_g, kn_g, qr_g, kr_g, cos_t, sin_t, layer=None):
    c_q = rmsnorm_rows(z, cq_g, col_block=Z_B // MLA_Q_RANK)
    c_kv = rmsnorm_rows(z, ckv_g, col_block=(Z_B + MLA_Q_RANK) // MLA_KV_RANK)
    kr = prep_kr(z, kr_g, cos_t, sin_t, s)
    q_up = matmul(c_q, pack_w_uq(w_uq), F32, name="mla_q_up")
    kv_up = matmul(c_kv, w_ukv, F32, layer=layer, name="mla_kv_up")
    qc, kc, vt = prep_b(q_up, kv_up, kr, qn_g, qr_g, kn_g, cos_t, sin_t, s)
    return flash_attention(qc, kc, vt, s, diff=False, dk=2 * LANE, scale=(MLA_NOPE + MLA_ROPE) ** -0.5,
                           name="mla_attention")


DIL_PAD = DIL_STEPS * max(d for _, d in DIL_GROUPS)


def _dil_kernel(zq_ref, zk_ref, zv_ref, bias_ref, qg_ref, kg_ref, o_ref, q_sc, k_sc, v_sc, og_sc, lse_sc, *, s):
    g = pl.program_id(2)
    ngroups = len(DIL_GROUPS)

    def norm(x, gain):
        ms = jnp.mean(x * x, axis=-1, keepdims=True)
        return x * lax.rsqrt(ms + EPS) * gain

    zeros = jnp.zeros((DIL_PAD, LANE), F32)
    k_sc[0:DIL_PAD, :] = zeros
    v_sc[0:DIL_PAD, :] = zeros
    q_sc[...] = norm(zq_ref[...], qg_ref[...])
    k_sc[DIL_PAD:, :] = norm(zk_ref[...], kg_ref[...])
    v_sc[DIL_PAD:, :] = zv_ref[...]
    scale = LANE ** -0.5

    def group(gi, dil):
        nsub = s // DIL_STEPS

        def body(tt, carry):
            c = tt % dil
            n = tt // dil
            q0 = c + dil * DIL_STEPS * n
            rows = pl.ds(q0, DIL_STEPS, stride=dil) if dil > 1 else pl.ds(q0, DIL_STEPS)
            k0 = q0 + DIL_PAD - dil * DIL_STEPS
            band = pl.ds(k0, 2 * DIL_STEPS, stride=dil) if dil > 1 else pl.ds(k0, 2 * DIL_STEPS)
            qs = q_sc[rows, :].astype(BF)
            ks = k_sc[band, :].astype(BF)
            vs = v_sc[band, :].astype(BF)
            first = jnp.where(n == 0, 1, 0)
            logits = lax.dot_general(qs, ks, (((1,), (1,)), ((), ())), preferred_element_type=F32) * scale
            logits = logits + bias_ref[first]
            mx = jnp.max(logits, axis=-1, keepdims=True)
            e = jnp.exp(logits - mx)
            den = jnp.sum(e, axis=-1, keepdims=True)
            o = jnp.dot((e / den).astype(BF), vs, preferred_element_type=F32)
            og_sc[gi, rows, :] = o
            lse_sc[gi, rows, :] = jnp.broadcast_to(mx + jnp.log(den), (DIL_STEPS, LANE))
            return carry

        lax.fori_loop(0, nsub, body, 0, unroll=8)

    for gi, (_, dil) in enumerate(DIL_GROUPS):
        pl.when(g == gi)(functools.partial(group, gi, dil))

    @pl.when(g == ngroups - 1)
    def _():
        lses = [lse_sc[gi] for gi in range(ngroups)]
        mx = functools.reduce(jnp.maximum, lses)
        ws = [jnp.exp(l - mx) for l in lses]
        tot = functools.reduce(lambda a, b2: a + b2, ws)
        acc = ws[0] * og_sc[0]
        for gi in range(1, ngroups):
            acc = acc + ws[gi] * og_sc[gi]
        o_ref[...] = (acc / tot).astype(o_ref.dtype)


def mixer_c(z, s, q_g, k_g, bias_c):
    m = z.shape[0]
    b = m // s
    ngroups = len(DIL_GROUPS)
    cb = Z_C // LANE

    def col(which):
        return lambda bb, h, g: (bb, cb + (g * 3 + which) * HEADS + h)

    row = lambda g: g.reshape(1, LANE).astype(F32)
    return pl.pallas_call(
        functools.partial(_dil_kernel, s=s),
        grid=(b, HEADS, ngroups),
        in_specs=[pl.BlockSpec((s, LANE), col(0)),
                  pl.BlockSpec((s, LANE), col(1)),
                  pl.BlockSpec((s, LANE), col(2)),
                  pl.BlockSpec((None, None, 2, DIL_STEPS, 2 * DIL_STEPS), lambda bb, h, g: (g, h, 0, 0, 0)),
                  pl.BlockSpec((1, LANE), lambda bb, h, g: (0, 0)),
                  pl.BlockSpec((1, LANE), lambda bb, h, g: (0, 0))],
        out_specs=pl.BlockSpec((s, LANE), lambda bb, h, g: (bb, h)),
        out_shape=jax.ShapeDtypeStruct((m, HEADS * LANE), BF),
        scratch_shapes=[pltpu.VMEM((s, LANE), F32), pltpu.VMEM((DIL_PAD + s, LANE), F32),
                        pltpu.VMEM((DIL_PAD + s, LANE), F32), pltpu.VMEM((ngroups, s, LANE), F32),
                        pltpu.VMEM((ngroups, s, LANE), F32)],
        compiler_params=_cparams(("parallel", "parallel", "arbitrary")),
        name="dilated_attention",
    )(z, z, z, bias_c, row(q_g), row(k_g))


CONV_PAD = 8
GDN_GROUP = 4
GDN_BASE = 8


def _gdn_kernel(par_ref, zq_ref, zk_ref, zv_ref, zg_ref, zab_ref, wq_ref, wk_ref, wv_ref, og_ref, o_ref,
                x_sc, q_sc, k_sc, v_sc, g_sc, b_sc, u_sc, w_sc, a_sc, st_sc, *, s):
    h = pl.program_id(1)
    c = GDN_CHUNK

    def conv_silu(z_ref, w_ref):
        x_sc[0:CONV_PAD, :] = jnp.zeros((CONV_PAD, LANE), F32)
        x_sc[CONV_PAD:, :] = z_ref[...]
        y = x_sc[CONV_PAD:, :] * w_ref[CONV_W - 1:CONV_W, :]
        for i in range(CONV_W - 1):
            off = CONV_PAD - (CONV_W - 1) + i
            y = y + x_sc[off:off + s, :] * w_ref[i:i + 1, :]
        return y * jax.nn.sigmoid(y)

    def l2(x):
        return x * lax.rsqrt(jnp.sum(x * x, axis=-1, keepdims=True) + EPS)

    q_sc[...] = l2(conv_silu(zq_ref, wq_ref)) * (LANE ** -0.5)
    k_sc[...] = l2(conv_silu(zk_ref, wk_ref))
    v_sc[...] = conv_silu(zv_ref, wv_ref)
    lane = lax.broadcasted_iota(jnp.int32, (1, LANE), 1)
    ab = zab_ref[...]
    a_col = jnp.sum(jnp.where(lane == h, ab, 0.0), axis=-1, keepdims=True)
    b_col = jnp.sum(jnp.where(lane == h + HEADS, ab, 0.0), axis=-1, keepdims=True)
    a_neg_exp = par_ref[0, h]
    dt_bias = par_ref[1, h]
    g_sc[...] = jnp.broadcast_to(a_neg_exp * jax.nn.softplus(a_col + dt_bias), (s, LANE))
    b_sc[...] = jnp.broadcast_to(jax.nn.sigmoid(b_col), (s, LANE))
    st_sc[...] = jnp.zeros(st_sc.shape, F32)

    gr = GDN_GROUP * c
    r_i = lax.broadcasted_iota(jnp.int32, (gr, gr), 0)
    c_i = lax.broadcasted_iota(jnp.int32, (gr, gr), 1)
    blk = {}
    size = GDN_BASE
    while size <= c:
        blk[size] = (r_i // size) == (c_i // size)
        size *= 2
    same = blk[c]
    tril = same & (r_i >= c_i)
    strict = same & (r_i > c_i)
    tril_b = tril.astype(BF)
    triu_b = (same & (r_i <= c_i)).astype(BF)
    eye = (r_i == c_i).astype(F32)

    def mm(a, b2):
        return jnp.dot(a.astype(BF), b2.astype(BF), preferred_element_type=F32)

    def mm_t(a, b2):
        return lax.dot_general(a.astype(BF), b2.astype(BF), (((1,), (1,)), ((), ())), preferred_element_type=F32)

    def split2(x):
        hi = x.astype(BF)
        return hi, (x - hi.astype(F32)).astype(BF)

    def split3(x):
        hi = x.astype(BF)
        r1 = x - hi.astype(F32)
        mid = r1.astype(BF)
        return hi, mid, (r1 - mid.astype(F32)).astype(BF)

    def mm_hi(a, b2):
        a_hi, a_lo = split2(a)
        b_hi, b_lo = split2(b2)
        d = functools.partial(jnp.dot, preferred_element_type=F32)
        return d(a_hi, b_hi) + (d(a_hi, b_lo) + d(a_lo, b_hi))

    def widen(x):
        return jnp.concatenate([x] * (gr // LANE), axis=1)

    def local_group(gi, carry):
        base = pl.multiple_of(gi * gr, gr)
        rows = pl.ds(base, gr)
        q = q_sc[rows, :]
        k = k_sc[rows, :]
        beta = b_sc[rows, :]
        parts = split3(g_sc[rows, :])
        gc = sum(jnp.dot(tril_b, part, preferred_element_type=F32) for part in parts)
        g_row = sum(lax.dot_general(widen(part), triu_b, (((0,), (0,)), ((), ())), preferred_element_type=F32)
                    for part in parts)
        decay = jnp.exp(jnp.where(tril, widen(gc) - g_row, NEG))
        kb = k * beta
        vb = v_sc[rows, :] * beta
        lower = jnp.where(strict, mm_t(kb, k) * decay, 0.0)
        neg = -jnp.where(blk[GDN_BASE], lower, 0.0)
        p1 = mm_hi(neg, neg)
        tmat = eye + neg
        tmat = tmat + mm_hi(tmat, p1)
        tmat = tmat + mm_hi(tmat, mm_hi(p1, p1))
        size = GDN_BASE
        while size < c:
            off = jnp.where(blk[2 * size] & jnp.logical_not(blk[size]), lower, 0.0)
            tmat = tmat - mm_hi(mm_hi(tmat, off), tmat)
            size *= 2
        eg = jnp.exp(gc)
        u_sc[rows, :] = mm(tmat, vb)
        w_sc[rows, :] = mm(tmat, kb * eg)
        intra = mm_t(q, k) * decay
        q_sc[rows, :] = q * eg
        for uu in range(GDN_GROUP):
            sl = slice(uu * c, (uu + 1) * c)
            crow = pl.ds(base + uu * c, c)
            g_last = gc[(uu + 1) * c - 1:(uu + 1) * c, :]
            a_sc[crow, 0:c] = intra[sl, sl]
            k_sc[crow, :] = k[sl] * jnp.exp(g_last - gc[sl])
            g_sc[crow, :] = jnp.broadcast_to(jnp.exp(g_last), (c, LANE))
        return carry


    def scan(n, carry):
        rows = pl.ds(pl.multiple_of(n * c, c), c)
        state = st_sc[...]
        v_new = u_sc[rows, :] - mm(w_sc[rows, :], state)
        o = mm(q_sc[rows, :], state) + mm(a_sc[rows, 0:c], v_new)
        decay_last = g_sc[pl.ds(pl.multiple_of(n * c, c), 1), :]
        st_sc[...] = state * decay_last + lax.dot_general(
            k_sc[rows, :].astype(BF), v_new.astype(BF), (((0,), (0,)), ((), ())), preferred_element_type=F32)
        u_sc[rows, :] = o
        return carry

    ngroups = s // gr
    local_group(0, 0)

    def pipelined(g, carry):
        local_group(g + 1, 0)
        for uu in range(GDN_GROUP):
            scan(g * GDN_GROUP + uu, 0)
        return carry

    lax.fori_loop(0, ngroups - 1, pipelined, 0)
    for uu in range(GDN_GROUP):
        scan((ngroups - 1) * GDN_GROUP + uu, 0)
    o = u_sc[...]
    ms = jnp.mean(o * o, axis=-1, keepdims=True)
    gate = zg_ref[...]
    o_ref[...] = (o * lax.rsqrt(ms + EPS) * og_ref[...] * (gate * jax.nn.sigmoid(gate))).astype(o_ref.dtype)


def mixer_d(z, s, conv_w, a_log, dt_bias, o_g):
    m = z.shape[0]
    b = m // s
    cb = Z_D // LANE
    par = jnp.stack([-jnp.exp(a_log.astype(F32)), dt_bias.astype(F32)], axis=0)
    cw = conv_w.astype(F32)

    def col(which):
        return lambda bb, h: (bb, cb + which * HEADS + h)

    return pl.pallas_call(
        functools.partial(_gdn_kernel, s=s),
        grid=(b, HEADS),
        in_specs=[pl.BlockSpec(memory_space=pltpu.SMEM),
                  pl.BlockSpec((s, LANE), col(0)),
                  pl.BlockSpec((s, LANE), col(1)),
                  pl.BlockSpec((s, LANE), col(2)),
                  pl.BlockSpec((s, LANE), col(3)),
                  pl.BlockSpec((s, LANE), lambda bb, h: (bb, cb + 4 * HEADS)),
                  pl.BlockSpec((CONV_W, LANE), lambda bb, h: (0, h)),
                  pl.BlockSpec((CONV_W, LANE), lambda bb, h: (0, HEADS + h)),
                  pl.BlockSpec((CONV_W, LANE), lambda bb, h: (0, 2 * HEADS + h)),
                  pl.BlockSpec((1, LANE), lambda bb, h: (0, 0))],
        out_specs=pl.BlockSpec((s, LANE), lambda bb, h: (bb, h)),
        out_shape=jax.ShapeDtypeStruct((m, HEADS * LANE), BF),
        scratch_shapes=[pltpu.VMEM((CONV_PAD + s, LANE), F32), pltpu.VMEM((s, LANE), F32),
                        pltpu.VMEM((s, LANE), F32), pltpu.VMEM((s, LANE), F32), pltpu.VMEM((s, LANE), F32),
                        pltpu.VMEM((s, LANE), F32), pltpu.VMEM((s, LANE), F32), pltpu.VMEM((s, LANE), F32),
                        pltpu.VMEM((s, LANE), F32), pltpu.VMEM((LANE, LANE), F32)],
        compiler_params=_cparams(("parallel", "arbitrary")),
        name="gated_deltanet",
    )(par, z, z, z, z, z, cw, cw, cw, o_g.reshape(1, LANE).astype(F32))


def _cast_kernel(x_ref, o_ref):
    o_ref[...] = x_ref[...].astype(o_ref.dtype)


def cast_bf16(w, bm=512):
    r, c = w.shape
    bm = min(bm, r)
    return pl.pallas_call(
        _cast_kernel,
        grid=(r // bm,),
        in_specs=[pl.BlockSpec((bm, c), lambda i: (i, 0))],
        out_specs=pl.BlockSpec((bm, c), lambda i: (i, 0)),
        out_shape=jax.ShapeDtypeStruct((r, c), BF),
        compiler_params=_cparams(("parallel",)),
        name="cast_bf16",
    )(w)


IN_COLS = DA_COLS + MLA_COLS + DIL_COLS + GDN_QKV + 2 * HEADS + HEADS * LANE


def _pack_w_in_kernel(w_ref, o_ref):
    rows = w_ref.shape[0]

    def put(dst, src, width):
        o_ref[:, dst:dst + width] = w_ref[:, src:src + width]

    def zero(dst, width):
        o_ref[:, dst:dst + width] = jnp.zeros((rows, width), BF)

    c0 = DA_COLS + MLA_COLS + DIL_COLS
    put(0, 0, DA_COLS + MLA_COLS)
    zero(Z_B + MLA_COLS, Z_C - Z_B - MLA_COLS)
    put(Z_C, DA_COLS + MLA_COLS, DIL_COLS)
    put(Z_D, c0, GDN_QKV)
    put(Z_D + GDN_QKV, c0 + GDN_QKV + 2 * HEADS, HEADS * LANE)
    put(Z_D + GDN_QKV + HEADS * LANE, c0 + GDN_QKV, 2 * HEADS)
    used = Z_D + GDN_QKV + HEADS * LANE + 2 * HEADS
    zero(used, Z_COLS - used)


def pack_w_in(w, bm=256):
    r = w.shape[0]
    return pl.pallas_call(
        _pack_w_in_kernel,
        grid=(r // bm,),
        in_specs=[pl.BlockSpec((bm, IN_COLS), lambda i: (i, 0))],
        out_specs=pl.BlockSpec((bm, Z_COLS), lambda i: (i, 0)),
        out_shape=jax.ShapeDtypeStruct((r, Z_COLS), BF),
        compiler_params=_cparams(("parallel",)),
        name="pack_w_in",
    )(w)


def _cast_stacked(w, bm=512):
    lead, c = w.shape[:-1], w.shape[-1]
    return cast_bf16(w.reshape(-1, c), bm).reshape(*lead, c)


def kernel(x, p, rel_bias, norm_mix_g, w_in, da_lambda, da_q_g, da_k_g, da_o_g, mla_w_uq, mla_w_ukv, mla_cq_g, mla_ckv_g, mla_qn_g, mla_kn_g, mla_qr_g, mla_kr_g, dil_q_g, dil_k_g, gdn_conv_w, gdn_a_log, gdn_dt_bias, gdn_o_g, w_bgate, b_bgate, w_branch, w_out, norm_ffn_g, w_ffn_in, w_ffn_out, norm_ple_g, w_ple, w_ple_gate):
    b, s, d = x.shape
    m = b * s
    bias_a = diff_bias_tiles(rel_bias, s)
    bias_c = dil_bias_tiles(rel_bias)
    cos_t, sin_t = rope_tables(s)
    w_in_p = pack_w_in(w_in.astype(BF).reshape(DEPTH * d, IN_COLS)).reshape(DEPTH, d, Z_COLS)
    w_bgate_b = _cast_stacked(w_bgate)
    w_branch_b = _cast_stacked(w_branch)
    w_out_b = _cast_stacked(w_out)
    w_ffn_in_b = _cast_stacked(w_ffn_in, bm=128)
    w_ffn_out_b = _cast_stacked(w_ffn_out)
    w_ple_gate_b = _cast_stacked(w_ple_gate)
    w_ple_b = _cast_stacked(w_ple)
    w_ukv_b = _cast_stacked(mla_w_ukv)
    p_b = _cast_stacked(p).reshape(DEPTH, m, PLE_DIM)
    xr = x.reshape(m, d)
    for i in range(DEPTH):
        h = rmsnorm_rows(xr, norm_mix_g[i])
        z = matmul(h, w_in_p, F32, layer=i, name="in_proj")
        lam_init = 0.8 - 0.6 * math.exp(-0.3 * i)
        lq1, lk1, lq2, lk2 = (da_lambda[i, j].astype(F32) for j in range(4))
        lam = jnp.exp(jnp.sum(lq1 * lk1)) - jnp.exp(jnp.sum(lq2 * lk2)) + lam_init
        o_a = mixer_a(z, s, lam, lam_init, da_q_g[i], da_k_g[i], da_o_g[i], bias_a)
        o_b = mixer_b(z, s, mla_w_uq[i], w_ukv_b, mla_cq_g[i], mla_ckv_g[i], mla_qn_g[i], mla_kn_g[i],
                      mla_qr_g[i], mla_kr_g[i], cos_t, sin_t, layer=i)
        o_c = mixer_c(z, s, dil_q_g[i], dil_k_g[i], bias_c)
        o_d = mixer_d(z, s, gdn_conv_w[i], gdn_a_log[i], gdn_dt_bias[i], gdn_o_g[i])
        o_all = jnp.stack([o_a, o_b, o_c, o_d], axis=0)
        merged = gated_merge(h, w_bgate_b, b_bgate[i].reshape(N_BRANCH, 1, d), o_all, w_branch_b, i)
        xr = matmul_residual(xr, merged, w_out_b, bn=1024, layer=i, name="out_proj")
        hf = rmsnorm_rows(xr, norm_ffn_g[i])
        act = swiglu_in(hf, w_ffn_in_b, i)
        xr = matmul_residual(xr, act, w_ffn_out_b, bm=1024, bn=512, bk=D_FF // 2, layer=i, name="ffn_out")
        hp = rmsnorm_rows(xr, norm_ple_g[i])
        xr = ple_update(xr, p_b, w_ple_b, hp, w_ple_gate_b, i)
    return xr.reshape(b, s, d)
```

```python
import functools
import math

import numpy as np
import jax
import jax.numpy as jnp
from jax import lax
from jax.experimental import pallas as pl
from jax.experimental.pallas import tpu as pltpu

D_MODEL = 4096
DEPTH = 2
EPS = 1e-6
PLE_DIM = 256
HEADS = 8
N_BRANCH = 4
BRANCH_W = 1024
DA_QK_DIM = 64
MLA_Q_RANK = 1024
MLA_KV_RANK = 512
MLA_NOPE = 128
MLA_ROPE = 64
ROPE_THETA = 10000.0
DIL_GROUPS = ((128, 1), (512, 4), (2048, 16))
DIL_STEPS = 128
GDN_CHUNK = 64
CONV_W = 4
N_BUCKETS = 32
MAX_DIST = 2048
D_FF = 11008

DA_COLS = 3072
MLA_COLS = 1600
DIL_COLS = 9216
GDN_QKV = 3072

LANE = 128
Z_A = 0
Z_B = Z_A + DA_COLS
Z_C = Z_B + MLA_COLS + 64
Z_D = Z_C + DIL_COLS
Z_COLS = 18 * 1024

NEG = -1e30
ATT_T = 256
VMEM_LIMIT = 56 * 1024 * 1024
BF = jnp.bfloat16
F32 = jnp.float32


def _cparams(sem):
    return pltpu.CompilerParams(dimension_semantics=sem, vmem_limit_bytes=VMEM_LIMIT)


def _rmsnorm_kernel(x_ref, g_ref, o_ref):
    x = x_ref[...]
    ms = jnp.mean(x * x, axis=-1, keepdims=True)
    o_ref[...] = (x * lax.rsqrt(ms + EPS) * g_ref[...]).astype(o_ref.dtype)


def rmsnorm_rows(x, g, col_block=0, bm=512):
    m = x.shape[0]
    d = g.shape[-1]
    return pl.pallas_call(
        _rmsnorm_kernel,
        grid=(m // bm,),
        in_specs=[pl.BlockSpec((bm, d), lambda i: (i, col_block)),
                  pl.BlockSpec((1, d), lambda i: (0, 0))],
        out_specs=pl.BlockSpec((bm, d), lambda i: (i, 0)),
        out_shape=jax.ShapeDtypeStruct((m, d), BF),
        compiler_params=_cparams(("parallel",)),
        name="rmsnorm_rows",
    )(x, g.reshape(1, d))


def _mm_kernel(x_ref, w_ref, o_ref):
    o_ref[...] = jnp.dot(x_ref[...], w_ref[...], preferred_element_type=F32).astype(o_ref.dtype)


def _layer_spec(w, block, index_map, layer):
    if layer is None:
        return pl.BlockSpec(block, index_map)
    return pl.BlockSpec((None,) + block, lambda *g: (layer,) + index_map(*g))


def matmul(x, w, out_dtype, bm=1024, bn=1024, layer=None, name="matmul"):
    m, k = x.shape
    n = w.shape[-1]
    bn = min(bn, n)
    bm = min(bm, m)
    return pl.pallas_call(
        _mm_kernel,
        grid=(m // bm, n // bn),
        in_specs=[pl.BlockSpec((bm, k), lambda i, j: (i, 0)),
                  _layer_spec(w, (k, bn), lambda i, j: (0, j), layer)],
        out_specs=pl.BlockSpec((bm, bn), lambda i, j: (i, j)),
        out_shape=jax.ShapeDtypeStruct((m, n), out_dtype),
        compiler_params=_cparams(("parallel", "parallel")),
        name=name,
    )(x, w)


def _mm_residual_kernel(r_ref, x_ref, w_ref, o_ref, *acc, nk):
    part = jnp.dot(x_ref[...], w_ref[...], preferred_element_type=F32)
    if nk == 1:
        o_ref[...] = r_ref[...] + part
    else:
        acc_ref, = acc

        @pl.when(pl.program_id(2) == 0)
        def _():
            acc_ref[...] = r_ref[...]

        total = acc_ref[...] + part
        acc_ref[...] = total
        o_ref[...] = total


def matmul_residual(r, x, w, bm=1024, bn=512, bk=None, layer=None, name="matmul_residual"):
    m, kdim = x.shape
    n = w.shape[-1]
    bk = kdim if bk is None else bk
    bm = min(bm, m)
    nk = kdim // bk
    return pl.pallas_call(
        functools.partial(_mm_residual_kernel, nk=nk),
        grid=(m // bm, n // bn, nk),
        in_specs=[pl.BlockSpec((bm, bn), lambda i, j, k: (i, j)),
                  pl.BlockSpec((bm, bk), lambda i, j, k: (i, k)),
                  _layer_spec(w, (bk, bn), lambda i, j, k: (k, j), layer)],
        out_specs=pl.BlockSpec((bm, bn), lambda i, j, k: (i, j)),
        out_shape=jax.ShapeDtypeStruct((m, n), F32),
        scratch_shapes=[pltpu.VMEM((bm, bn), F32)] if nk > 1 else [],
        compiler_params=_cparams(("parallel", "parallel", "arbitrary")),
        name=name,
    )(r, x, w)


def _merge_kernel(h_ref, wg_ref, b_ref, oa_ref, ob_ref, oc_ref, od_ref, wb_ref, out_ref, acc_ref):
    n = pl.program_id(2)
    gate = jax.nn.sigmoid(jnp.dot(h_ref[...], wg_ref[...], preferred_element_type=F32) + b_ref[...])

    @pl.when(n == 0)
    def _():
        acc_ref[...] = jnp.zeros(acc_ref.shape, F32)

    for nn, o_ref in enumerate((oa_ref, ob_ref, oc_ref, od_ref)):
        @pl.when(n == nn)
        def _(o_ref=o_ref):
            total = acc_ref[...] + gate * jnp.dot(o_ref[...], wb_ref[...], preferred_element_type=F32)
            acc_ref[...] = total
            out_ref[...] = total.astype(out_ref.dtype)


def gated_merge(h, w_bgate, b_bgate, branches, w_branch, layer, bm=1024, bn=512):
    m, d = h.shape
    bm = min(bm, m)
    branch_spec = pl.BlockSpec((bm, BRANCH_W), lambda i, j, n: (i, 0))
    return pl.pallas_call(
        _merge_kernel,
        grid=(m // bm, d // bn, N_BRANCH),
        in_specs=[pl.BlockSpec((bm, d), lambda i, j, n: (i, 0)),
                  pl.BlockSpec((None, None, d, bn), lambda i, j, n: (layer, n, 0, j)),
                  pl.BlockSpec((None, 1, bn), lambda i, j, n: (n, 0, j)),
                  branch_spec, branch_spec, branch_spec, branch_spec,
                  pl.BlockSpec((None, None, BRANCH_W, bn), lambda i, j, n: (layer, n, 0, j))],
        out_specs=pl.BlockSpec((bm, bn), lambda i, j, n: (i, j)),
        out_shape=jax.ShapeDtypeStruct((m, d), BF),
        scratch_shapes=[pltpu.VMEM((bm, bn), F32)],
        compiler_params=_cparams(("parallel", "parallel", "arbitrary")),
        name="gated_merge",
    )(h, w_bgate, b_bgate, *branches, w_branch)


def _swiglu_kernel(x_ref, wg_ref, wu_ref, o_ref):
    x = x_ref[...]
    g = jnp.dot(x, wg_ref[...].astype(BF), preferred_element_type=F32)
    u = jnp.dot(x, wu_ref[...].astype(BF), preferred_element_type=F32)
    o_ref[...] = (g * jax.nn.sigmoid(g) * u).astype(o_ref.dtype)


def swiglu_in(x, w, layer, bm=2048, bn=256):
    m, k = x.shape
    n = w.shape[-1] // 2
    bm = min(bm, m)
    nb = n // bn
    return pl.pallas_call(
        _swiglu_kernel,
        grid=(m // bm, nb),
        in_specs=[pl.BlockSpec((bm, k), lambda i, j: (i, 0)),
                  pl.BlockSpec((None, k, bn), lambda i, j: (layer, 0, j)),
                  pl.BlockSpec((None, k, bn), lambda i, j: (layer, 0, nb + j))],
        out_specs=pl.BlockSpec((bm, bn), lambda i, j: (i, j)),
        out_shape=jax.ShapeDtypeStruct((m, n), BF),
        compiler_params=_cparams(("parallel", "parallel")),
        name="swiglu_in",
    )(x, w, w)


def _ple_kernel(r_ref, p_ref, wp_ref, h_ref, wg_ref, o_ref):
    e = jnp.dot(p_ref[...], wp_ref[...], preferred_element_type=F32)
    g = jnp.dot(h_ref[...], wg_ref[...].astype(BF), preferred_element_type=F32)
    o_ref[...] = r_ref[...] + e * jax.nn.sigmoid(g)


def ple_update(r, p, w_ple, hp, w_gate, layer, bm=1024, bn=512):
    m, d = r.shape
    kp = p.shape[-1]
    bm = min(bm, m)
    return pl.pallas_call(
        _ple_kernel,
        grid=(m // bm, d // bn),
        in_specs=[pl.BlockSpec((bm, bn), lambda i, j: (i, j)),
                  pl.BlockSpec((None, bm, kp), lambda i, j: (layer, i, 0)),
                  pl.BlockSpec((None, kp, bn), lambda i, j: (layer, 0, j)),
                  pl.BlockSpec((bm, d), lambda i, j: (i, 0)),
                  pl.BlockSpec((None, d, bn), lambda i, j: (layer, 0, j))],
        out_specs=pl.BlockSpec((bm, bn), lambda i, j: (i, j)),
        out_shape=jax.ShapeDtypeStruct((m, d), F32),
        compiler_params=_cparams(("parallel", "parallel")),
        name="ple_update",
    )(r, p, w_ple, hp, w_gate)


def _static_buckets(dist):
    max_exact = N_BUCKETS // 2
    d = np.maximum(np.asarray(dist), 0)
    large = max_exact + (np.log(np.maximum(d, 1).astype(np.float32) / np.float32(max_exact))
                         / np.float32(math.log(MAX_DIST / max_exact))
                         * np.float32(N_BUCKETS - max_exact)).astype(np.int32)
    large = np.minimum(large, N_BUCKETS - 1)
    return np.where(d < max_exact, d, large).astype(np.int32)


def _toeplitz_tiles(f, t):
    hh, n = f.shape
    big = n + t - 1
    fpad = jnp.concatenate([jnp.full((hh, t - 1), NEG, f.dtype), f], axis=1)
    flat = jnp.tile(fpad, (1, t + 1))[:, :t * (big + 1)]
    w = flat.reshape(hh, t, big + 1)[:, :, :n]
    w = jnp.flip(w.reshape(hh, t, n // t, t), axis=3)
    return w.transpose(0, 2, 1, 3)


def _bias_tile_kernel(prev_ref, cur_ref, o_ref, *, t):
    row = jnp.concatenate([prev_ref[...], cur_ref[...]], axis=1)
    rolled = pltpu.roll(jnp.broadcast_to(row, (t, 2 * t)), 0, 1, stride=1, stride_axis=0)
    o_ref[...] = rolled[:, t:]


def diff_bias_tiles(rel_bias, s):
    t = min(ATT_T, s)
    nq = s // t
    f = jnp.take(rel_bias[:, :HEADS], _static_buckets(np.arange(s)), axis=0).T.astype(F32)
    fb = jnp.concatenate([jnp.full((HEADS, t), NEG, F32), f], axis=1).reshape(HEADS, nq + 1, 1, t)
    return pl.pallas_call(
        functools.partial(_bias_tile_kernel, t=t),
        grid=(HEADS, nq),
        in_specs=[pl.BlockSpec((None, None, 1, t), lambda h, d: (h, d, 0, 0)),
                  pl.BlockSpec((None, None, 1, t), lambda h, d: (h, d + 1, 0, 0))],
        out_specs=pl.BlockSpec((None, None, t, t), lambda h, d: (h, d, 0, 0)),
        out_shape=jax.ShapeDtypeStruct((HEADS, nq, t, t), F32),
        compiler_params=_cparams(("parallel", "parallel")),
        name="diff_bias_tiles",
    )(fb, fb)


def dil_bias_tiles(rel_bias):
    out = []
    for gi, (_, dil) in enumerate(DIL_GROUPS):
        lo = HEADS + gi * HEADS
        steps = np.arange(2 * DIL_STEPS)
        f = jnp.take(rel_bias[:, lo:lo + HEADS], _static_buckets(steps * dil), axis=0).T.astype(F32)
        f = jnp.where(steps[None, :] <= DIL_STEPS, f, NEG)
        tiles = _toeplitz_tiles(f, DIL_STEPS)
        general = jnp.concatenate([tiles[:, 1], tiles[:, 0]], axis=-1)
        first = jnp.concatenate([jnp.full_like(tiles[:, 1], NEG), tiles[:, 0]], axis=-1)
        out.append(jnp.stack([general, first], axis=1))
    return jnp.stack(out, axis=0)


def rope_tables(s):
    half = MLA_ROPE // 2
    inv = ROPE_THETA ** (-jnp.arange(half, dtype=F32) / half)
    ang = jnp.arange(s).astype(F32)[:, None] * inv[None, :]
    cos, sin = jnp.cos(ang), jnp.sin(ang)
    zero = jnp.zeros((s, LANE - MLA_ROPE), F32)
    return (jnp.concatenate([cos, cos, zero], axis=1), jnp.concatenate([-sin, sin, zero], axis=1))


def _rope_tile(t, c, s):
    half = MLA_ROPE // 2
    swapped = pltpu.roll(t, half, axis=1) + pltpu.roll(t, LANE - half, axis=1)
    return t * c + swapped * s


def _flash_kernel(lam_ref, qt_ref, k_ref, vt_ref, *rest, diff, t, scale, post_scale):
    if diff:
        bias_ref, g_ref, o_ref, m_sc, l_sc, acc_sc = rest
    else:
        o_ref, m_sc, l_sc, acc_sc = rest
    i = pl.program_id(2)
    m_sc[...] = jnp.full(m_sc.shape, NEG, F32)
    l_sc[...] = jnp.zeros(l_sc.shape, F32)
    acc_sc[...] = jnp.zeros(acc_sc.shape, F32)
    qts = [qt_ref[0], qt_ref[1]] if diff else [qt_ref[:, 0:t], qt_ref[:, t:2 * t]]

    def load_kv(j):
        start = pl.multiple_of(j * t, t)
        return k_ref[pl.ds(start, t), :], vt_ref[:, pl.ds(start, t)]

    def step(work):
        scores = [jnp.dot(kv[0], qts[c], preferred_element_type=F32) for c, kv, _, _ in work]
        for (c, kv, bias, diagonal), s in zip(work, scores):
            if scale != 1.0:
                s = s * scale
            if bias is not None:
                s = bias + s
            elif diagonal:
                key_i = lax.broadcasted_iota(jnp.int32, (t, t), 0)
                qry_i = lax.broadcasted_iota(jnp.int32, (t, t), 1)
                s = jnp.where(key_i <= qry_i, s, NEG)
            m_prev = m_sc[c]
            m_new = jnp.maximum(m_prev, jnp.max(s, axis=0, keepdims=True))
            alpha = jnp.exp(m_prev - m_new)
            p = jnp.exp(s - m_new)
            l_sc[c] = alpha * l_sc[c] + jnp.sum(p, axis=0, keepdims=True)
            m_sc[c] = m_new
            acc_sc[c] = alpha * acc_sc[c] + jnp.dot(kv[1], p.astype(BF), preferred_element_type=F32)

    def sweep(n, work):
        def body(jj, carry):
            step(work(4 * jj) + work(4 * jj + 1) + work(4 * jj + 2) + work(4 * jj + 3))
            return carry

        lax.fori_loop(0, n // 4, body, 0)
        done = (n // 4) * 4

        @pl.when(n % 4 >= 2)
        def _():
            step(work(done) + work(done + 1))

        @pl.when(n % 2 == 1)
        def _():
            step(work(n - 1))

    if diff:
        def work(j):
            kv = load_kv(j)
            bias = bias_ref[i - j]
            return [(0, kv, bias, False), (1, kv, bias, False)]

        sweep(i + 1, work)
        out_t = acc_sc[0] / l_sc[0] - lam_ref[0] * (acc_sc[1] / l_sc[1])
        o = out_t.T
        ms = jnp.mean(o * o, axis=-1, keepdims=True)
        o_ref[...] = (o * lax.rsqrt(ms + EPS) * g_ref[...] * post_scale).astype(o_ref.dtype)
    else:
        def work(j):
            kv = load_kv(j)
            return [(0, kv, None, False), (1, kv, None, False)]

        sweep(2 * i, work)
        kv = load_kv(2 * i)
        step([(0, kv, None, True), (1, kv, None, False)])
        step([(1, load_kv(2 * i + 1), None, True)])
        o_ref[0:t, :] = (acc_sc[0] / l_sc[0]).T.astype(o_ref.dtype)
        o_ref[t:2 * t, :] = (acc_sc[1] / l_sc[1]).T.astype(o_ref.dtype)


def flash_attention(q, k, vt, s, *, diff, dk, scale, bias=None, lam=None, gain=None, post_scale=1.0, name):
    m = k.shape[0]
    b = m // s
    t = min(ATT_T, s)
    lam = jnp.zeros((1,), F32) if lam is None else lam.reshape(1).astype(F32)
    if diff:
        nq = s // t
        q_spec = pl.BlockSpec((2, dk, t), lambda bb, h, i: (0, h, bb * nq + i))
        o_rows = t
    else:
        nq = s // (2 * t)
        q_spec = pl.BlockSpec((dk, 2 * t), lambda bb, h, i: (h, bb * nq + i))
        o_rows = 2 * t
    in_specs = [pl.BlockSpec(memory_space=pltpu.SMEM), q_spec,
                pl.BlockSpec((s, dk), lambda bb, h, i: (bb, h)),
                pl.BlockSpec((LANE, s), lambda bb, h, i: (h, bb))]
    args = [lam, q, k, vt]
    if diff:
        in_specs += [pl.BlockSpec((None, nq, t, t), lambda bb, h, i: (h, 0, 0, 0)),
                     pl.BlockSpec((1, LANE), lambda bb, h, i: (0, 0))]
        args += [bias, gain.reshape(1, LANE).astype(F32)]
    return pl.pallas_call(
        functools.partial(_flash_kernel, diff=diff, t=t, scale=scale, post_scale=post_scale),
        grid=(b, HEADS, nq),
        in_specs=in_specs,
        out_specs=pl.BlockSpec((o_rows, LANE), lambda bb, h, i: (bb * nq + i, h)),
        out_shape=jax.ShapeDtypeStruct((m, HEADS * LANE), BF),
        scratch_shapes=[pltpu.VMEM((2, 1, t), F32), pltpu.VMEM((2, 1, t), F32), pltpu.VMEM((2, LANE, t), F32)],
        compiler_params=_cparams(("parallel", "parallel", "arbitrary")),
        name=name,
    )(*args)


def _prep_a_kernel(z_ref, qg_ref, kg_ref, qz_ref, kn_ref, vt_ref):
    lane = lax.broadcasted_iota(jnp.int32, (1, LANE), 1)
    lo = lane < DA_QK_DIM

    def norm_halves(x, g):
        x2 = x * x
        s_lo = jnp.sum(jnp.where(lo, x2, 0.0), axis=-1, keepdims=True)
        s_hi = jnp.sum(jnp.where(lo, 0.0, x2), axis=-1, keepdims=True)
        inv = lax.rsqrt(jnp.where(lo, s_lo, s_hi) * (1.0 / DA_QK_DIM) + EPS)
        return x * inv * g

    scale = DA_QK_DIM ** -0.5
    for h in range(HEADS):
        cq = slice(h * LANE, (h + 1) * LANE)
        ck = slice(HEADS * LANE + h * LANE, HEADS * LANE + (h + 1) * LANE)
        q = norm_halves(z_ref[:, cq], qg_ref[...]) * scale
        qt = q.T
        sub_lo = lax.broadcasted_iota(jnp.int32, (LANE, 1), 0) < DA_QK_DIM
        qz_ref[0, cq, :] = jnp.where(sub_lo, qt, 0.0).astype(BF)
        qz_ref[1, cq, :] = jnp.where(sub_lo, 0.0, qt).astype(BF)
        kn_ref[:, cq] = norm_halves(z_ref[:, ck], kg_ref[...]).astype(BF)
        cv = slice(2 * HEADS * LANE + h * LANE, 2 * HEADS * LANE + (h + 1) * LANE)
        vt_ref[cq, :] = z_ref[:, cv].T.astype(BF)


def prep_a(z, q_g, k_g, bm=256):
    m = z.shape[0]
    bm = min(bm, m)
    w = HEADS * LANE
    g2 = lambda g: jnp.concatenate([g, g]).reshape(1, LANE).astype(F32)
    return pl.pallas_call(
        _prep_a_kernel,
        grid=(m // bm,),
        in_specs=[pl.BlockSpec((bm, DA_COLS), lambda i: (i, Z_A // DA_COLS)),
                  pl.BlockSpec((1, LANE), lambda i: (0, 0)),
                  pl.BlockSpec((1, LANE), lambda i: (0, 0))],
        out_specs=[pl.BlockSpec((2, w, bm), lambda i: (0, 0, i)),
                   pl.BlockSpec((bm, w), lambda i: (i, 0)),
                   pl.BlockSpec((w, bm), lambda i: (0, i))],
        out_shape=[jax.ShapeDtypeStruct((2, w, m), BF), jax.ShapeDtypeStruct((m, w), BF),
                   jax.ShapeDtypeStruct((w, m), BF)],
        compiler_params=_cparams(("parallel",)),
        name="prep_a",
    )(z, g2(q_g), g2(k_g))


def mixer_a(z, s, lam, lam_init, q_g, k_g, o_g, bias_tiles):
    qz, kn, vt = prep_a(z, q_g, k_g)
    return flash_attention(qz, kn, vt, s, diff=True, dk=LANE, scale=1.0, bias=bias_tiles, lam=lam, gain=o_g,
                           post_scale=1.0 - lam_init, name="diff_attention")


def _prep_b_kernel(zq_ref, zkv_ref, zkr_ref, wq_ref, wkv_ref, cq_g, ckv_g, kr_g, qn_g, qr_g, kn_g, c_ref, s_ref,
                   qc_ref, kc_ref, vt_ref):
    def norm(x, g, width):
        ms = jnp.sum(x * x, axis=-1, keepdims=True) * (1.0 / width)
        return x * lax.rsqrt(ms + EPS) * g

    cos, sin = c_ref[...], s_ref[...]
    c_q = norm(zq_ref[...], cq_g[...], MLA_Q_RANK).astype(BF)
    c_kv = norm(zkv_ref[...], ckv_g[...], MLA_KV_RANK).astype(BF)
    q_up = jnp.dot(c_q, wq_ref[...], preferred_element_type=F32)
    kv_up = jnp.dot(c_kv, wkv_ref[...], preferred_element_type=F32)
    kr = _rope_tile(norm(zkr_ref[...], kr_g[...], MLA_ROPE), cos, sin).astype(BF)
    for h in range(HEADS):
        c0 = slice(2 * h * LANE, (2 * h + 1) * LANE)
        c1 = slice((2 * h + 1) * LANE, (2 * h + 2) * LANE)
        qc_ref[c0, :] = norm(q_up[:, c0], qn_g[...], MLA_NOPE).T.astype(BF)
        qc_ref[c1, :] = _rope_tile(norm(q_up[:, c1], qr_g[...], MLA_ROPE), cos, sin).T.astype(BF)
        kc_ref[:, c0] = norm(kv_up[:, c0], kn_g[...], MLA_NOPE).astype(BF)
        kc_ref[:, c1] = kr
        vt_ref[h * LANE:(h + 1) * LANE, :] = kv_up[:, c1].T.astype(BF)


def prep_b(z, w_uq, w_ukv, layer, cq_g, ckv_g, kr_g, qn_g, qr_g, kn_g, cos_t, sin_t, s, bm=256):
    m = z.shape[0]
    bm = min(bm, s)
    nsb = s // bm
    w2 = 2 * HEADS * LANE
    row = lambda g: g.reshape(1, -1).astype(F32)
    pad = lambda g: jnp.concatenate([g, jnp.zeros((LANE - MLA_ROPE,), g.dtype)])
    const = lambda i: (0, 0)
    return pl.pallas_call(
        _prep_b_kernel,
        grid=(m // bm,),
        in_specs=[pl.BlockSpec((bm, MLA_Q_RANK), lambda i: (i, Z_B // MLA_Q_RANK)),
                  pl.BlockSpec((bm, MLA_KV_RANK), lambda i: (i, (Z_B + MLA_Q_RANK) // MLA_KV_RANK)),
                  pl.BlockSpec((bm, LANE), lambda i: (i, (Z_B + MLA_Q_RANK + MLA_KV_RANK) // LANE)),
                  pl.BlockSpec((MLA_Q_RANK, w2), const),
                  pl.BlockSpec((None, MLA_KV_RANK, w2), lambda i: (layer, 0, 0)),
                  pl.BlockSpec((1, MLA_Q_RANK), const),
                  pl.BlockSpec((1, MLA_KV_RANK), const),
                  pl.BlockSpec((1, LANE), const),
                  pl.BlockSpec((1, LANE), const),
                  pl.BlockSpec((1, LANE), const),
                  pl.BlockSpec((1, LANE), const),
                  pl.BlockSpec((bm, LANE), lambda i: (i % nsb, 0)),
                  pl.BlockSpec((bm, LANE), lambda i: (i % nsb, 0))],
        out_specs=[pl.BlockSpec((w2, bm), lambda i: (0, i)),
                   pl.BlockSpec((bm, w2), lambda i: (i, 0)),
                   pl.BlockSpec((HEADS * LANE, bm), lambda i: (0, i))],
        out_shape=[jax.ShapeDtypeStruct((w2, m), BF), jax.ShapeDtypeStruct((m, w2), BF),
                   jax.ShapeDtypeStruct((HEADS * LANE, m), BF)],
        compiler_params=_cparams(("parallel",)),
        name="prep_b",
    )(z, z, z, w_uq, w_ukv, row(cq_g), row(ckv_g), row(pad(kr_g)), row(qn_g), row(pad(qr_g)), row(kn_g),
      cos_t, sin_t)


def pack_w_uq(w):
    w = w.reshape(MLA_Q_RANK, HEADS, MLA_NOPE + MLA_ROPE)
    w = jnp.pad(w, ((0, 0), (0, 0), (0, 2 * LANE - MLA_NOPE - MLA_ROPE)))
    return w.reshape(MLA_Q_RANK, HEADS * 2 * LANE).astype(BF)


def mixer_b(z, s, w_uq, w_ukv, cq_g, ckv_g, qn_g, kn_g, qr_g, kr_g, cos_t, sin_t, layer=0):
    qc, kc, vt = prep_b(z, pack_w_uq(w_uq), w_ukv, layer, cq_g, ckv_g, kr_g, qn_g, qr_g, kn_g, cos_t, sin_t, s)
    return flash_attention(qc, kc, vt, s, diff=False, dk=2 * LANE, scale=(MLA_NOPE + MLA_ROPE) ** -0.5,
                           name="mla_attention")


DIL_PAD = DIL_STEPS * max(d for _, d in DIL_GROUPS)


def _dil_kernel(zq_ref, zk_ref, zv_ref, bias_ref, qg_ref, kg_ref, o_ref, q_sc, k_sc, v_sc, og_sc, lse_sc, *, s):
    g = pl.program_id(2)
    ngroups = len(DIL_GROUPS)

    def norm(x, gain):
        ms = jnp.mean(x * x, axis=-1, keepdims=True)
        return x * lax.rsqrt(ms + EPS) * gain

    zeros = jnp.zeros((DIL_PAD, LANE), F32)
    k_sc[0:DIL_PAD, :] = zeros
    v_sc[0:DIL_PAD, :] = zeros
    q_sc[...] = norm(zq_ref[...], qg_ref[...])
    k_sc[DIL_PAD:, :] = norm(zk_ref[...], kg_ref[...])
    v_sc[DIL_PAD:, :] = zv_ref[...]
    scale = LANE ** -0.5

    def group(gi, dil):
        nsub = s // DIL_STEPS

        def body(tt, carry):
            c = tt % dil
            n = tt // dil
            q0 = c + dil * DIL_STEPS * n
            rows = pl.ds(q0, DIL_STEPS, stride=dil) if dil > 1 else pl.ds(q0, DIL_STEPS)
            k0 = q0 + DIL_PAD - dil * DIL_STEPS
            band = pl.ds(k0, 2 * DIL_STEPS, stride=dil) if dil > 1 else pl.ds(k0, 2 * DIL_STEPS)
            qs = q_sc[rows, :].astype(BF)
            ks = k_sc[band, :].astype(BF)
            vs = v_sc[band, :].astype(BF)
            first = jnp.where(n == 0, 1, 0)
            logits = lax.dot_general(qs, ks, (((1,), (1,)), ((), ())), preferred_element_type=F32) * scale
            logits = logits + bias_ref[first]
            mx = jnp.max(logits, axis=-1, keepdims=True)
            e = jnp.exp(logits - mx)
            den = jnp.sum(e, axis=-1, keepdims=True)
            o = jnp.dot((e / den).astype(BF), vs, preferred_element_type=F32)
            og_sc[gi, rows, :] = o
            lse_sc[gi, rows, :] = jnp.broadcast_to(mx + jnp.log(den), (DIL_STEPS, LANE))
            return carry

        lax.fori_loop(0, nsub, body, 0, unroll=8)

    for gi, (_, dil) in enumerate(DIL_GROUPS):
        pl.when(g == gi)(functools.partial(group, gi, dil))

    @pl.when(g == ngroups - 1)
    def _():
        lses = [lse_sc[gi] for gi in range(ngroups)]
        mx = functools.reduce(jnp.maximum, lses)
        ws = [jnp.exp(l - mx) for l in lses]
        tot = functools.reduce(lambda a, b2: a + b2, ws)
        acc = ws[0] * og_sc[0]
        for gi in range(1, ngroups):
            acc = acc + ws[gi] * og_sc[gi]
        o_ref[...] = (acc / tot).astype(o_ref.dtype)


def mixer_c(z, s, q_g, k_g, bias_c):
    m = z.shape[0]
    b = m // s
    ngroups = len(DIL_GROUPS)
    cb = Z_C // LANE

    def col(which):
        return lambda bb, h, g: (bb, cb + (g * 3 + which) * HEADS + h)

    row = lambda g: g.reshape(1, LANE).astype(F32)
    return pl.pallas_call(
        functools.partial(_dil_kernel, s=s),
        grid=(b, HEADS, ngroups),
        in_specs=[pl.BlockSpec((s, LANE), col(0)),
                  pl.BlockSpec((s, LANE), col(1)),
                  pl.BlockSpec((s, LANE), col(2)),
                  pl.BlockSpec((None, None, 2, DIL_STEPS, 2 * DIL_STEPS), lambda bb, h, g: (g, h, 0, 0, 0)),
                  pl.BlockSpec((1, LANE), lambda bb, h, g: (0, 0)),
                  pl.BlockSpec((1, LANE), lambda bb, h, g: (0, 0))],
        out_specs=pl.BlockSpec((s, LANE), lambda bb, h, g: (bb, h)),
        out_shape=jax.ShapeDtypeStruct((m, HEADS * LANE), BF),
        scratch_shapes=[pltpu.VMEM((s, LANE), F32), pltpu.VMEM((DIL_PAD + s, LANE), F32),
                        pltpu.VMEM((DIL_PAD + s, LANE), F32), pltpu.VMEM((ngroups, s, LANE), F32),
                        pltpu.VMEM((ngroups, s, LANE), F32)],
        compiler_params=_cparams(("parallel", "parallel", "arbitrary")),
        name="dilated_attention",
    )(z, z, z, bias_c, row(q_g), row(k_g))


CONV_PAD = 8
GDN_GROUP = 4
GDN_BASE = 8


def _gdn_kernel(par_ref, zq_ref, zk_ref, zv_ref, zg_ref, zab_ref, wq_ref, wk_ref, wv_ref, og_ref, o_ref,
                x_sc, q_sc, k_sc, v_sc, g_sc, b_sc, u_sc, w_sc, a_sc, st_sc, *, s):
    h = pl.program_id(1)
    c = GDN_CHUNK

    def conv_silu(z_ref, w_ref):
        x_sc[0:CONV_PAD, :] = jnp.zeros((CONV_PAD, LANE), F32)
        x_sc[CONV_PAD:, :] = z_ref[...]
        y = x_sc[CONV_PAD:, :] * w_ref[CONV_W - 1:CONV_W, :]
        for i in range(CONV_W - 1):
            off = CONV_PAD - (CONV_W - 1) + i
            y = y + x_sc[off:off + s, :] * w_ref[i:i + 1, :]
        return y * jax.nn.sigmoid(y)

    def l2(x):
        return x * lax.rsqrt(jnp.sum(x * x, axis=-1, keepdims=True) + EPS)

    q_sc[...] = l2(conv_silu(zq_ref, wq_ref)) * (LANE ** -0.5)
    k_sc[...] = l2(conv_silu(zk_ref, wk_ref))
    v_sc[...] = conv_silu(zv_ref, wv_ref)
    lane = lax.broadcasted_iota(jnp.int32, (1, LANE), 1)
    ab = zab_ref[...]
    a_col = jnp.sum(jnp.where(lane == h, ab, 0.0), axis=-1, keepdims=True)
    b_col = jnp.sum(jnp.where(lane == h + HEADS, ab, 0.0), axis=-1, keepdims=True)
    a_neg_exp = par_ref[0, h]
    dt_bias = par_ref[1, h]
    g_sc[...] = jnp.broadcast_to(a_neg_exp * jax.nn.softplus(a_col + dt_bias), (s, LANE))
    b_sc[...] = jnp.broadcast_to(jax.nn.sigmoid(b_col), (s, LANE))
    st_sc[...] = jnp.zeros(st_sc.shape, F32)

    gr = GDN_GROUP * c
    r_i = lax.broadcasted_iota(jnp.int32, (gr, gr), 0)
    c_i = lax.broadcasted_iota(jnp.int32, (gr, gr), 1)
    blk = {}
    size = GDN_BASE
    while size <= c:
        blk[size] = (r_i // size) == (c_i // size)
        size *= 2
    same = blk[c]
    tril = same & (r_i >= c_i)
    strict = same & (r_i > c_i)
    tril_b = tril.astype(BF)
    triu_b = (same & (r_i <= c_i)).astype(BF)
    eye = (r_i == c_i).astype(F32)

    def mm(a, b2):
        return jnp.dot(a.astype(BF), b2.astype(BF), preferred_element_type=F32)

    def mm_t(a, b2):
        return lax.dot_general(a.astype(BF), b2.astype(BF), (((1,), (1,)), ((), ())), preferred_element_type=F32)

    def split2(x):
        hi = x.astype(BF)
        return hi, (x - hi.astype(F32)).astype(BF)

    def split3(x):
        hi = x.astype(BF)
        r1 = x - hi.astype(F32)
        mid = r1.astype(BF)
        return hi, mid, (r1 - mid.astype(F32)).astype(BF)

    def mm_hi(a, b2):
        a_hi, a_lo = split2(a)
        b_hi, b_lo = split2(b2)
        d = functools.partial(jnp.dot, preferred_element_type=F32)
        return d(a_hi, b_hi) + (d(a_hi, b_lo) + d(a_lo, b_hi))

    def widen(x):
        return jnp.concatenate([x] * (gr // LANE), axis=1)

    def local_group(gi, carry):
        base = pl.multiple_of(gi * gr, gr)
        rows = pl.ds(base, gr)
        q = q_sc[rows, :]
        k = k_sc[rows, :]
        beta = b_sc[rows, :]
        parts = split3(g_sc[rows, :])
        gc = sum(jnp.dot(tril_b, part, preferred_element_type=F32) for part in parts)
        g_row = sum(lax.dot_general(widen(part), triu_b, (((0,), (0,)), ((), ())), preferred_element_type=F32)
                    for part in parts)
        decay = jnp.exp(jnp.where(tril, widen(gc) - g_row, NEG))
        kb = k * beta
        vb = v_sc[rows, :] * beta
        lower = jnp.where(strict, mm_t(kb, k) * decay, 0.0)
        neg = -jnp.where(blk[GDN_BASE], lower, 0.0)
        p1 = mm_hi(neg, neg)
        tmat = eye + neg
        tmat = tmat + mm_hi(tmat, p1)
        tmat = tmat + mm_hi(tmat, mm_hi(p1, p1))
        size = GDN_BASE
        while size < c:
            off = jnp.where(blk[2 * size] & jnp.logical_not(blk[size]), lower, 0.0)
            tmat = tmat - mm_hi(mm_hi(tmat, off), tmat)
            size *= 2
        eg = jnp.exp(gc)
        u_sc[rows, :] = mm(tmat, vb)
        w_sc[rows, :] = mm(tmat, kb * eg)
        intra = mm_t(q, k) * decay
        q_sc[rows, :] = q * eg
        for uu in range(GDN_GROUP):
            sl = slice(uu * c, (uu + 1) * c)
            crow = pl.ds(base + uu * c, c)
            g_last = gc[(uu + 1) * c - 1:(uu + 1) * c, :]
            a_sc[crow, 0:c] = intra[sl, sl]
            k_sc[crow, :] = k[sl] * jnp.exp(g_last - gc[sl])
            g_sc[crow, :] = jnp.broadcast_to(jnp.exp(g_last), (c, LANE))
        return carry


    def scan(n, carry):
        rows = pl.ds(pl.multiple_of(n * c, c), c)
        state = st_sc[...]
        v_new = u_sc[rows, :] - mm(w_sc[rows, :], state)
        o = mm(q_sc[rows, :], state) + mm(a_sc[rows, 0:c], v_new)
        decay_last = g_sc[pl.ds(pl.multiple_of(n * c, c), 1), :]
        st_sc[...] = state * decay_last + lax.dot_general(
            k_sc[rows, :].astype(BF), v_new.astype(BF), (((0,), (0,)), ((), ())), preferred_element_type=F32)
        u_sc[rows, :] = o
        return carry

    ngroups = s // gr
    local_group(0, 0)

    def pipelined(g, carry):
        local_group(g + 1, 0)
        for uu in range(GDN_GROUP):
            scan(g * GDN_GROUP + uu, 0)
        return carry

    lax.fori_loop(0, ngroups - 1, pipelined, 0)
    for uu in range(GDN_GROUP):
        scan((ngroups - 1) * GDN_GROUP + uu, 0)
    o = u_sc[...]
    ms = jnp.mean(o * o, axis=-1, keepdims=True)
    gate = zg_ref[...]
    o_ref[...] = (o * lax.rsqrt(ms + EPS) * og_ref[...] * (gate * jax.nn.sigmoid(gate))).astype(o_ref.dtype)


def mixer_d(z, s, conv_w, a_log, dt_bias, o_g):
    m = z.shape[0]
    b = m // s
    cb = Z_D // LANE
    par = jnp.stack([-jnp.exp(a_log.astype(F32)), dt_bias.astype(F32)], axis=0)
    cw = conv_w.astype(F32)

    def col(which):
        return lambda bb, h: (bb, cb + which * HEADS + h)

    return pl.pallas_call(
        functools.partial(_gdn_kernel, s=s),
        grid=(b, HEADS),
        in_specs=[pl.BlockSpec(memory_space=pltpu.SMEM),
                  pl.BlockSpec((s, LANE), col(0)),
                  pl.BlockSpec((s, LANE), col(1)),
                  pl.BlockSpec((s, LANE), col(2)),
                  pl.BlockSpec((s, LANE), col(3)),
                  pl.BlockSpec((s, LANE), lambda bb, h: (bb, cb + 4 * HEADS)),
                  pl.BlockSpec((CONV_W, LANE), lambda bb, h: (0, h)),
                  pl.BlockSpec((CONV_W, LANE), lambda bb, h: (0, HEADS + h)),
                  pl.BlockSpec((CONV_W, LANE), lambda bb, h: (0, 2 * HEADS + h)),
                  pl.BlockSpec((1, LANE), lambda bb, h: (0, 0))],
        out_specs=pl.BlockSpec((s, LANE), lambda bb, h: (bb, h)),
        out_shape=jax.ShapeDtypeStruct((m, HEADS * LANE), BF),
        scratch_shapes=[pltpu.VMEM((CONV_PAD + s, LANE), F32), pltpu.VMEM((s, LANE), F32),
                        pltpu.VMEM((s, LANE), F32), pltpu.VMEM((s, LANE), F32), pltpu.VMEM((s, LANE), F32),
                        pltpu.VMEM((s, LANE), F32), pltpu.VMEM((s, LANE), F32), pltpu.VMEM((s, LANE), F32),
                        pltpu.VMEM((s, LANE), F32), pltpu.VMEM((LANE, LANE), F32)],
        compiler_params=_cparams(("parallel", "arbitrary")),
        name="gated_deltanet",
    )(par, z, z, z, z, z, cw, cw, cw, o_g.reshape(1, LANE).astype(F32))


def _cast_kernel(x_ref, o_ref):
    o_ref[...] = x_ref[...].astype(o_ref.dtype)


def cast_bf16(w, bm=512):
    r, c = w.shape
    bm = min(bm, r)
    return pl.pallas_call(
        _cast_kernel,
        grid=(r // bm,),
        in_specs=[pl.BlockSpec((bm, c), lambda i: (i, 0))],
        out_specs=pl.BlockSpec((bm, c), lambda i: (i, 0)),
        out_shape=jax.ShapeDtypeStruct((r, c), BF),
        compiler_params=_cparams(("parallel",)),
        name="cast_bf16",
    )(w)


IN_COLS = DA_COLS + MLA_COLS + DIL_COLS + GDN_QKV + 2 * HEADS + HEADS * LANE


def _pack_w_in_kernel(w_ref, o_ref):
    rows = w_ref.shape[0]

    def put(dst, src, width):
        o_ref[:, dst:dst + width] = w_ref[:, src:src + width]

    def zero(dst, width):
        o_ref[:, dst:dst + width] = jnp.zeros((rows, width), BF)

    c0 = DA_COLS + MLA_COLS + DIL_COLS
    put(0, 0, DA_COLS + MLA_COLS)
    zero(Z_B + MLA_COLS, Z_C - Z_B - MLA_COLS)
    put(Z_C, DA_COLS + MLA_COLS, DIL_COLS)
    put(Z_D, c0, GDN_QKV)
    put(Z_D + GDN_QKV, c0 + GDN_QKV + 2 * HEADS, HEADS * LANE)
    put(Z_D + GDN_QKV + HEADS * LANE, c0 + GDN_QKV, 2 * HEADS)
    used = Z_D + GDN_QKV + HEADS * LANE + 2 * HEADS
    zero(used, Z_COLS - used)


def pack_w_in(w, bm=256):
    r = w.shape[0]
    return pl.pallas_call(
        _pack_w_in_kernel,
        grid=(r // bm,),
        in_specs=[pl.BlockSpec((bm, IN_COLS), lambda i: (i, 0))],
        out_specs=pl.BlockSpec((bm, Z_COLS), lambda i: (i, 0)),
        out_shape=jax.ShapeDtypeStruct((r, Z_COLS), BF),
        compiler_params=_cparams(("parallel",)),
        name="pack_w_in",
    )(w)


def _cast_stacked(w, bm=512):
    lead, c = w.shape[:-1], w.shape[-1]
    return cast_bf16(w.reshape(-1, c), bm).reshape(*lead, c)


def kernel(x, p, rel_bias, norm_mix_g, w_in, da_lambda, da_q_g, da_k_g, da_o_g, mla_w_uq, mla_w_ukv, mla_cq_g, mla_ckv_g, mla_qn_g, mla_kn_g, mla_qr_g, mla_kr_g, dil_q_g, dil_k_g, gdn_conv_w, gdn_a_log, gdn_dt_bias, gdn_o_g, w_bgate, b_bgate, w_branch, w_out, norm_ffn_g, w_ffn_in, w_ffn_out, norm_ple_g, w_ple, w_ple_gate):
    b, s, d = x.shape
    m = b * s
    bias_a = diff_bias_tiles(rel_bias, s)
    bias_c = dil_bias_tiles(rel_bias)
    cos_t, sin_t = rope_tables(s)
    w_in_p = pack_w_in(w_in.astype(BF).reshape(DEPTH * d, IN_COLS)).reshape(DEPTH, d, Z_COLS)
    w_bgate_b = _cast_stacked(w_bgate)
    w_branch_b = _cast_stacked(w_branch)
    w_out_b = _cast_stacked(w_out)
    w_ffn_out_b = _cast_stacked(w_ffn_out)
    w_ple_b = _cast_stacked(w_ple)
    w_ukv_b = _cast_stacked(mla_w_ukv)
    p_b = _cast_stacked(p).reshape(DEPTH, m, PLE_DIM)
    xr = x.reshape(m, d)
    for i in range(DEPTH):
        h = rmsnorm_rows(xr, norm_mix_g[i])
        z = matmul(h, w_in_p, F32, layer=i, name="in_proj")
        lam_init = 0.8 - 0.6 * math.exp(-0.3 * i)
        lq1, lk1, lq2, lk2 = (da_lambda[i, j].astype(F32) for j in range(4))
        lam = jnp.exp(jnp.sum(lq1 * lk1)) - jnp.exp(jnp.sum(lq2 * lk2)) + lam_init
        o_a = mixer_a(z, s, lam, lam_init, da_q_g[i], da_k_g[i], da_o_g[i], bias_a)
        o_b = mixer_b(z, s, mla_w_uq[i], w_ukv_b, mla_cq_g[i], mla_ckv_g[i], mla_qn_g[i], mla_kn_g[i],
                      mla_qr_g[i], mla_kr_g[i], cos_t, sin_t, layer=i)
        o_c = mixer_c(z, s, dil_q_g[i], dil_k_g[i], bias_c)
        o_d = mixer_d(z, s, gdn_conv_w[i], gdn_a_log[i], gdn_dt_bias[i], gdn_o_g[i])
        merged = gated_merge(h, w_bgate_b, b_bgate[i].reshape(N_BRANCH, 1, d), (o_a, o_b, o_c, o_d), w_branch_b, i)
        xr = matmul_residual(xr, merged, w_out_b, bn=1024, layer=i, name="out_proj")
        hf = rmsnorm_rows(xr, norm_ffn_g[i])
        act = swiglu_in(hf, w_ffn_in, i)
        xr = matmul_residual(xr, act, w_ffn_out_b, bm=1024, bn=512, bk=D_FF // 2, layer=i, name="ffn_out")
        hp = rmsnorm_rows(xr, norm_ple_g[i])
        xr = ple_update(xr, p_b, w_ple_b, hp, w_ple_gate, i)
    return xr.reshape(b, s, d)
```

```python
import functools
import math

import numpy as np
import jax
import jax.numpy as jnp
from jax import lax
from jax.experimental import pallas as pl
from jax.experimental.pallas import tpu as pltpu

D_MODEL = 4096
DEPTH = 2
EPS = 1e-6
PLE_DIM = 256
HEADS = 8
N_BRANCH = 4
BRANCH_W = 1024
DA_QK_DIM = 64
MLA_Q_RANK = 1024
MLA_KV_RANK = 512
MLA_NOPE = 128
MLA_ROPE = 64
ROPE_THETA = 10000.0
DIL_GROUPS = ((128, 1), (512, 4), (2048, 16))
DIL_STEPS = 128
GDN_CHUNK = 64
CONV_W = 4
N_BUCKETS = 32
MAX_DIST = 2048
D_FF = 11008

DA_COLS = 3072
MLA_COLS = 1600
DIL_COLS = 9216
GDN_QKV = 3072

LANE = 128
Z_A = 0
Z_B = Z_A + DA_COLS
Z_C = Z_B + MLA_COLS + 64
Z_D = Z_C + DIL_COLS
Z_COLS = 18 * 1024

NEG = -1e30
ATT_T = 256
VMEM_LIMIT = 56 * 1024 * 1024
BF = jnp.bfloat16
F32 = jnp.float32


def _cparams(sem):
    return pltpu.CompilerParams(dimension_semantics=sem, vmem_limit_bytes=VMEM_LIMIT)


def _rmsnorm_kernel(x_ref, g_ref, o_ref):
    x = x_ref[...]
    ms = jnp.mean(x * x, axis=-1, keepdims=True)
    o_ref[...] = (x * lax.rsqrt(ms + EPS) * g_ref[...]).astype(o_ref.dtype)


def rmsnorm_rows(x, g, col_block=0, bm=512):
    m = x.shape[0]
    d = g.shape[-1]
    return pl.pallas_call(
        _rmsnorm_kernel,
        grid=(m // bm,),
        in_specs=[pl.BlockSpec((bm, d), lambda i: (i, col_block)),
                  pl.BlockSpec((1, d), lambda i: (0, 0))],
        out_specs=pl.BlockSpec((bm, d), lambda i: (i, 0)),
        out_shape=jax.ShapeDtypeStruct((m, d), BF),
        compiler_params=_cparams(("parallel",)),
        name="rmsnorm_rows",
    )(x, g.reshape(1, d))


def _mm_kernel(x_ref, w_ref, o_ref):
    o_ref[...] = jnp.dot(x_ref[...], w_ref[...], preferred_element_type=F32).astype(o_ref.dtype)


def _layer_spec(w, block, index_map, layer):
    if layer is None:
        return pl.BlockSpec(block, index_map)
    return pl.BlockSpec((None,) + block, lambda *g: (layer,) + index_map(*g))


def matmul(x, w, out_dtype, bm=1024, bn=1024, layer=None, name="matmul"):
    m, k = x.shape
    n = w.shape[-1]
    bn = min(bn, n)
    bm = min(bm, m)
    return pl.pallas_call(
        _mm_kernel,
        grid=(m // bm, n // bn),
        in_specs=[pl.BlockSpec((bm, k), lambda i, j: (i, 0)),
                  _layer_spec(w, (k, bn), lambda i, j: (0, j), layer)],
        out_specs=pl.BlockSpec((bm, bn), lambda i, j: (i, j)),
        out_shape=jax.ShapeDtypeStruct((m, n), out_dtype),
        compiler_params=_cparams(("parallel", "parallel")),
        name=name,
    )(x, w)


def _mm_residual_kernel(r_ref, x_ref, w_ref, o_ref, *acc, nk):
    part = jnp.dot(x_ref[...], w_ref[...], preferred_element_type=F32)
    if nk == 1:
        o_ref[...] = r_ref[...] + part
    else:
        acc_ref, = acc

        @pl.when(pl.program_id(2) == 0)
        def _():
            acc_ref[...] = r_ref[...]

        total = acc_ref[...] + part
        acc_ref[...] = total
        o_ref[...] = total


def matmul_residual(r, x, w, bm=1024, bn=512, bk=None, layer=None, name="matmul_residual"):
    m, kdim = x.shape
    n = w.shape[-1]
    bk = kdim if bk is None else bk
    bm = min(bm, m)
    nk = kdim // bk
    return pl.pallas_call(
        functools.partial(_mm_residual_kernel, nk=nk),
        grid=(m // bm, n // bn, nk),
        in_specs=[pl.BlockSpec((bm, bn), lambda i, j, k: (i, j)),
                  pl.BlockSpec((bm, bk), lambda i, j, k: (i, k)),
                  _layer_spec(w, (bk, bn), lambda i, j, k: (k, j), layer)],
        out_specs=pl.BlockSpec((bm, bn), lambda i, j, k: (i, j)),
        out_shape=jax.ShapeDtypeStruct((m, n), F32),
        scratch_shapes=[pltpu.VMEM((bm, bn), F32)] if nk > 1 else [],
        compiler_params=_cparams(("parallel", "parallel", "arbitrary")),
        name=name,
    )(r, x, w)


def _merge_kernel(h_ref, wg_ref, b_ref, o_ref, wb_ref, out_ref, acc_ref):
    n = pl.program_id(2)
    gate = jax.nn.sigmoid(jnp.dot(h_ref[...], wg_ref[...].astype(BF), preferred_element_type=F32) + b_ref[...])
    term = gate * jnp.dot(o_ref[...], wb_ref[...], preferred_element_type=F32)

    @pl.when(n == 0)
    def _():
        acc_ref[...] = jnp.zeros(acc_ref.shape, F32)

    total = acc_ref[...] + term
    acc_ref[...] = total
    out_ref[...] = total.astype(out_ref.dtype)


def gated_merge(h, w_bgate, b_bgate, o_all, w_branch, layer, bm=1024, bn=512):
    m, d = h.shape
    bm = min(bm, m)
    return pl.pallas_call(
        _merge_kernel,
        grid=(m // bm, d // bn, N_BRANCH),
        in_specs=[pl.BlockSpec((bm, d), lambda i, j, n: (i, 0)),
                  pl.BlockSpec((None, None, d, bn), lambda i, j, n: (layer, n, 0, j)),
                  pl.BlockSpec((None, 1, bn), lambda i, j, n: (n, 0, j)),
                  pl.BlockSpec((None, bm, BRANCH_W), lambda i, j, n: (n, i, 0)),
                  pl.BlockSpec((None, None, BRANCH_W, bn), lambda i, j, n: (layer, n, 0, j))],
        out_specs=pl.BlockSpec((bm, bn), lambda i, j, n: (i, j)),
        out_shape=jax.ShapeDtypeStruct((m, d), BF),
        scratch_shapes=[pltpu.VMEM((bm, bn), F32)],
        compiler_params=_cparams(("parallel", "parallel", "arbitrary")),
        name="gated_merge",
    )(h, w_bgate, b_bgate, o_all, w_branch)


def _swiglu_kernel(x_ref, wg_ref, wu_ref, o_ref):
    x = x_ref[...]
    g = jnp.dot(x, wg_ref[...].astype(BF), preferred_element_type=F32)
    u = jnp.dot(x, wu_ref[...].astype(BF), preferred_element_type=F32)
    o_ref[...] = (g * jax.nn.sigmoid(g) * u).astype(o_ref.dtype)


def swiglu_in(x, w, layer, bm=2048, bn=256):
    m, k = x.shape
    n = w.shape[-1] // 2
    bm = min(bm, m)
    nb = n // bn
    return pl.pallas_call(
        _swiglu_kernel,
        grid=(m // bm, nb),
        in_specs=[pl.BlockSpec((bm, k), lambda i, j: (i, 0)),
                  pl.BlockSpec((None, k, bn), lambda i, j: (layer, 0, j)),
                  pl.BlockSpec((None, k, bn), lambda i, j: (layer, 0, nb + j))],
        out_specs=pl.BlockSpec((bm, bn), lambda i, j: (i, j)),
        out_shape=jax.ShapeDtypeStruct((m, n), BF),
        compiler_params=_cparams(("parallel", "parallel")),
        name="swiglu_in",
    )(x, w, w)


def _ple_kernel(r_ref, p_ref, wp_ref, h_ref, wg_ref, o_ref):
    e = jnp.dot(p_ref[...].astype(BF), wp_ref[...], preferred_element_type=F32)
    g = jnp.dot(h_ref[...], wg_ref[...].astype(BF), preferred_element_type=F32)
    o_ref[...] = r_ref[...] + e * jax.nn.sigmoid(g)


def ple_update(r, p, w_ple, hp, w_gate, layer, bm=1024, bn=512):
    m, d = r.shape
    kp = p.shape[-1]
    bm = min(bm, m)
    return pl.pallas_call(
        _ple_kernel,
        grid=(m // bm, d // bn),
        in_specs=[pl.BlockSpec((bm, bn), lambda i, j: (i, j)),
                  pl.BlockSpec((None, bm, kp), lambda i, j: (layer, i, 0)),
                  pl.BlockSpec((None, kp, bn), lambda i, j: (layer, 0, j)),
                  pl.BlockSpec((bm, d), lambda i, j: (i, 0)),
                  pl.BlockSpec((None, d, bn), lambda i, j: (layer, 0, j))],
        out_specs=pl.BlockSpec((bm, bn), lambda i, j: (i, j)),
        out_shape=jax.ShapeDtypeStruct((m, d), F32),
        compiler_params=_cparams(("parallel", "parallel")),
        name="ple_update",
    )(r, p, w_ple, hp, w_gate)


def _static_buckets(dist):
    max_exact = N_BUCKETS // 2
    d = np.maximum(np.asarray(dist), 0)
    large = max_exact + (np.log(np.maximum(d, 1).astype(np.float32) / np.float32(max_exact))
                         / np.float32(math.log(MAX_DIST / max_exact))
                         * np.float32(N_BUCKETS - max_exact)).astype(np.int32)
    large = np.minimum(large, N_BUCKETS - 1)
    return np.where(d < max_exact, d, large).astype(np.int32)


def _toeplitz_tiles(f, t):
    hh, n = f.shape
    big = n + t - 1
    fpad = jnp.concatenate([jnp.full((hh, t - 1), NEG, f.dtype), f], axis=1)
    flat = jnp.tile(fpad, (1, t + 1))[:, :t * (big + 1)]
    w = flat.reshape(hh, t, big + 1)[:, :, :n]
    w = jnp.flip(w.reshape(hh, t, n // t, t), axis=3)
    return w.transpose(0, 2, 1, 3)


def _bias_tile_kernel(prev_ref, cur_ref, o_ref, *, t):
    row = jnp.concatenate([prev_ref[...], cur_ref[...]], axis=1)
    rolled = pltpu.roll(jnp.broadcast_to(row, (t, 2 * t)), 0, 1, stride=1, stride_axis=0)
    o_ref[...] = rolled[:, t:]


def diff_bias_tiles(rel_bias, s):
    t = min(ATT_T, s)
    nq = s // t
    f = jnp.take(rel_bias[:, :HEADS], _static_buckets(np.arange(s)), axis=0).T.astype(F32)
    fb = jnp.concatenate([jnp.full((HEADS, t), NEG, F32), f], axis=1).reshape(HEADS, nq + 1, 1, t)
    return pl.pallas_call(
        functools.partial(_bias_tile_kernel, t=t),
        grid=(HEADS, nq),
        in_specs=[pl.BlockSpec((None, None, 1, t), lambda h, d: (h, d, 0, 0)),
                  pl.BlockSpec((None, None, 1, t), lambda h, d: (h, d + 1, 0, 0))],
        out_specs=pl.BlockSpec((None, None, t, t), lambda h, d: (h, d, 0, 0)),
        out_shape=jax.ShapeDtypeStruct((HEADS, nq, t, t), F32),
        compiler_params=_cparams(("parallel", "parallel")),
        name="diff_bias_tiles",
    )(fb, fb)


def dil_bias_tiles(rel_bias):
    out = []
    for gi, (_, dil) in enumerate(DIL_GROUPS):
        lo = HEADS + gi * HEADS
        steps = np.arange(2 * DIL_STEPS)
        f = jnp.take(rel_bias[:, lo:lo + HEADS], _static_buckets(steps * dil), axis=0).T.astype(F32)
        f = jnp.where(steps[None, :] <= DIL_STEPS, f, NEG)
        tiles = _toeplitz_tiles(f, DIL_STEPS)
        general = jnp.concatenate([tiles[:, 1], tiles[:, 0]], axis=-1)
        first = jnp.concatenate([jnp.full_like(tiles[:, 1], NEG), tiles[:, 0]], axis=-1)
        out.append(jnp.stack([general, first], axis=1))
    return jnp.stack(out, axis=0)


def rope_tables(s):
    half = MLA_ROPE // 2
    inv = ROPE_THETA ** (-jnp.arange(half, dtype=F32) / half)
    ang = jnp.arange(s).astype(F32)[:, None] * inv[None, :]
    cos, sin = jnp.cos(ang), jnp.sin(ang)
    zero = jnp.zeros((s, LANE - MLA_ROPE), F32)
    return (jnp.concatenate([cos, cos, zero], axis=1), jnp.concatenate([-sin, sin, zero], axis=1))


def _rope_tile(t, c, s):
    half = MLA_ROPE // 2
    swapped = pltpu.roll(t, half, axis=1) + pltpu.roll(t, LANE - half, axis=1)
    return t * c + swapped * s


def _flash_kernel(lam_ref, qt_ref, k_ref, vt_ref, *rest, diff, t, scale, post_scale):
    if diff:
        bias_ref, g_ref, o_ref, m_sc, l_sc, acc_sc = rest
    else:
        o_ref, m_sc, l_sc, acc_sc = rest
    i = pl.program_id(2)
    m_sc[...] = jnp.full(m_sc.shape, NEG, F32)
    l_sc[...] = jnp.zeros(l_sc.shape, F32)
    acc_sc[...] = jnp.zeros(acc_sc.shape, F32)
    qts = [qt_ref[0], qt_ref[1]] if diff else [qt_ref[:, 0:t], qt_ref[:, t:2 * t]]

    def load_kv(j):
        start = pl.multiple_of(j * t, t)
        return k_ref[pl.ds(start, t), :], vt_ref[:, pl.ds(start, t)]

    def step(work):
        scores = [jnp.dot(kv[0], qts[c], preferred_element_type=F32) for c, kv, _, _ in work]
        for (c, kv, bias, diagonal), s in zip(work, scores):
            if scale != 1.0:
                s = s * scale
            if bias is not None:
                s = bias + s
            elif diagonal:
                key_i = lax.broadcasted_iota(jnp.int32, (t, t), 0)
                qry_i = lax.broadcasted_iota(jnp.int32, (t, t), 1)
                s = jnp.where(key_i <= qry_i, s, NEG)
            m_prev = m_sc[c]
            m_new = jnp.maximum(m_prev, jnp.max(s, axis=0, keepdims=True))
            alpha = jnp.exp(m_prev - m_new)
            p = jnp.exp(s - m_new)
            l_sc[c] = alpha * l_sc[c] + jnp.sum(p, axis=0, keepdims=True)
            m_sc[c] = m_new
            acc_sc[c] = alpha * acc_sc[c] + jnp.dot(kv[1], p.astype(BF), preferred_element_type=F32)

    def sweep(n, work):
        def body(jj, carry):
            step(work(4 * jj) + work(4 * jj + 1) + work(4 * jj + 2) + work(4 * jj + 3))
            return carry

        lax.fori_loop(0, n // 4, body, 0)
        done = (n // 4) * 4

        @pl.when(n % 4 >= 2)
        def _():
            step(work(done) + work(done + 1))

        @pl.when(n % 2 == 1)
        def _():
            step(work(n - 1))

    if diff:
        def work(j):
            kv = load_kv(j)
            bias = bias_ref[i - j]
            return [(0, kv, bias, False), (1, kv, bias, False)]

        sweep(i + 1, work)
        out_t = acc_sc[0] / l_sc[0] - lam_ref[0] * (acc_sc[1] / l_sc[1])
        o = out_t.T
        ms = jnp.mean(o * o, axis=-1, keepdims=True)
        o_ref[...] = (o * lax.rsqrt(ms + EPS) * g_ref[...] * post_scale).astype(o_ref.dtype)
    else:
        def work(j):
            kv = load_kv(j)
            return [(0, kv, None, False), (1, kv, None, False)]

        sweep(2 * i, work)
        kv = load_kv(2 * i)
        step([(0, kv, None, True), (1, kv, None, False)])
        step([(1, load_kv(2 * i + 1), None, True)])
        o_ref[0:t, :] = (acc_sc[0] / l_sc[0]).T.astype(o_ref.dtype)
        o_ref[t:2 * t, :] = (acc_sc[1] / l_sc[1]).T.astype(o_ref.dtype)


def flash_attention(q, k, vt, s, *, diff, dk, scale, bias=None, lam=None, gain=None, post_scale=1.0, name):
    m = k.shape[0]
    b = m // s
    t = min(ATT_T, s)
    lam = jnp.zeros((1,), F32) if lam is None else lam.reshape(1).astype(F32)
    if diff:
        nq = s // t
        q_spec = pl.BlockSpec((2, dk, t), lambda bb, h, i: (0, h, bb * nq + i))
        o_rows = t
    else:
        nq = s // (2 * t)
        q_spec = pl.BlockSpec((dk, 2 * t), lambda bb, h, i: (h, bb * nq + i))
        o_rows = 2 * t
    in_specs = [pl.BlockSpec(memory_space=pltpu.SMEM), q_spec,
                pl.BlockSpec((s, dk), lambda bb, h, i: (bb, h)),
                pl.BlockSpec((LANE, s), lambda bb, h, i: (h, bb))]
    args = [lam, q, k, vt]
    if diff:
        in_specs += [pl.BlockSpec((None, nq, t, t), lambda bb, h, i: (h, 0, 0, 0)),
                     pl.BlockSpec((1, LANE), lambda bb, h, i: (0, 0))]
        args += [bias, gain.reshape(1, LANE).astype(F32)]
    return pl.pallas_call(
        functools.partial(_flash_kernel, diff=diff, t=t, scale=scale, post_scale=post_scale),
        grid=(b, HEADS, nq),
        in_specs=in_specs,
        out_specs=pl.BlockSpec((o_rows, LANE), lambda bb, h, i: (bb * nq + i, h)),
        out_shape=jax.ShapeDtypeStruct((m, HEADS * LANE), BF),
        scratch_shapes=[pltpu.VMEM((2, 1, t), F32), pltpu.VMEM((2, 1, t), F32), pltpu.VMEM((2, LANE, t), F32)],
        compiler_params=_cparams(("parallel", "parallel", "arbitrary")),
        name=name,
    )(*args)


def _prep_a_kernel(z_ref, qg_ref, kg_ref, qz_ref, kn_ref, vt_ref):
    lane = lax.broadcasted_iota(jnp.int32, (1, LANE), 1)
    lo = lane < DA_QK_DIM

    def norm_halves(x, g):
        x2 = x * x
        s_lo = jnp.sum(jnp.where(lo, x2, 0.0), axis=-1, keepdims=True)
        s_hi = jnp.sum(jnp.where(lo, 0.0, x2), axis=-1, keepdims=True)
        inv = lax.rsqrt(jnp.where(lo, s_lo, s_hi) * (1.0 / DA_QK_DIM) + EPS)
        return x * inv * g

    scale = DA_QK_DIM ** -0.5
    for h in range(HEADS):
        cq = slice(h * LANE, (h + 1) * LANE)
        ck = slice(HEADS * LANE + h * LANE, HEADS * LANE + (h + 1) * LANE)
        q = norm_halves(z_ref[:, cq], qg_ref[...]) * scale
        qt = q.T
        sub_lo = lax.broadcasted_iota(jnp.int32, (LANE, 1), 0) < DA_QK_DIM
        qz_ref[0, cq, :] = jnp.where(sub_lo, qt, 0.0).astype(BF)
        qz_ref[1, cq, :] = jnp.where(sub_lo, 0.0, qt).astype(BF)
        kn_ref[:, cq] = norm_halves(z_ref[:, ck], kg_ref[...]).astype(BF)
        cv = slice(2 * HEADS * LANE + h * LANE, 2 * HEADS * LANE + (h + 1) * LANE)
        vt_ref[cq, :] = z_ref[:, cv].T.astype(BF)


def prep_a(z, q_g, k_g, bm=256):
    m = z.shape[0]
    bm = min(bm, m)
    w = HEADS * LANE
    g2 = lambda g: jnp.concatenate([g, g]).reshape(1, LANE).astype(F32)
    return pl.pallas_call(
        _prep_a_kernel,
        grid=(m // bm,),
        in_specs=[pl.BlockSpec((bm, DA_COLS), lambda i: (i, Z_A // DA_COLS)),
                  pl.BlockSpec((1, LANE), lambda i: (0, 0)),
                  pl.BlockSpec((1, LANE), lambda i: (0, 0))],
        out_specs=[pl.BlockSpec((2, w, bm), lambda i: (0, 0, i)),
                   pl.BlockSpec((bm, w), lambda i: (i, 0)),
                   pl.BlockSpec((w, bm), lambda i: (0, i))],
        out_shape=[jax.ShapeDtypeStruct((2, w, m), BF), jax.ShapeDtypeStruct((m, w), BF),
                   jax.ShapeDtypeStruct((w, m), BF)],
        compiler_params=_cparams(("parallel",)),
        name="prep_a",
    )(z, g2(q_g), g2(k_g))


def mixer_a(z, s, lam, lam_init, q_g, k_g, o_g, bias_tiles):
    qz, kn, vt = prep_a(z, q_g, k_g)
    return flash_attention(qz, kn, vt, s, diff=True, dk=LANE, scale=1.0, bias=bias_tiles, lam=lam, gain=o_g,
                           post_scale=1.0 - lam_init, name="diff_attention")


def _prep_b_kernel(zq_ref, zkv_ref, zkr_ref, wq_ref, wkv_ref, cq_g, ckv_g, kr_g, qn_g, qr_g, kn_g, c_ref, s_ref,
                   qc_ref, kc_ref, vt_ref):
    def norm(x, g, width):
        ms = jnp.sum(x * x, axis=-1, keepdims=True) * (1.0 / width)
        return x * lax.rsqrt(ms + EPS) * g

    cos, sin = c_ref[...], s_ref[...]
    c_q = norm(zq_ref[...], cq_g[...], MLA_Q_RANK).astype(BF)
    c_kv = norm(zkv_ref[...], ckv_g[...], MLA_KV_RANK).astype(BF)
    q_up = jnp.dot(c_q, wq_ref[...], preferred_element_type=F32)
    kv_up = jnp.dot(c_kv, wkv_ref[...], preferred_element_type=F32)
    kr = _rope_tile(norm(zkr_ref[...], kr_g[...], MLA_ROPE), cos, sin).astype(BF)
    for h in range(HEADS):
        c0 = slice(2 * h * LANE, (2 * h + 1) * LANE)
        c1 = slice((2 * h + 1) * LANE, (2 * h + 2) * LANE)
        qc_ref[c0, :] = norm(q_up[:, c0], qn_g[...], MLA_NOPE).T.astype(BF)
        qc_ref[c1, :] = _rope_tile(norm(q_up[:, c1], qr_g[...], MLA_ROPE), cos, sin).T.astype(BF)
        kc_ref[:, c0] = norm(kv_up[:, c0], kn_g[...], MLA_NOPE).astype(BF)
        kc_ref[:, c1] = kr
        vt_ref[h * LANE:(h + 1) * LANE, :] = kv_up[:, c1].T.astype(BF)


def prep_b(z, w_uq, w_ukv, layer, cq_g, ckv_g, kr_g, qn_g, qr_g, kn_g, cos_t, sin_t, s, bm=256):
    m = z.shape[0]
    bm = min(bm, s)
    nsb = s // bm
    w2 = 2 * HEADS * LANE
    row = lambda g: g.reshape(1, -1).astype(F32)
    pad = lambda g: jnp.concatenate([g, jnp.zeros((LANE - MLA_ROPE,), g.dtype)])
    const = lambda i: (0, 0)
    return pl.pallas_call(
        _prep_b_kernel,
        grid=(m // bm,),
        in_specs=[pl.BlockSpec((bm, MLA_Q_RANK), lambda i: (i, Z_B // MLA_Q_RANK)),
                  pl.BlockSpec((bm, MLA_KV_RANK), lambda i: (i, (Z_B + MLA_Q_RANK) // MLA_KV_RANK)),
                  pl.BlockSpec((bm, LANE), lambda i: (i, (Z_B + MLA_Q_RANK + MLA_KV_RANK) // LANE)),
                  pl.BlockSpec((MLA_Q_RANK, w2), const),
                  pl.BlockSpec((None, MLA_KV_RANK, w2), lambda i: (layer, 0, 0)),
                  pl.BlockSpec((1, MLA_Q_RANK), const),
                  pl.BlockSpec((1, MLA_KV_RANK), const),
                  pl.BlockSpec((1, LANE), const),
                  pl.BlockSpec((1, LANE), const),
                  pl.BlockSpec((1, LANE), const),
                  pl.BlockSpec((1, LANE), const),
                  pl.BlockSpec((bm, LANE), lambda i: (i % nsb, 0)),
                  pl.BlockSpec((bm, LANE), lambda i: (i % nsb, 0))],
        out_specs=[pl.BlockSpec((w2, bm), lambda i: (0, i)),
                   pl.BlockSpec((bm, w2), lambda i: (i, 0)),
                   pl.BlockSpec((HEADS * LANE, bm), lambda i: (0, i))],
        out_shape=[jax.ShapeDtypeStruct((w2, m), BF), jax.ShapeDtypeStruct((m, w2), BF),
                   jax.ShapeDtypeStruct((HEADS * LANE, m), BF)],
        compiler_params=_cparams(("parallel",)),
        name="prep_b",
    )(z, z, z, w_uq, w_ukv, row(cq_g), row(ckv_g), row(pad(kr_g)), row(qn_g), row(pad(qr_g)), row(kn_g),
      cos_t, sin_t)


def pack_w_uq(w):
    w = w.reshape(MLA_Q_RANK, HEADS, MLA_NOPE + MLA_ROPE)
    w = jnp.pad(w, ((0, 0), (0, 0), (0, 2 * LANE - MLA_NOPE - MLA_ROPE)))
    return w.reshape(MLA_Q_RANK, HEADS * 2 * LANE).astype(BF)


def mixer_b(z, s, w_uq, w_ukv, cq_g, ckv_g, qn_g, kn_g, qr_g, kr_g, cos_t, sin_t, layer=0):
    qc, kc, vt = prep_b(z, pack_w_uq(w_uq), w_ukv, layer, cq_g, ckv_g, kr_g, qn_g, qr_g, kn_g, cos_t, sin_t, s)
    return flash_attention(qc, kc, vt, s, diff=False, dk=2 * LANE, scale=(MLA_NOPE + MLA_ROPE) ** -0.5,
                           name="mla_attention")


DIL_PAD = DIL_STEPS * max(d for _, d in DIL_GROUPS)


def _dil_kernel(zq_ref, zk_ref, zv_ref, bias_ref, qg_ref, kg_ref, o_ref, q_sc, k_sc, v_sc, og_sc, lse_sc, *, s):
    g = pl.program_id(2)
    ngroups = len(DIL_GROUPS)

    def norm(x, gain):
        ms = jnp.mean(x * x, axis=-1, keepdims=True)
        return x * lax.rsqrt(ms + EPS) * gain

    zeros = jnp.zeros((DIL_PAD, LANE), F32)
    k_sc[0:DIL_PAD, :] = zeros
    v_sc[0:DIL_PAD, :] = zeros
    q_sc[...] = norm(zq_ref[...], qg_ref[...])
    k_sc[DIL_PAD:, :] = norm(zk_ref[...], kg_ref[...])
    v_sc[DIL_PAD:, :] = zv_ref[...]
    scale = LANE ** -0.5

    def group(gi, dil):
        nsub = s // DIL_STEPS

        def body(tt, carry):
            c = tt % dil
            n = tt // dil
            q0 = c + dil * DIL_STEPS * n
            rows = pl.ds(q0, DIL_STEPS, stride=dil) if dil > 1 else pl.ds(q0, DIL_STEPS)
            k0 = q0 + DIL_PAD - dil * DIL_STEPS
            band = pl.ds(k0, 2 * DIL_STEPS, stride=dil) if dil > 1 else pl.ds(k0, 2 * DIL_STEPS)
            qs = q_sc[rows, :].astype(BF)
            ks = k_sc[band, :].astype(BF)
            vs = v_sc[band, :].astype(BF)
            first = jnp.where(n == 0, 1, 0)
            logits = lax.dot_general(qs, ks, (((1,), (1,)), ((), ())), preferred_element_type=F32) * scale
            logits = logits + bias_ref[first]
            mx = jnp.max(logits, axis=-1, keepdims=True)
            e = jnp.exp(logits - mx)
            den = jnp.sum(e, axis=-1, keepdims=True)
            o = jnp.dot((e / den).astype(BF), vs, preferred_element_type=F32)
            og_sc[gi, rows, :] = o
            lse_sc[gi, rows, :] = jnp.broadcast_to(mx + jnp.log(den), (DIL_STEPS, LANE))
            return carry

        lax.fori_loop(0, nsub, body, 0, unroll=8)

    for gi, (_, dil) in enumerate(DIL_GROUPS):
        pl.when(g == gi)(functools.partial(group, gi, dil))

    @pl.when(g == ngroups - 1)
    def _():
        lses = [lse_sc[gi] for gi in range(ngroups)]
        mx = functools.reduce(jnp.maximum, lses)
        ws = [jnp.exp(l - mx) for l in lses]
        tot = functools.reduce(lambda a, b2: a + b2, ws)
        acc = ws[0] * og_sc[0]
        for gi in range(1, ngroups):
            acc = acc + ws[gi] * og_sc[gi]
        o_ref[...] = (acc / tot).astype(o_ref.dtype)


def mixer_c(z, s, q_g, k_g, bias_c):
    m = z.shape[0]
    b = m // s
    ngroups = len(DIL_GROUPS)
    cb = Z_C // LANE

    def col(which):
        return lambda bb, h, g: (bb, cb + (g * 3 + which) * HEADS + h)

    row = lambda g: g.reshape(1, LANE).astype(F32)
    return pl.pallas_call(
        functools.partial(_dil_kernel, s=s),
        grid=(b, HEADS, ngroups),
        in_specs=[pl.BlockSpec((s, LANE), col(0)),
                  pl.BlockSpec((s, LANE), col(1)),
                  pl.BlockSpec((s, LANE), col(2)),
                  pl.BlockSpec((None, None, 2, DIL_STEPS, 2 * DIL_STEPS), lambda bb, h, g: (g, h, 0, 0, 0)),
                  pl.BlockSpec((1, LANE), lambda bb, h, g: (0, 0)),
                  pl.BlockSpec((1, LANE), lambda bb, h, g: (0, 0))],
        out_specs=pl.BlockSpec((s, LANE), lambda bb, h, g: (bb, h)),
        out_shape=jax.ShapeDtypeStruct((m, HEADS * LANE), BF),
        scratch_shapes=[pltpu.VMEM((s, LANE), F32), pltpu.VMEM((DIL_PAD + s, LANE), F32),
                        pltpu.VMEM((DIL_PAD + s, LANE), F32), pltpu.VMEM((ngroups, s, LANE), F32),
                        pltpu.VMEM((ngroups, s, LANE), F32)],
        compiler_params=_cparams(("parallel", "parallel", "arbitrary")),
        name="dilated_attention",
    )(z, z, z, bias_c, row(q_g), row(k_g))


CONV_PAD = 8
GDN_GROUP = 4
GDN_BASE = 8


def _gdn_kernel(par_ref, zq_ref, zk_ref, zv_ref, zg_ref, zab_ref, wq_ref, wk_ref, wv_ref, og_ref, o_ref,
                x_sc, q_sc, k_sc, v_sc, g_sc, b_sc, u_sc, w_sc, a_sc, st_sc, *, s):
    h = pl.program_id(1)
    c = GDN_CHUNK

    def conv_silu(z_ref, w_ref):
        x_sc[0:CONV_PAD, :] = jnp.zeros((CONV_PAD, LANE), F32)
        x_sc[CONV_PAD:, :] = z_ref[...]
        y = x_sc[CONV_PAD:, :] * w_ref[CONV_W - 1:CONV_W, :]
        for i in range(CONV_W - 1):
            off = CONV_PAD - (CONV_W - 1) + i
            y = y + x_sc[off:off + s, :] * w_ref[i:i + 1, :]
        return y * jax.nn.sigmoid(y)

    def l2(x):
        return x * lax.rsqrt(jnp.sum(x * x, axis=-1, keepdims=True) + EPS)

    q_sc[...] = l2(conv_silu(zq_ref, wq_ref)) * (LANE ** -0.5)
    k_sc[...] = l2(conv_silu(zk_ref, wk_ref))
    v_sc[...] = conv_silu(zv_ref, wv_ref)
    lane = lax.broadcasted_iota(jnp.int32, (1, LANE), 1)
    ab = zab_ref[...]
    a_col = jnp.sum(jnp.where(lane == h, ab, 0.0), axis=-1, keepdims=True)
    b_col = jnp.sum(jnp.where(lane == h + HEADS, ab, 0.0), axis=-1, keepdims=True)
    a_neg_exp = par_ref[0, h]
    dt_bias = par_ref[1, h]
    g_sc[...] = jnp.broadcast_to(a_neg_exp * jax.nn.softplus(a_col + dt_bias), (s, LANE))
    b_sc[...] = jnp.broadcast_to(jax.nn.sigmoid(b_col), (s, LANE))
    st_sc[...] = jnp.zeros(st_sc.shape, F32)

    gr = GDN_GROUP * c
    r_i = lax.broadcasted_iota(jnp.int32, (gr, gr), 0)
    c_i = lax.broadcasted_iota(jnp.int32, (gr, gr), 1)
    blk = {}
    size = GDN_BASE
    while size <= c:
        blk[size] = (r_i // size) == (c_i // size)
        size *= 2
    same = blk[c]
    tril = same & (r_i >= c_i)
    strict = same & (r_i > c_i)
    tril_b = tril.astype(BF)
    triu_b = (same & (r_i <= c_i)).astype(BF)
    eye = (r_i == c_i).astype(F32)

    def mm(a, b2):
        return jnp.dot(a.astype(BF), b2.astype(BF), preferred_element_type=F32)

    def mm_t(a, b2):
        return lax.dot_general(a.astype(BF), b2.astype(BF), (((1,), (1,)), ((), ())), preferred_element_type=F32)

    def split2(x):
        hi = x.astype(BF)
        return hi, (x - hi.astype(F32)).astype(BF)

    def split3(x):
        hi = x.astype(BF)
        r1 = x - hi.astype(F32)
        mid = r1.astype(BF)
        return hi, mid, (r1 - mid.astype(F32)).astype(BF)

    def mm_hi(a, b2):
        a_hi, a_lo = split2(a)
        b_hi, b_lo = split2(b2)
        d = functools.partial(jnp.dot, preferred_element_type=F32)
        return d(a_hi, b_hi) + (d(a_hi, b_lo) + d(a_lo, b_hi))

    def widen(x):
        return jnp.concatenate([x] * (gr // LANE), axis=1)

    def local_group(gi, carry):
        base = pl.multiple_of(gi * gr, gr)
        rows = pl.ds(base, gr)
        q = q_sc[rows, :]
        k = k_sc[rows, :]
        beta = b_sc[rows, :]
        parts = split3(g_sc[rows, :])
        gc = sum(jnp.dot(tril_b, part, preferred_element_type=F32) for part in parts)
        g_row = sum(lax.dot_general(widen(part), triu_b, (((0,), (0,)), ((), ())), preferred_element_type=F32)
                    for part in parts)
        decay = jnp.exp(jnp.where(tril, widen(gc) - g_row, NEG))
        kb = k * beta
        vb = v_sc[rows, :] * beta
        lower = jnp.where(strict, mm_t(kb, k) * decay, 0.0)
        neg = -jnp.where(blk[GDN_BASE], lower, 0.0)
        p1 = mm_hi(neg, neg)
        tmat = eye + neg
        tmat = tmat + mm_hi(tmat, p1)
        tmat = tmat + mm_hi(tmat, mm_hi(p1, p1))
        size = GDN_BASE
        while size < c:
            off = jnp.where(blk[2 * size] & jnp.logical_not(blk[size]), lower, 0.0)
            tmat = tmat - mm_hi(mm_hi(tmat, off), tmat)
            size *= 2
        eg = jnp.exp(gc)
        u_sc[rows, :] = mm(tmat, vb)
        w_sc[rows, :] = mm(tmat, kb * eg)
        intra = mm_t(q, k) * decay
        q_sc[rows, :] = q * eg
        for uu in range(GDN_GROUP):
            sl = slice(uu * c, (uu + 1) * c)
            crow = pl.ds(base + uu * c, c)
            g_last = gc[(uu + 1) * c - 1:(uu + 1) * c, :]
            a_sc[crow, 0:c] = intra[sl, sl]
            k_sc[crow, :] = k[sl] * jnp.exp(g_last - gc[sl])
            g_sc[crow, :] = jnp.broadcast_to(jnp.exp(g_last), (c, LANE))
        return carry


    def scan(n, carry):
        rows = pl.ds(pl.multiple_of(n * c, c), c)
        state = st_sc[...]
        v_new = u_sc[rows, :] - mm(w_sc[rows, :], state)
        o = mm(q_sc[rows, :], state) + mm(a_sc[rows, 0:c], v_new)
        decay_last = g_sc[pl.ds(pl.multiple_of(n * c, c), 1), :]
        st_sc[...] = state * decay_last + lax.dot_general(
            k_sc[rows, :].astype(BF), v_new.astype(BF), (((0,), (0,)), ((), ())), preferred_element_type=F32)
        u_sc[rows, :] = o
        return carry

    ngroups = s // gr
    local_group(0, 0)

    def pipelined(g, carry):
        local_group(g + 1, 0)
        for uu in range(GDN_GROUP):
            scan(g * GDN_GROUP + uu, 0)
        return carry

    lax.fori_loop(0, ngroups - 1, pipelined, 0)
    for uu in range(GDN_GROUP):
        scan((ngroups - 1) * GDN_GROUP + uu, 0)
    o = u_sc[...]
    ms = jnp.mean(o * o, axis=-1, keepdims=True)
    gate = zg_ref[...]
    o_ref[...] = (o * lax.rsqrt(ms + EPS) * og_ref[...] * (gate * jax.nn.sigmoid(gate))).astype(o_ref.dtype)


def mixer_d(z, s, conv_w, a_log, dt_bias, o_g):
    m = z.shape[0]
    b = m // s
    cb = Z_D // LANE
    par = jnp.stack([-jnp.exp(a_log.astype(F32)), dt_bias.astype(F32)], axis=0)
    cw = conv_w.astype(F32)

    def col(which):
        return lambda bb, h: (bb, cb + which * HEADS + h)

    return pl.pallas_call(
        functools.partial(_gdn_kernel, s=s),
        grid=(b, HEADS),
        in_specs=[pl.BlockSpec(memory_space=pltpu.SMEM),
                  pl.BlockSpec((s, LANE), col(0)),
                  pl.BlockSpec((s, LANE), col(1)),
                  pl.BlockSpec((s, LANE), col(2)),
                  pl.BlockSpec((s, LANE), col(3)),
                  pl.BlockSpec((s, LANE), lambda bb, h: (bb, cb + 4 * HEADS)),
                  pl.BlockSpec((CONV_W, LANE), lambda bb, h: (0, h)),
                  pl.BlockSpec((CONV_W, LANE), lambda bb, h: (0, HEADS + h)),
                  pl.BlockSpec((CONV_W, LANE), lambda bb, h: (0, 2 * HEADS + h)),
                  pl.BlockSpec((1, LANE), lambda bb, h: (0, 0))],
        out_specs=pl.BlockSpec((s, LANE), lambda bb, h: (bb, h)),
        out_shape=jax.ShapeDtypeStruct((m, HEADS * LANE), BF),
        scratch_shapes=[pltpu.VMEM((CONV_PAD + s, LANE), F32), pltpu.VMEM((s, LANE), F32),
                        pltpu.VMEM((s, LANE), F32), pltpu.VMEM((s, LANE), F32), pltpu.VMEM((s, LANE), F32),
                        pltpu.VMEM((s, LANE), F32), pltpu.VMEM((s, LANE), F32), pltpu.VMEM((s, LANE), F32),
                        pltpu.VMEM((s, LANE), F32), pltpu.VMEM((LANE, LANE), F32)],
        compiler_params=_cparams(("parallel", "arbitrary")),
        name="gated_deltanet",
    )(par, z, z, z, z, z, cw, cw, cw, o_g.reshape(1, LANE).astype(F32))


def _cast_kernel(x_ref, o_ref):
    o_ref[...] = x_ref[...].astype(o_ref.dtype)


def cast_bf16(w, bm=512):
    r, c = w.shape
    bm = min(bm, r)
    return pl.pallas_call(
        _cast_kernel,
        grid=(r // bm,),
        in_specs=[pl.BlockSpec((bm, c), lambda i: (i, 0))],
        out_specs=pl.BlockSpec((bm, c), lambda i: (i, 0)),
        out_shape=jax.ShapeDtypeStruct((r, c), BF),
        compiler_params=_cparams(("parallel",)),
        name="cast_bf16",
    )(w)


IN_COLS = DA_COLS + MLA_COLS + DIL_COLS + GDN_QKV + 2 * HEADS + HEADS * LANE


def _pack_w_in_kernel(w_ref, o_ref):
    rows = w_ref.shape[0]

    def put(dst, src, width):
        o_ref[:, dst:dst + width] = w_ref[:, src:src + width]

    def zero(dst, width):
        o_ref[:, dst:dst + width] = jnp.zeros((rows, width), BF)

    c0 = DA_COLS + MLA_COLS + DIL_COLS
    put(0, 0, DA_COLS + MLA_COLS)
    zero(Z_B + MLA_COLS, Z_C - Z_B - MLA_COLS)
    put(Z_C, DA_COLS + MLA_COLS, DIL_COLS)
    put(Z_D, c0, GDN_QKV)
    put(Z_D + GDN_QKV, c0 + GDN_QKV + 2 * HEADS, HEADS * LANE)
    put(Z_D + GDN_QKV + HEADS * LANE, c0 + GDN_QKV, 2 * HEADS)
    used = Z_D + GDN_QKV + HEADS * LANE + 2 * HEADS
    zero(used, Z_COLS - used)


def pack_w_in(w, bm=256):
    r = w.shape[0]
    return pl.pallas_call(
        _pack_w_in_kernel,
        grid=(r // bm,),
        in_specs=[pl.BlockSpec((bm, IN_COLS), lambda i: (i, 0))],
        out_specs=pl.BlockSpec((bm, Z_COLS), lambda i: (i, 0)),
        out_shape=jax.ShapeDtypeStruct((r, Z_COLS), BF),
        compiler_params=_cparams(("parallel",)),
        name="pack_w_in",
    )(w)


def _cast_stacked(w, bm=512):
    lead, c = w.shape[:-1], w.shape[-1]
    return cast_bf16(w.reshape(-1, c), bm).reshape(*lead, c)


def kernel(x, p, rel_bias, norm_mix_g, w_in, da_lambda, da_q_g, da_k_g, da_o_g, mla_w_uq, mla_w_ukv, mla_cq_g, mla_ckv_g, mla_qn_g, mla_kn_g, mla_qr_g, mla_kr_g, dil_q_g, dil_k_g, gdn_conv_w, gdn_a_log, gdn_dt_bias, gdn_o_g, w_bgate, b_bgate, w_branch, w_out, norm_ffn_g, w_ffn_in, w_ffn_out, norm_ple_g, w_ple, w_ple_gate):
    b, s, d = x.shape
    m = b * s
    bias_a = diff_bias_tiles(rel_bias, s)
    bias_c = dil_bias_tiles(rel_bias)
    cos_t, sin_t = rope_tables(s)
    w_in_p = pack_w_in(w_in.astype(BF).reshape(DEPTH * d, IN_COLS)).reshape(DEPTH, d, Z_COLS)
    w_branch_b = _cast_stacked(w_branch)
    w_out_b = _cast_stacked(w_out)
    w_ffn_out_b = _cast_stacked(w_ffn_out)
    w_ple_b = _cast_stacked(w_ple)
    w_ukv_b = _cast_stacked(mla_w_ukv)
    p_r = p.reshape(DEPTH, m, PLE_DIM)
    xr = x.reshape(m, d)
    for i in range(DEPTH):
        h = rmsnorm_rows(xr, norm_mix_g[i])
        z = matmul(h, w_in_p, F32, layer=i, name="in_proj")
        lam_init = 0.8 - 0.6 * math.exp(-0.3 * i)
        lq1, lk1, lq2, lk2 = (da_lambda[i, j].astype(F32) for j in range(4))
        lam = jnp.exp(jnp.sum(lq1 * lk1)) - jnp.exp(jnp.sum(lq2 * lk2)) + lam_init
        o_a = mixer_a(z, s, lam, lam_init, da_q_g[i], da_k_g[i], da_o_g[i], bias_a)
        o_b = mixer_b(z, s, mla_w_uq[i], w_ukv_b, mla_cq_g[i], mla_ckv_g[i], mla_qn_g[i], mla_kn_g[i],
                      mla_qr_g[i], mla_kr_g[i], cos_t, sin_t, layer=i)
        o_c = mixer_c(z, s, dil_q_g[i], dil_k_g[i], bias_c)
        o_d = mixer_d(z, s, gdn_conv_w[i], gdn_a_log[i], gdn_dt_bias[i], gdn_o_g[i])
        o_all = jnp.stack([o_a, o_b, o_c, o_d], axis=0)
        merged = gated_merge(h, w_bgate, b_bgate[i].reshape(N_BRANCH, 1, d), o_all, w_branch_b, i)
        xr = matmul_residual(xr, merged, w_out_b, bn=1024, layer=i, name="out_proj")
        hf = rmsnorm_rows(xr, norm_ffn_g[i])
        act = swiglu_in(hf, w_ffn_in, i)
        xr = matmul_residual(xr, act, w_ffn_out_b, bm=1024, bn=512, bk=D_FF // 2, layer=i, name="ffn_out")
        hp = rmsnorm_rows(xr, norm_ple_g[i])
        xr = ple_update(xr, p_r, w_ple_b, hp, w_ple_gate, i)
    return xr.reshape(b, s, d)
```

```python
import functools
import math

import numpy as np
import jax
import jax.numpy as jnp
from jax import lax
from jax.experimental import pallas as pl
from jax.experimental.pallas import tpu as pltpu

D_MODEL = 4096
DEPTH = 2
EPS = 1e-6
PLE_DIM = 256
HEADS = 8
N_BRANCH = 4
BRANCH_W = 1024
DA_QK_DIM = 64
MLA_Q_RANK = 1024
MLA_KV_RANK = 512
MLA_NOPE = 128
MLA_ROPE = 64
ROPE_THETA = 10000.0
DIL_GROUPS = ((128, 1), (512, 4), (2048, 16))
DIL_STEPS = 128
GDN_CHUNK = 64
CONV_W = 4
N_BUCKETS = 32
MAX_DIST = 2048
D_FF = 11008

DA_COLS = 3072
MLA_COLS = 1600
DIL_COLS = 9216
GDN_QKV = 3072

LANE = 128
Z_A = 0
Z_B = Z_A + DA_COLS
Z_C = Z_B + MLA_COLS + 64
Z_D = Z_C + DIL_COLS
Z_COLS = 18 * 1024

NEG = -1e30
ATT_T = 256
VMEM_LIMIT = 56 * 1024 * 1024
BF = jnp.bfloat16
F32 = jnp.float32


def _cparams(sem):
    return pltpu.CompilerParams(dimension_semantics=sem, vmem_limit_bytes=VMEM_LIMIT)


def _rmsnorm_kernel(x_ref, g_ref, o_ref):
    x = x_ref[...]
    ms = jnp.mean(x * x, axis=-1, keepdims=True)
    o_ref[...] = (x * lax.rsqrt(ms + EPS) * g_ref[...]).astype(o_ref.dtype)


def rmsnorm_rows(x, g, col_block=0, bm=512):
    m = x.shape[0]
    d = g.shape[-1]
    return pl.pallas_call(
        _rmsnorm_kernel,
        grid=(m // bm,),
        in_specs=[pl.BlockSpec((bm, d), lambda i: (i, col_block)),
                  pl.BlockSpec((1, d), lambda i: (0, 0))],
        out_specs=pl.BlockSpec((bm, d), lambda i: (i, 0)),
        out_shape=jax.ShapeDtypeStruct((m, d), BF),
        compiler_params=_cparams(("parallel",)),
        name="rmsnorm_rows",
    )(x, g.reshape(1, d))


def _mm_kernel(x_ref, w_ref, o_ref):
    o_ref[...] = jnp.dot(x_ref[...], w_ref[...], preferred_element_type=F32).astype(o_ref.dtype)


def _layer_spec(w, block, index_map, layer):
    if layer is None:
        return pl.BlockSpec(block, index_map)
    return pl.BlockSpec((None,) + block, lambda *g: (layer,) + index_map(*g))


def matmul(x, w, out_dtype, bm=1024, bn=1024, layer=None, name="matmul"):
    m, k = x.shape
    n = w.shape[-1]
    bn = min(bn, n)
    bm = min(bm, m)
    return pl.pallas_call(
        _mm_kernel,
        grid=(m // bm, n // bn),
        in_specs=[pl.BlockSpec((bm, k), lambda i, j: (i, 0)),
                  _layer_spec(w, (k, bn), lambda i, j: (0, j), layer)],
        out_specs=pl.BlockSpec((bm, bn), lambda i, j: (i, j)),
        out_shape=jax.ShapeDtypeStruct((m, n), out_dtype),
        compiler_params=_cparams(("parallel", "parallel")),
        name=name,
    )(x, w)


def _mm_residual_kernel(r_ref, x_ref, w_ref, o_ref, *acc, nk):
    part = jnp.dot(x_ref[...], w_ref[...], preferred_element_type=F32)
    if nk == 1:
        o_ref[...] = r_ref[...] + part
    else:
        acc_ref, = acc

        @pl.when(pl.program_id(2) == 0)
        def _():
            acc_ref[...] = r_ref[...]

        total = acc_ref[...] + part
        acc_ref[...] = total
        o_ref[...] = total


def matmul_residual(r, x, w, bm=1024, bn=512, bk=None, layer=None, name="matmul_residual"):
    m, kdim = x.shape
    n = w.shape[-1]
    bk = kdim if bk is None else bk
    bm = min(bm, m)
    nk = kdim // bk
    return pl.pallas_call(
        functools.partial(_mm_residual_kernel, nk=nk),
        grid=(m // bm, n // bn, nk),
        in_specs=[pl.BlockSpec((bm, bn), lambda i, j, k: (i, j)),
                  pl.BlockSpec((bm, bk), lambda i, j, k: (i, k)),
                  _layer_spec(w, (bk, bn), lambda i, j, k: (k, j), layer)],
        out_specs=pl.BlockSpec((bm, bn), lambda i, j, k: (i, j)),
        out_shape=jax.ShapeDtypeStruct((m, n), F32),
        scratch_shapes=[pltpu.VMEM((bm, bn), F32)] if nk > 1 else [],
        compiler_params=_cparams(("parallel", "parallel", "arbitrary")),
        name=name,
    )(r, x, w)


def _merge_kernel(h_ref, wg_ref, b_ref, o_ref, wb_ref, out_ref, acc_ref):
    n = pl.program_id(2)
    gate = jax.nn.sigmoid(jnp.dot(h_ref[...], wg_ref[...].astype(BF), preferred_element_type=F32) + b_ref[...])
    term = gate * jnp.dot(o_ref[...], wb_ref[...], preferred_element_type=F32)

    @pl.when(n == 0)
    def _():
        acc_ref[...] = jnp.zeros(acc_ref.shape, F32)

    total = acc_ref[...] + term
    acc_ref[...] = total
    out_ref[...] = total.astype(out_ref.dtype)


def gated_merge(h, w_bgate, b_bgate, o_all, w_branch, layer, bm=1024, bn=512):
    m, d = h.shape
    bm = min(bm, m)
    return pl.pallas_call(
        _merge_kernel,
        grid=(m // bm, d // bn, N_BRANCH),
        in_specs=[pl.BlockSpec((bm, d), lambda i, j, n: (i, 0)),
                  pl.BlockSpec((None, None, d, bn), lambda i, j, n: (layer, n, 0, j)),
                  pl.BlockSpec((None, 1, bn), lambda i, j, n: (n, 0, j)),
                  pl.BlockSpec((None, bm, BRANCH_W), lambda i, j, n: (n, i, 0)),
                  pl.BlockSpec((None, None, BRANCH_W, bn), lambda i, j, n: (layer, n, 0, j))],
        out_specs=pl.BlockSpec((bm, bn), lambda i, j, n: (i, j)),
        out_shape=jax.ShapeDtypeStruct((m, d), BF),
        scratch_shapes=[pltpu.VMEM((bm, bn), F32)],
        compiler_params=_cparams(("parallel", "parallel", "arbitrary")),
        name="gated_merge",
    )(h, w_bgate, b_bgate, o_all, w_branch)


def _swiglu_kernel(x_ref, wg_ref, wu_ref, o_ref):
    x = x_ref[...]
    g = jnp.dot(x, wg_ref[...].astype(BF), preferred_element_type=F32)
    u = jnp.dot(x, wu_ref[...].astype(BF), preferred_element_type=F32)
    o_ref[...] = (g * jax.nn.sigmoid(g) * u).astype(o_ref.dtype)


def swiglu_in(x, w, layer, bm=2048, bn=256):
    m, k = x.shape
    n = w.shape[-1] // 2
    bm = min(bm, m)
    nb = n // bn
    return pl.pallas_call(
        _swiglu_kernel,
        grid=(m // bm, nb),
        in_specs=[pl.BlockSpec((bm, k), lambda i, j: (i, 0)),
                  pl.BlockSpec((None, k, bn), lambda i, j: (layer, 0, j)),
                  pl.BlockSpec((None, k, bn), lambda i, j: (layer, 0, nb + j))],
        out_specs=pl.BlockSpec((bm, bn), lambda i, j: (i, j)),
        out_shape=jax.ShapeDtypeStruct((m, n), BF),
        compiler_params=_cparams(("parallel", "parallel")),
        name="swiglu_in",
    )(x, w, w)


def _ple_kernel(r_ref, p_ref, wp_ref, h_ref, wg_ref, o_ref):
    e = jnp.dot(p_ref[...].astype(BF), wp_ref[...], preferred_element_type=F32)
    g = jnp.dot(h_ref[...], wg_ref[...].astype(BF), preferred_element_type=F32)
    o_ref[...] = r_ref[...] + e * jax.nn.sigmoid(g)


def ple_update(r, p, w_ple, hp, w_gate, layer, bm=1024, bn=512):
    m, d = r.shape
    kp = p.shape[-1]
    bm = min(bm, m)
    return pl.pallas_call(
        _ple_kernel,
        grid=(m // bm, d // bn),
        in_specs=[pl.BlockSpec((bm, bn), lambda i, j: (i, j)),
                  pl.BlockSpec((None, bm, kp), lambda i, j: (layer, i, 0)),
                  pl.BlockSpec((None, kp, bn), lambda i, j: (layer, 0, j)),
                  pl.BlockSpec((bm, d), lambda i, j: (i, 0)),
                  pl.BlockSpec((None, d, bn), lambda i, j: (layer, 0, j))],
        out_specs=pl.BlockSpec((bm, bn), lambda i, j: (i, j)),
        out_shape=jax.ShapeDtypeStruct((m, d), F32),
        compiler_params=_cparams(("parallel", "parallel")),
        name="ple_update",
    )(r, p, w_ple, hp, w_gate)


def _static_buckets(dist):
    max_exact = N_BUCKETS // 2
    d = np.maximum(np.asarray(dist), 0)
    large = max_exact + (np.log(np.maximum(d, 1).astype(np.float32) / np.float32(max_exact))
                         / np.float32(math.log(MAX_DIST / max_exact))
                         * np.float32(N_BUCKETS - max_exact)).astype(np.int32)
    large = np.minimum(large, N_BUCKETS - 1)
    return np.where(d < max_exact, d, large).astype(np.int32)


def _toeplitz_tiles(f, t):
    hh, n = f.shape
    big = n + t - 1
    fpad = jnp.concatenate([jnp.full((hh, t - 1), NEG, f.dtype), f], axis=1)
    flat = jnp.tile(fpad, (1, t + 1))[:, :t * (big + 1)]
    w = flat.reshape(hh, t, big + 1)[:, :, :n]
    w = jnp.flip(w.reshape(hh, t, n // t, t), axis=3)
    return w.transpose(0, 2, 1, 3)


def _bias_tile_kernel(prev_ref, cur_ref, o_ref, *, t):
    row = jnp.concatenate([prev_ref[...], cur_ref[...]], axis=1)
    rolled = pltpu.roll(jnp.broadcast_to(row, (t, 2 * t)), 0, 1, stride=1, stride_axis=0)
    o_ref[...] = rolled[:, t:]


def diff_bias_tiles(rel_bias, s):
    t = min(ATT_T, s)
    nq = s // t
    f = jnp.take(rel_bias[:, :HEADS], _static_buckets(np.arange(s)), axis=0).T.astype(F32)
    fb = jnp.concatenate([jnp.full((HEADS, t), NEG, F32), f], axis=1).reshape(HEADS, nq + 1, 1, t)
    return pl.pallas_call(
        functools.partial(_bias_tile_kernel, t=t),
        grid=(HEADS, nq),
        in_specs=[pl.BlockSpec((None, None, 1, t), lambda h, d: (h, d, 0, 0)),
                  pl.BlockSpec((None, None, 1, t), lambda h, d: (h, d + 1, 0, 0))],
        out_specs=pl.BlockSpec((None, None, t, t), lambda h, d: (h, d, 0, 0)),
        out_shape=jax.ShapeDtypeStruct((HEADS, nq, t, t), F32),
        compiler_params=_cparams(("parallel", "parallel")),
        name="diff_bias_tiles",
    )(fb, fb)


def dil_bias_tiles(rel_bias):
    out = []
    for gi, (_, dil) in enumerate(DIL_GROUPS):
        lo = HEADS + gi * HEADS
        steps = np.arange(2 * DIL_STEPS)
        f = jnp.take(rel_bias[:, lo:lo + HEADS], _static_buckets(steps * dil), axis=0).T.astype(F32)
        f = jnp.where(steps[None, :] <= DIL_STEPS, f, NEG)
        tiles = _toeplitz_tiles(f, DIL_STEPS)
        general = jnp.concatenate([tiles[:, 1], tiles[:, 0]], axis=-1)
        first = jnp.concatenate([jnp.full_like(tiles[:, 1], NEG), tiles[:, 0]], axis=-1)
        out.append(jnp.stack([general, first], axis=1))
    return jnp.stack(out, axis=0)


def rope_tables(s):
    half = MLA_ROPE // 2
    inv = ROPE_THETA ** (-jnp.arange(half, dtype=F32) / half)
    ang = jnp.arange(s).astype(F32)[:, None] * inv[None, :]
    cos, sin = jnp.cos(ang), jnp.sin(ang)
    zero = jnp.zeros((s, LANE - MLA_ROPE), F32)
    return (jnp.concatenate([cos, cos, zero], axis=1), jnp.concatenate([-sin, sin, zero], axis=1))


def _rope_tile(t, c, s):
    half = MLA_ROPE // 2
    swapped = pltpu.roll(t, half, axis=1) + pltpu.roll(t, LANE - half, axis=1)
    return t * c + swapped * s


def _flash_kernel(lam_ref, qt_ref, k_ref, vt_ref, *rest, diff, t, scale, post_scale):
    if diff:
        bias_ref, g_ref, o_ref, m_sc, l_sc, acc_sc = rest
    else:
        o_ref, m_sc, l_sc, acc_sc = rest
    i = pl.program_id(2)
    m_sc[...] = jnp.full(m_sc.shape, NEG, F32)
    l_sc[...] = jnp.zeros(l_sc.shape, F32)
    acc_sc[...] = jnp.zeros(acc_sc.shape, F32)
    qts = [qt_ref[0], qt_ref[1]] if diff else [qt_ref[:, 0:t], qt_ref[:, t:2 * t]]

    def load_kv(j):
        start = pl.multiple_of(j * t, t)
        return k_ref[pl.ds(start, t), :], vt_ref[:, pl.ds(start, t)]

    def step(work):
        scores = [jnp.dot(kv[0], qts[c], preferred_element_type=F32) for c, kv, _, _ in work]
        for (c, kv, bias, diagonal), s in zip(work, scores):
            if scale != 1.0:
                s = s * scale
            if bias is not None:
                s = bias + s
            elif diagonal:
                key_i = lax.broadcasted_iota(jnp.int32, (t, t), 0)
                qry_i = lax.broadcasted_iota(jnp.int32, (t, t), 1)
                s = jnp.where(key_i <= qry_i, s, NEG)
            m_prev = m_sc[c]
            m_new = jnp.maximum(m_prev, jnp.max(s, axis=0, keepdims=True))
            alpha = jnp.exp(m_prev - m_new)
            p = jnp.exp(s - m_new)
            l_sc[c] = alpha * l_sc[c] + jnp.sum(p, axis=0, keepdims=True)
            m_sc[c] = m_new
            acc_sc[c] = alpha * acc_sc[c] + jnp.dot(kv[1], p.astype(BF), preferred_element_type=F32)

    def sweep(n, work):
        def body(jj, carry):
            step(work(4 * jj) + work(4 * jj + 1) + work(4 * jj + 2) + work(4 * jj + 3))
            return carry

        lax.fori_loop(0, n // 4, body, 0)
        done = (n // 4) * 4

        @pl.when(n % 4 >= 2)
        def _():
            step(work(done) + work(done + 1))

        @pl.when(n % 2 == 1)
        def _():
            step(work(n - 1))

    if diff:
        def work(j):
            kv = load_kv(j)
            bias = bias_ref[i - j]
            return [(0, kv, bias, False), (1, kv, bias, False)]

        sweep(i + 1, work)
        out_t = acc_sc[0] / l_sc[0] - lam_ref[0] * (acc_sc[1] / l_sc[1])
        o = out_t.T
        ms = jnp.mean(o * o, axis=-1, keepdims=True)
        o_ref[...] = (o * lax.rsqrt(ms + EPS) * g_ref[...] * post_scale).astype(o_ref.dtype)
    else:
        def work(j):
            kv = load_kv(j)
            return [(0, kv, None, False), (1, kv, None, False)]

        sweep(2 * i, work)
        kv = load_kv(2 * i)
        step([(0, kv, None, True), (1, kv, None, False)])
        step([(1, load_kv(2 * i + 1), None, True)])
        o_ref[0:t, :] = (acc_sc[0] / l_sc[0]).T.astype(o_ref.dtype)
        o_ref[t:2 * t, :] = (acc_sc[1] / l_sc[1]).T.astype(o_ref.dtype)


def flash_attention(q, k, vt, s, *, diff, dk, scale, bias=None, lam=None, gain=None, post_scale=1.0, name):
    m = k.shape[0]
    b = m // s
    t = min(ATT_T, s)
    lam = jnp.zeros((1,), F32) if lam is None else lam.reshape(1).astype(F32)
    if diff:
        nq = s // t
        q_spec = pl.BlockSpec((2, dk, t), lambda bb, h, i: (0, h, bb * nq + i))
        o_rows = t
    else:
        nq = s // (2 * t)
        q_spec = pl.BlockSpec((dk, 2 * t), lambda bb, h, i: (h, bb * nq + i))
        o_rows = 2 * t
    in_specs = [pl.BlockSpec(memory_space=pltpu.SMEM), q_spec,
                pl.BlockSpec((s, dk), lambda bb, h, i: (bb, h)),
                pl.BlockSpec((LANE, s), lambda bb, h, i: (h, bb))]
    args = [lam, q, k, vt]
    if diff:
        in_specs += [pl.BlockSpec((None, nq, t, t), lambda bb, h, i: (h, 0, 0, 0)),
                     pl.BlockSpec((1, LANE), lambda bb, h, i: (0, 0))]
        args += [bias, gain.reshape(1, LANE).astype(F32)]
    return pl.pallas_call(
        functools.partial(_flash_kernel, diff=diff, t=t, scale=scale, post_scale=post_scale),
        grid=(b, HEADS, nq),
        in_specs=in_specs,
        out_specs=pl.BlockSpec((o_rows, LANE), lambda bb, h, i: (bb * nq + i, h)),
        out_shape=jax.ShapeDtypeStruct((m, HEADS * LANE), BF),
        scratch_shapes=[pltpu.VMEM((2, 1, t), F32), pltpu.VMEM((2, 1, t), F32), pltpu.VMEM((2, LANE, t), F32)],
        compiler_params=_cparams(("parallel", "parallel", "arbitrary")),
        name=name,
    )(*args)


def _prep_a_kernel(z_ref, qg_ref, kg_ref, qz_ref, kn_ref, vt_ref):
    lane = lax.broadcasted_iota(jnp.int32, (1, LANE), 1)
    lo = lane < DA_QK_DIM

    def norm_halves(x, g):
        x2 = x * x
        s_lo = jnp.sum(jnp.where(lo, x2, 0.0), axis=-1, keepdims=True)
        s_hi = jnp.sum(jnp.where(lo, 0.0, x2), axis=-1, keepdims=True)
        inv = lax.rsqrt(jnp.where(lo, s_lo, s_hi) * (1.0 / DA_QK_DIM) + EPS)
        return x * inv * g

    scale = DA_QK_DIM ** -0.5
    for h in range(HEADS):
        cq = slice(h * LANE, (h + 1) * LANE)
        ck = slice(HEADS * LANE + h * LANE, HEADS * LANE + (h + 1) * LANE)
        q = norm_halves(z_ref[:, cq], qg_ref[...]) * scale
        qt = q.T
        sub_lo = lax.broadcasted_iota(jnp.int32, (LANE, 1), 0) < DA_QK_DIM
        qz_ref[0, cq, :] = jnp.where(sub_lo, qt, 0.0).astype(BF)
        qz_ref[1, cq, :] = jnp.where(sub_lo, 0.0, qt).astype(BF)
        kn_ref[:, cq] = norm_halves(z_ref[:, ck], kg_ref[...]).astype(BF)
        cv = slice(2 * HEADS * LANE + h * LANE, 2 * HEADS * LANE + (h + 1) * LANE)
        vt_ref[cq, :] = z_ref[:, cv].T.astype(BF)


def prep_a(z, q_g, k_g, bm=256):
    m = z.shape[0]
    bm = min(bm, m)
    w = HEADS * LANE
    g2 = lambda g: jnp.concatenate([g, g]).reshape(1, LANE).astype(F32)
    return pl.pallas_call(
        _prep_a_kernel,
        grid=(m // bm,),
        in_specs=[pl.BlockSpec((bm, DA_COLS), lambda i: (i, Z_A // DA_COLS)),
                  pl.BlockSpec((1, LANE), lambda i: (0, 0)),
                  pl.BlockSpec((1, LANE), lambda i: (0, 0))],
        out_specs=[pl.BlockSpec((2, w, bm), lambda i: (0, 0, i)),
                   pl.BlockSpec((bm, w), lambda i: (i, 0)),
                   pl.BlockSpec((w, bm), lambda i: (0, i))],
        out_shape=[jax.ShapeDtypeStruct((2, w, m), BF), jax.ShapeDtypeStruct((m, w), BF),
                   jax.ShapeDtypeStruct((w, m), BF)],
        compiler_params=_cparams(("parallel",)),
        name="prep_a",
    )(z, g2(q_g), g2(k_g))


def mixer_a(z, s, lam, lam_init, q_g, k_g, o_g, bias_tiles):
    qz, kn, vt = prep_a(z, q_g, k_g)
    return flash_attention(qz, kn, vt, s, diff=True, dk=LANE, scale=1.0, bias=bias_tiles, lam=lam, gain=o_g,
                           post_scale=1.0 - lam_init, name="diff_attention")


def _prep_b_kernel(zq_ref, zkv_ref, zkr_ref, wq_ref, wkv_ref, cq_g, ckv_g, kr_g, qn_g, qr_g, kn_g, c_ref, s_ref,
                   qc_ref, kc_ref, vt_ref):
    def norm(x, g, width):
        ms = jnp.sum(x * x, axis=-1, keepdims=True) * (1.0 / width)
        return x * lax.rsqrt(ms + EPS) * g

    cos, sin = c_ref[...], s_ref[...]
    c_q = norm(zq_ref[...], cq_g[...], MLA_Q_RANK).astype(BF)
    c_kv = norm(zkv_ref[...], ckv_g[...], MLA_KV_RANK).astype(BF)
    q_up = jnp.dot(c_q, wq_ref[...], preferred_element_type=F32)
    kv_up = jnp.dot(c_kv, wkv_ref[...], preferred_element_type=F32)
    kr = _rope_tile(norm(zkr_ref[...], kr_g[...], MLA_ROPE), cos, sin).astype(BF)
    for h in range(HEADS):
        c0 = slice(2 * h * LANE, (2 * h + 1) * LANE)
        c1 = slice((2 * h + 1) * LANE, (2 * h + 2) * LANE)
        qc_ref[c0, :] = norm(q_up[:, c0], qn_g[...], MLA_NOPE).T.astype(BF)
        qc_ref[c1, :] = _rope_tile(norm(q_up[:, c1], qr_g[...], MLA_ROPE), cos, sin).T.astype(BF)
        kc_ref[:, c0] = norm(kv_up[:, c0], kn_g[...], MLA_NOPE).astype(BF)
        kc_ref[:, c1] = kr
        vt_ref[h * LANE:(h + 1) * LANE, :] = kv_up[:, c1].T.astype(BF)


def prep_b(z, w_uq, w_ukv, layer, cq_g, ckv_g, kr_g, qn_g, qr_g, kn_g, cos_t, sin_t, s, bm=256):
    m = z.shape[0]
    bm = min(bm, s)
    nsb = s // bm
    w2 = 2 * HEADS * LANE
    row = lambda g: g.reshape(1, -1).astype(F32)
    pad = lambda g: jnp.concatenate([g, jnp.zeros((LANE - MLA_ROPE,), g.dtype)])
    const = lambda i: (0, 0)
    return pl.pallas_call(
        _prep_b_kernel,
        grid=(m // bm,),
        in_specs=[pl.BlockSpec((bm, MLA_Q_RANK), lambda i: (i, Z_B // MLA_Q_RANK)),
                  pl.BlockSpec((bm, MLA_KV_RANK), lambda i: (i, (Z_B + MLA_Q_RANK) // MLA_KV_RANK)),
                  pl.BlockSpec((bm, LANE), lambda i: (i, (Z_B + MLA_Q_RANK + MLA_KV_RANK) // LANE)),
                  pl.BlockSpec((MLA_Q_RANK, w2), const),
                  pl.BlockSpec((None, MLA_KV_RANK, w2), lambda i: (layer, 0, 0)),
                  pl.BlockSpec((1, MLA_Q_RANK), const),
                  pl.BlockSpec((1, MLA_KV_RANK), const),
                  pl.BlockSpec((1, LANE), const),
                  pl.BlockSpec((1, LANE), const),
                  pl.BlockSpec((1, LANE), const),
                  pl.BlockSpec((1, LANE), const),
                  pl.BlockSpec((bm, LANE), lambda i: (i % nsb, 0)),
                  pl.BlockSpec((bm, LANE), lambda i: (i % nsb, 0))],
        out_specs=[pl.BlockSpec((w2, bm), lambda i: (0, i)),
                   pl.BlockSpec((bm, w2), lambda i: (i, 0)),
                   pl.BlockSpec((HEADS * LANE, bm), lambda i: (0, i))],
        out_shape=[jax.ShapeDtypeStruct((w2, m), BF), jax.ShapeDtypeStruct((m, w2), BF),
                   jax.ShapeDtypeStruct((HEADS * LANE, m), BF)],
        compiler_params=_cparams(("parallel",)),
        name="prep_b",
    )(z, z, z, w_uq, w_ukv, row(cq_g), row(ckv_g), row(pad(kr_g)), row(qn_g), row(pad(qr_g)), row(kn_g),
      cos_t, sin_t)


def pack_w_uq(w):
    w = w.reshape(MLA_Q_RANK, HEADS, MLA_NOPE + MLA_ROPE)
    w = jnp.pad(w, ((0, 0), (0, 0), (0, 2 * LANE - MLA_NOPE - MLA_ROPE)))
    return w.reshape(MLA_Q_RANK, HEADS * 2 * LANE).astype(BF)


def mixer_b(z, s, w_uq, w_ukv, cq_g, ckv_g, qn_g, kn_g, qr_g, kr_g, cos_t, sin_t, layer=0):
    qc, kc, vt = prep_b(z, pack_w_uq(w_uq), w_ukv, layer, cq_g, ckv_g, kr_g, qn_g, qr_g, kn_g, cos_t, sin_t, s)
    return flash_attention(qc, kc, vt, s, diff=False, dk=2 * LANE, scale=(MLA_NOPE + MLA_ROPE) ** -0.5,
                           name="mla_attention")


DIL_PAD = DIL_STEPS * max(d for _, d in DIL_GROUPS)


def _dil_kernel(zq_ref, zk_ref, zv_ref, bias_ref, qg_ref, kg_ref, o_ref, q_sc, k_sc, v_sc, og_sc, lse_sc, *, s):
    g = pl.program_id(2)
    ngroups = len(DIL_GROUPS)

    def norm(x, gain):
        ms = jnp.mean(x * x, axis=-1, keepdims=True)
        return x * lax.rsqrt(ms + EPS) * gain

    zeros = jnp.zeros((DIL_PAD, LANE), F32)
    k_sc[0:DIL_PAD, :] = zeros
    v_sc[0:DIL_PAD, :] = zeros
    q_sc[...] = norm(zq_ref[...], qg_ref[...])
    k_sc[DIL_PAD:, :] = norm(zk_ref[...], kg_ref[...])
    v_sc[DIL_PAD:, :] = zv_ref[...]
    scale = LANE ** -0.5

    def group(gi, dil):
        nsub = s // DIL_STEPS

        def body(tt, carry):
            c = tt % dil
            n = tt // dil
            q0 = c + dil * DIL_STEPS * n
            rows = pl.ds(q0, DIL_STEPS, stride=dil) if dil > 1 else pl.ds(q0, DIL_STEPS)
            k0 = q0 + DIL_PAD - dil * DIL_STEPS
            band = pl.ds(k0, 2 * DIL_STEPS, stride=dil) if dil > 1 else pl.ds(k0, 2 * DIL_STEPS)
            qs = q_sc[rows, :].astype(BF)
            ks = k_sc[band, :].astype(BF)
            vs = v_sc[band, :].astype(BF)
            first = jnp.where(n == 0, 1, 0)
            logits = lax.dot_general(qs, ks, (((1,), (1,)), ((), ())), preferred_element_type=F32) * scale
            logits = logits + bias_ref[first]
            mx = jnp.max(logits, axis=-1, keepdims=True)
            e = jnp.exp(logits - mx)
            den = jnp.sum(e, axis=-1, keepdims=True)
            o = jnp.dot((e / den).astype(BF), vs, preferred_element_type=F32)
            og_sc[gi, rows, :] = o
            lse_sc[gi, rows, :] = jnp.broadcast_to(mx + jnp.log(den), (DIL_STEPS, LANE))
            return carry

        lax.fori_loop(0, nsub, body, 0, unroll=8)

    for gi, (_, dil) in enumerate(DIL_GROUPS):
        pl.when(g == gi)(functools.partial(group, gi, dil))

    @pl.when(g == ngroups - 1)
    def _():
        lses = [lse_sc[gi] for gi in range(ngroups)]
        mx = functools.reduce(jnp.maximum, lses)
        ws = [jnp.exp(l - mx) for l in lses]
        tot = functools.reduce(lambda a, b2: a + b2, ws)
        acc = ws[0] * og_sc[0]
        for gi in range(1, ngroups):
            acc = acc + ws[gi] * og_sc[gi]
        o_ref[...] = (acc / tot).astype(o_ref.dtype)


def mixer_c(z, s, q_g, k_g, bias_c):
    m = z.shape[0]
    b = m // s
    ngroups = len(DIL_GROUPS)
    cb = Z_C // LANE

    def col(which):
        return lambda bb, h, g: (bb, cb + (g * 3 + which) * HEADS + h)

    row = lambda g: g.reshape(1, LANE).astype(F32)
    return pl.pallas_call(
        functools.partial(_dil_kernel, s=s),
        grid=(b, HEADS, ngroups),
        in_specs=[pl.BlockSpec((s, LANE), col(0)),
                  pl.BlockSpec((s, LANE), col(1)),
                  pl.BlockSpec((s, LANE), col(2)),
                  pl.BlockSpec((None, None, 2, DIL_STEPS, 2 * DIL_STEPS), lambda bb, h, g: (g, h, 0, 0, 0)),
                  pl.BlockSpec((1, LANE), lambda bb, h, g: (0, 0)),
                  pl.BlockSpec((1, LANE), lambda bb, h, g: (0, 0))],
        out_specs=pl.BlockSpec((s, LANE), lambda bb, h, g: (bb, h)),
        out_shape=jax.ShapeDtypeStruct((m, HEADS * LANE), BF),
        scratch_shapes=[pltpu.VMEM((s, LANE), F32), pltpu.VMEM((DIL_PAD + s, LANE), F32),
                        pltpu.VMEM((DIL_PAD + s, LANE), F32), pltpu.VMEM((ngroups, s, LANE), F32),
                        pltpu.VMEM((ngroups, s, LANE), F32)],
        compiler_params=_cparams(("parallel", "parallel", "arbitrary")),
        name="dilated_attention",
    )(z, z, z, bias_c, row(q_g), row(k_g))


CONV_PAD = 8
GDN_GROUP = 4
GDN_BASE = 8


def _gdn_kernel(par_ref, zq_ref, zk_ref, zv_ref, zg_ref, zab_ref, wq_ref, wk_ref, wv_ref, og_ref, o_ref,
                x_sc, q_sc, k_sc, v_sc, g_sc, b_sc, u_sc, w_sc, a_sc, st_sc, *, s):
    h = pl.program_id(1)
    c = GDN_CHUNK

    def conv_silu(z_ref, w_ref):
        x_sc[0:CONV_PAD, :] = jnp.zeros((CONV_PAD, LANE), F32)
        x_sc[CONV_PAD:, :] = z_ref[...]
        y = x_sc[CONV_PAD:, :] * w_ref[CONV_W - 1:CONV_W, :]
        for i in range(CONV_W - 1):
            off = CONV_PAD - (CONV_W - 1) + i
            y = y + x_sc[off:off + s, :] * w_ref[i:i + 1, :]
        return y * jax.nn.sigmoid(y)

    def l2(x):
        return x * lax.rsqrt(jnp.sum(x * x, axis=-1, keepdims=True) + EPS)

    q_sc[...] = l2(conv_silu(zq_ref, wq_ref)) * (LANE ** -0.5)
    k_sc[...] = l2(conv_silu(zk_ref, wk_ref))
    v_sc[...] = conv_silu(zv_ref, wv_ref)
    lane = lax.broadcasted_iota(jnp.int32, (1, LANE), 1)
    ab = zab_ref[...]
    a_col = jnp.sum(jnp.where(lane == h, ab, 0.0), axis=-1, keepdims=True)
    b_col = jnp.sum(jnp.where(lane == h + HEADS, ab, 0.0), axis=-1, keepdims=True)
    a_neg_exp = par_ref[0, h]
    dt_bias = par_ref[1, h]
    g_sc[...] = jnp.broadcast_to(a_neg_exp * jax.nn.softplus(a_col + dt_bias), (s, LANE))
    b_sc[...] = jnp.broadcast_to(jax.nn.sigmoid(b_col), (s, LANE))
    st_sc[...] = jnp.zeros(st_sc.shape, F32)

    gr = GDN_GROUP * c
    r_i = lax.broadcasted_iota(jnp.int32, (gr, gr), 0)
    c_i = lax.broadcasted_iota(jnp.int32, (gr, gr), 1)
    blk = {}
    size = GDN_BASE
    while size <= c:
        blk[size] = (r_i // size) == (c_i // size)
        size *= 2
    same = blk[c]
    tril = same & (r_i >= c_i)
    strict = same & (r_i > c_i)
    tril_b = tril.astype(BF)
    triu_b = (same & (r_i <= c_i)).astype(BF)
    eye = (r_i == c_i).astype(F32)

    def mm(a, b2):
        return jnp.dot(a.astype(BF), b2.astype(BF), preferred_element_type=F32)

    def mm_t(a, b2):
        return lax.dot_general(a.astype(BF), b2.astype(BF), (((1,), (1,)), ((), ())), preferred_element_type=F32)

    def split2(x):
        hi = x.astype(BF)
        return hi, (x - hi.astype(F32)).astype(BF)

    def split3(x):
        hi = x.astype(BF)
        r1 = x - hi.astype(F32)
        mid = r1.astype(BF)
        return hi, mid, (r1 - mid.astype(F32)).astype(BF)

    def mm_hi(a, b2):
        a_hi, a_lo = split2(a)
        b_hi, b_lo = split2(b2)
        d = functools.partial(jnp.dot, preferred_element_type=F32)
        return d(a_hi, b_hi) + (d(a_hi, b_lo) + d(a_lo, b_hi))

    def widen(x):
        return jnp.concatenate([x] * (gr // LANE), axis=1)

    def local_group(gi, carry):
        base = pl.multiple_of(gi * gr, gr)
        rows = pl.ds(base, gr)
        q = q_sc[rows, :]
        k = k_sc[rows, :]
        beta = b_sc[rows, :]
        parts = split3(g_sc[rows, :])
        gc = sum(jnp.dot(tril_b, part, preferred_element_type=F32) for part in parts)
        g_row = sum(lax.dot_general(widen(part), triu_b, (((0,), (0,)), ((), ())), preferred_element_type=F32)
                    for part in parts)
        decay = jnp.exp(jnp.where(tril, widen(gc) - g_row, NEG))
        kb = k * beta
        vb = v_sc[rows, :] * beta
        lower = jnp.where(strict, mm_t(kb, k) * decay, 0.0)
        neg = -jnp.where(blk[GDN_BASE], lower, 0.0)
        p1 = mm_hi(neg, neg)
        tmat = eye + neg
        tmat = tmat + mm_hi(tmat, p1)
        tmat = tmat + mm_hi(tmat, mm_hi(p1, p1))
        size = GDN_BASE
        while size < c:
            off = jnp.where(blk[2 * size] & jnp.logical_not(blk[size]), lower, 0.0)
            tmat = tmat - mm_hi(mm_hi(tmat, off), tmat)
            size *= 2
        eg = jnp.exp(gc)
        u_sc[rows, :] = mm(tmat, vb)
        w_sc[rows, :] = mm(tmat, kb * eg)
        intra = mm_t(q, k) * decay
        q_sc[rows, :] = q * eg
        for uu in range(GDN_GROUP):
            sl = slice(uu * c, (uu + 1) * c)
            crow = pl.ds(base + uu * c, c)
            g_last = gc[(uu + 1) * c - 1:(uu + 1) * c, :]
            a_sc[crow, 0:c] = intra[sl, sl]
            k_sc[crow, :] = k[sl] * jnp.exp(g_last - gc[sl])
            g_sc[crow, :] = jnp.broadcast_to(jnp.exp(g_last), (c, LANE))
        return carry


    def scan(n, carry):
        rows = pl.ds(pl.multiple_of(n * c, c), c)
        state = st_sc[...]
        v_new = u_sc[rows, :] - mm(w_sc[rows, :], state)
        o = mm(q_sc[rows, :], state) + mm(a_sc[rows, 0:c], v_new)
        decay_last = g_sc[pl.ds(pl.multiple_of(n * c, c), 1), :]
        st_sc[...] = state * decay_last + lax.dot_general(
            k_sc[rows, :].astype(BF), v_new.astype(BF), (((0,), (0,)), ((), ())), preferred_element_type=F32)
        u_sc[rows, :] = o
        return carry

    ngroups = s // gr
    local_group(0, 0)

    def pipelined(g, carry):
        local_group(g + 1, 0)
        for uu in range(GDN_GROUP):
            scan(g * GDN_GROUP + uu, 0)
        return carry

    lax.fori_loop(0, ngroups - 1, pipelined, 0)
    for uu in range(GDN_GROUP):
        scan((ngroups - 1) * GDN_GROUP + uu, 0)
    o = u_sc[...]
    ms = jnp.mean(o * o, axis=-1, keepdims=True)
    gate = zg_ref[...]
    o_ref[...] = (o * lax.rsqrt(ms + EPS) * og_ref[...] * (gate * jax.nn.sigmoid(gate))).astype(o_ref.dtype)


def mixer_d(z, s, conv_w, a_log, dt_bias, o_g):
    m = z.shape[0]
    b = m // s
    cb = Z_D // LANE
    par = jnp.stack([-jnp.exp(a_log.astype(F32)), dt_bias.astype(F32)], axis=0)
    cw = conv_w.astype(F32)

    def col(which):
        return lambda bb, h: (bb, cb + which * HEADS + h)

    return pl.pallas_call(
        functools.partial(_gdn_kernel, s=s),
        grid=(b, HEADS),
        in_specs=[pl.BlockSpec(memory_space=pltpu.SMEM),
                  pl.BlockSpec((s, LANE), col(0)),
                  pl.BlockSpec((s, LANE), col(1)),
                  pl.BlockSpec((s, LANE), col(2)),
                  pl.BlockSpec((s, LANE), col(3)),
                  pl.BlockSpec((s, LANE), lambda bb, h: (bb, cb + 4 * HEADS)),
                  pl.BlockSpec((CONV_W, LANE), lambda bb, h: (0, h)),
                  pl.BlockSpec((CONV_W, LANE), lambda bb, h: (0, HEADS + h)),
                  pl.BlockSpec((CONV_W, LANE), lambda bb, h: (0, 2 * HEADS + h)),
                  pl.BlockSpec((1, LANE), lambda bb, h: (0, 0))],
        out_specs=pl.BlockSpec((s, LANE), lambda bb, h: (bb, h)),
        out_shape=jax.ShapeDtypeStruct((m, HEADS * LANE), BF),
        scratch_shapes=[pltpu.VMEM((CONV_PAD + s, LANE), F32), pltpu.VMEM((s, LANE), F32),
                        pltpu.VMEM((s, LANE), F32), pltpu.VMEM((s, LANE), F32), pltpu.VMEM((s, LANE), F32),
                        pltpu.VMEM((s, LANE), F32), pltpu.VMEM((s, LANE), F32), pltpu.VMEM((s, LANE), F32),
                        pltpu.VMEM((s, LANE), F32), pltpu.VMEM((LANE, LANE), F32)],
        compiler_params=_cparams(("parallel", "arbitrary")),
        name="gated_deltanet",
    )(par, z, z, z, z, z, cw, cw, cw, o_g.reshape(1, LANE).astype(F32))


def _cast_kernel(x_ref, o_ref):
    o_ref[...] = x_ref[...].astype(o_ref.dtype)


def cast_bf16(w, bm=512):
    r, c = w.shape
    bm = min(bm, r)
    return pl.pallas_call(
        _cast_kernel,
        grid=(r // bm,),
        in_specs=[pl.BlockSpec((bm, c), lambda i: (i, 0))],
        out_specs=pl.BlockSpec((bm, c), lambda i: (i, 0)),
        out_shape=jax.ShapeDtypeStruct((r, c), BF),
        compiler_params=_cparams(("parallel",)),
        name="cast_bf16",
    )(w)


IN_COLS = DA_COLS + MLA_COLS + DIL_COLS + GDN_QKV + 2 * HEADS + HEADS * LANE


def _pack_w_in_kernel(w_ref, o_ref):
    rows = w_ref.shape[0]

    def put(dst, src, width):
        o_ref[:, dst:dst + width] = w_ref[:, src:src + width]

    def zero(dst, width):
        o_ref[:, dst:dst + width] = jnp.zeros((rows, width), BF)

    c0 = DA_COLS + MLA_COLS + DIL_COLS
    put(0, 0, DA_COLS + MLA_COLS)
    zero(Z_B + MLA_COLS, Z_C - Z_B - MLA_COLS)
    put(Z_C, DA_COLS + MLA_COLS, DIL_COLS)
    put(Z_D, c0, GDN_QKV)
    put(Z_D + GDN_QKV, c0 + GDN_QKV + 2 * HEADS, HEADS * LANE)
    put(Z_D + GDN_QKV + HEADS * LANE, c0 + GDN_QKV, 2 * HEADS)
    used = Z_D + GDN_QKV + HEADS * LANE + 2 * HEADS
    zero(used, Z_COLS - used)


def pack_w_in(w, bm=256):
    r = w.shape[0]
    return pl.pallas_call(
        _pack_w_in_kernel,
        grid=(r // bm,),
        in_specs=[pl.BlockSpec((bm, IN_COLS), lambda i: (i, 0))],
        out_specs=pl.BlockSpec((bm, Z_COLS), lambda i: (i, 0)),
        out_shape=jax.ShapeDtypeStruct((r, Z_COLS), BF),
        compiler_params=_cparams(("parallel",)),
        name="pack_w_in",
    )(w)


def _cast_stacked(w, bm=512):
    lead, c = w.shape[:-1], w.shape[-1]
    return cast_bf16(w.reshape(-1, c), bm).reshape(*lead, c)


def kernel(x, p, rel_bias, norm_mix_g, w_in, da_lambda, da_q_g, da_k_g, da_o_g, mla_w_uq, mla_w_ukv, mla_cq_g, mla_ckv_g, mla_qn_g, mla_kn_g, mla_qr_g, mla_kr_g, dil_q_g, dil_k_g, gdn_conv_w, gdn_a_log, gdn_dt_bias, gdn_o_g, w_bgate, b_bgate, w_branch, w_out, norm_ffn_g, w_ffn_in, w_ffn_out, norm_ple_g, w_ple, w_ple_gate):
    b, s, d = x.shape
    m = b * s
    bias_a = diff_bias_tiles(rel_bias, s)
    bias_c = dil_bias_tiles(rel_bias)
    cos_t, sin_t = rope_tables(s)
    w_in_p = pack_w_in(w_in.astype(BF).reshape(DEPTH * d, IN_COLS)).reshape(DEPTH, d, Z_COLS)
    w_branch_b = _cast_stacked(w_branch)
    w_out_b = _cast_stacked(w_out)
    w_ffn_out_b = _cast_stacked(w_ffn_out)
    w_ple_b = _cast_stacked(w_ple)
    w_ukv_b = _cast_stacked(mla_w_ukv)
    p_r = p.reshape(DEPTH, m, PLE_DIM)
    xr = x.reshape(m, d)
    for i in range(DEPTH):
        h = rmsnorm_rows(xr, norm_mix_g[i])
        z = matmul(h, w_in_p, F32, layer=i, name="in_proj")
        lam_init = 0.8 - 0.6 * math.exp(-0.3 * i)
        lq1, lk1, lq2, lk2 = (da_lambda[i, j].astype(F32) for j in range(4))
        lam = jnp.exp(jnp.sum(lq1 * lk1)) - jnp.exp(jnp.sum(lq2 * lk2)) + lam_init
        o_a = mixer_a(z, s, lam, lam_init, da_q_g[i], da_k_g[i], da_o_g[i], bias_a)
        o_b = mixer_b(z, s, mla_w_uq[i], w_ukv_b, mla_cq_g[i], mla_ckv_g[i], mla_qn_g[i], mla_kn_g[i],
                      mla_qr_g[i], mla_kr_g[i], cos_t, sin_t, layer=i)
        o_c = mixer_c(z, s, dil_q_g[i], dil_k_g[i], bias_c)
        o_d = mixer_d(z, s, gdn_conv_w[i], gdn_a_log[i], gdn_dt_bias[i], gdn_o_g[i])
        o_all = jnp.stack([o_a, o_b, o_c, o_d], axis=0)
        merged = gated_merge(h, w_bgate, b_bgate[i].reshape(N_BRANCH, 1, d), o_all, w_branch_b, i)
        xr = matmul_residual(xr, merged, w_out_b, bn=1024, layer=i, name="out_proj")
        hf = rmsnorm_rows(xr, norm_ffn_g[i])
        act = swiglu_in(hf, w_ffn_in, i)
        xr = matmul_residual(xr, act, w_ffn_out_b, bm=512, bn=512, layer=i, name="ffn_out")
        hp = rmsnorm_rows(xr, norm_ple_g[i])
        xr = ple_update(xr, p_r, w_ple_b, hp, w_ple_gate, i)
    return xr.reshape(b, s, d)
```

```python
import functools
import math

import numpy as np
import jax
import jax.numpy as jnp
from jax import lax
from jax.experimental import pallas as pl
from jax.experimental.pallas import tpu as pltpu

D_MODEL = 4096
DEPTH = 2
EPS = 1e-6
PLE_DIM = 256
HEADS = 8
N_BRANCH = 4
BRANCH_W = 1024
DA_QK_DIM = 64
MLA_Q_RANK = 1024
MLA_KV_RANK = 512
MLA_NOPE = 128
MLA_ROPE = 64
ROPE_THETA = 10000.0
DIL_GROUPS = ((128, 1), (512, 4), (2048, 16))
DIL_STEPS = 128
GDN_CHUNK = 64
CONV_W = 4
N_BUCKETS = 32
MAX_DIST = 2048
D_FF = 11008

DA_COLS = 3072
MLA_COLS = 1600
DIL_COLS = 9216
GDN_QKV = 3072

LANE = 128
Z_A = 0
Z_B = Z_A + DA_COLS
Z_C = Z_B + MLA_COLS + 64
Z_D = Z_C + DIL_COLS
Z_COLS = 18 * 1024

NEG = -1e30
ATT_T = 256
VMEM_LIMIT = 56 * 1024 * 1024
BF = jnp.bfloat16
F32 = jnp.float32


def _cparams(sem):
    return pltpu.CompilerParams(dimension_semantics=sem, vmem_limit_bytes=VMEM_LIMIT)


def _rmsnorm_kernel(x_ref, g_ref, o_ref):
    x = x_ref[...]
    ms = jnp.mean(x * x, axis=-1, keepdims=True)
    o_ref[...] = (x * lax.rsqrt(ms + EPS) * g_ref[...]).astype(o_ref.dtype)


def rmsnorm_rows(x, g, col_block=0, bm=512):
    m = x.shape[0]
    d = g.shape[-1]
    return pl.pallas_call(
        _rmsnorm_kernel,
        grid=(m // bm,),
        in_specs=[pl.BlockSpec((bm, d), lambda i: (i, col_block)),
                  pl.BlockSpec((1, d), lambda i: (0, 0))],
        out_specs=pl.BlockSpec((bm, d), lambda i: (i, 0)),
        out_shape=jax.ShapeDtypeStruct((m, d), BF),
        compiler_params=_cparams(("parallel",)),
        name="rmsnorm_rows",
    )(x, g.reshape(1, d))


def _mm_kernel(x_ref, w_ref, o_ref):
    o_ref[...] = jnp.dot(x_ref[...], w_ref[...], preferred_element_type=F32).astype(o_ref.dtype)


def _layer_spec(w, block, index_map, layer):
    if layer is None:
        return pl.BlockSpec(block, index_map)
    return pl.BlockSpec((None,) + block, lambda *g: (layer,) + index_map(*g))


def matmul(x, w, out_dtype, bm=1024, bn=1024, layer=None, name="matmul"):
    m, k = x.shape
    n = w.shape[-1]
    bn = min(bn, n)
    bm = min(bm, m)
    return pl.pallas_call(
        _mm_kernel,
        grid=(m // bm, n // bn),
        in_specs=[pl.BlockSpec((bm, k), lambda i, j: (i, 0)),
                  _layer_spec(w, (k, bn), lambda i, j: (0, j), layer)],
        out_specs=pl.BlockSpec((bm, bn), lambda i, j: (i, j)),
        out_shape=jax.ShapeDtypeStruct((m, n), out_dtype),
        compiler_params=_cparams(("parallel", "parallel")),
        name=name,
    )(x, w)


def _mm_residual_kernel(r_ref, x_ref, w_ref, o_ref, *acc, nk):
    part = jnp.dot(x_ref[...], w_ref[...], preferred_element_type=F32)
    if nk == 1:
        o_ref[...] = r_ref[...] + part
    else:
        acc_ref, = acc

        @pl.when(pl.program_id(2) == 0)
        def _():
            acc_ref[...] = r_ref[...]

        total = acc_ref[...] + part
        acc_ref[...] = total
        o_ref[...] = total


def matmul_residual(r, x, w, bm=1024, bn=512, bk=None, layer=None, name="matmul_residual"):
    m, kdim = x.shape
    n = w.shape[-1]
    bk = kdim if bk is None else bk
    bm = min(bm, m)
    nk = kdim // bk
    return pl.pallas_call(
        functools.partial(_mm_residual_kernel, nk=nk),
        grid=(m // bm, n // bn, nk),
        in_specs=[pl.BlockSpec((bm, bn), lambda i, j, k: (i, j)),
                  pl.BlockSpec((bm, bk), lambda i, j, k: (i, k)),
                  _layer_spec(w, (bk, bn), lambda i, j, k: (k, j), layer)],
        out_specs=pl.BlockSpec((bm, bn), lambda i, j, k: (i, j)),
        out_shape=jax.ShapeDtypeStruct((m, n), F32),
        scratch_shapes=[pltpu.VMEM((bm, bn), F32)] if nk > 1 else [],
        compiler_params=_cparams(("parallel", "parallel", "arbitrary")),
        name=name,
    )(r, x, w)


def _merge_kernel(h_ref, wg_ref, b_ref, o_ref, wb_ref, out_ref, acc_ref):
    n = pl.program_id(2)
    gate = jax.nn.sigmoid(jnp.dot(h_ref[...], wg_ref[...].astype(BF), preferred_element_type=F32) + b_ref[...])
    term = gate * jnp.dot(o_ref[...], wb_ref[...], preferred_element_type=F32)

    @pl.when(n == 0)
    def _():
        acc_ref[...] = jnp.zeros(acc_ref.shape, F32)

    total = acc_ref[...] + term
    acc_ref[...] = total
    out_ref[...] = total.astype(out_ref.dtype)


def gated_merge(h, w_bgate, b_bgate, o_all, w_branch, layer, bm=1024, bn=512):
    m, d = h.shape
    bm = min(bm, m)
    return pl.pallas_call(
        _merge_kernel,
        grid=(m // bm, d // bn, N_BRANCH),
        in_specs=[pl.BlockSpec((bm, d), lambda i, j, n: (i, 0)),
                  pl.BlockSpec((None, None, d, bn), lambda i, j, n: (layer, n, 0, j)),
                  pl.BlockSpec((None, 1, bn), lambda i, j, n: (n, 0, j)),
                  pl.BlockSpec((None, bm, BRANCH_W), lambda i, j, n: (n, i, 0)),
                  pl.BlockSpec((None, None, BRANCH_W, bn), lambda i, j, n: (layer, n, 0, j))],
        out_specs=pl.BlockSpec((bm, bn), lambda i, j, n: (i, j)),
        out_shape=jax.ShapeDtypeStruct((m, d), BF),
        scratch_shapes=[pltpu.VMEM((bm, bn), F32)],
        compiler_params=_cparams(("parallel", "parallel", "arbitrary")),
        name="gated_merge",
    )(h, w_bgate, b_bgate, o_all, w_branch)


def _swiglu_kernel(x_ref, wg_ref, wu_ref, o_ref):
    x = x_ref[...]
    g = jnp.dot(x, wg_ref[...].astype(BF), preferred_element_type=F32)
    u = jnp.dot(x, wu_ref[...].astype(BF), preferred_element_type=F32)
    o_ref[...] = (g * jax.nn.sigmoid(g) * u).astype(o_ref.dtype)


def swiglu_in(x, w, layer, bm=2048, bn=256):
    m, k = x.shape
    n = w.shape[-1] // 2
    bm = min(bm, m)
    nb = n // bn
    return pl.pallas_call(
        _swiglu_kernel,
        grid=(m // bm, nb),
        in_specs=[pl.BlockSpec((bm, k), lambda i, j: (i, 0)),
                  pl.BlockSpec((None, k, bn), lambda i, j: (layer, 0, j)),
                  pl.BlockSpec((None, k, bn), lambda i, j: (layer, 0, nb + j))],
        out_specs=pl.BlockSpec((bm, bn), lambda i, j: (i, j)),
        out_shape=jax.ShapeDtypeStruct((m, n), BF),
        compiler_params=_cparams(("parallel", "parallel")),
        name="swiglu_in",
    )(x, w, w)


def _ple_kernel(r_ref, p_ref, wp_ref, h_ref, wg_ref, o_ref):
    e = jnp.dot(p_ref[...].astype(BF), wp_ref[...], preferred_element_type=F32)
    g = jnp.dot(h_ref[...], wg_ref[...].astype(BF), preferred_element_type=F32)
    o_ref[...] = r_ref[...] + e * jax.nn.sigmoid(g)


def ple_update(r, p, w_ple, hp, w_gate, layer, bm=1024, bn=512):
    m, d = r.shape
    kp = p.shape[-1]
    bm = min(bm, m)
    return pl.pallas_call(
        _ple_kernel,
        grid=(m // bm, d // bn),
        in_specs=[pl.BlockSpec((bm, bn), lambda i, j: (i, j)),
                  pl.BlockSpec((None, bm, kp), lambda i, j: (layer, i, 0)),
                  pl.BlockSpec((None, kp, bn), lambda i, j: (layer, 0, j)),
                  pl.BlockSpec((bm, d), lambda i, j: (i, 0)),
                  pl.BlockSpec((None, d, bn), lambda i, j: (layer, 0, j))],
        out_specs=pl.BlockSpec((bm, bn), lambda i, j: (i, j)),
        out_shape=jax.ShapeDtypeStruct((m, d), F32),
        compiler_params=_cparams(("parallel", "parallel")),
        name="ple_update",
    )(r, p, w_ple, hp, w_gate)


def _static_buckets(dist):
    max_exact = N_BUCKETS // 2
    d = np.maximum(np.asarray(dist), 0)
    large = max_exact + (np.log(np.maximum(d, 1).astype(np.float32) / np.float32(max_exact))
                         / np.float32(math.log(MAX_DIST / max_exact))
                         * np.float32(N_BUCKETS - max_exact)).astype(np.int32)
    large = np.minimum(large, N_BUCKETS - 1)
    return np.where(d < max_exact, d, large).astype(np.int32)


def _toeplitz_tiles(f, t):
    hh, n = f.shape
    big = n + t - 1
    fpad = jnp.concatenate([jnp.full((hh, t - 1), NEG, f.dtype), f], axis=1)
    flat = jnp.tile(fpad, (1, t + 1))[:, :t * (big + 1)]
    w = flat.reshape(hh, t, big + 1)[:, :, :n]
    w = jnp.flip(w.reshape(hh, t, n // t, t), axis=3)
    return w.transpose(0, 2, 1, 3)


def _bias_tile_kernel(prev_ref, cur_ref, o_ref, *, t):
    row = jnp.concatenate([prev_ref[...], cur_ref[...]], axis=1)
    rolled = pltpu.roll(jnp.broadcast_to(row, (t, 2 * t)), 0, 1, stride=1, stride_axis=0)
    o_ref[...] = rolled[:, t:]


def diff_bias_tiles(rel_bias, s):
    t = min(ATT_T, s)
    nq = s // t
    f = jnp.take(rel_bias[:, :HEADS], _static_buckets(np.arange(s)), axis=0).T.astype(F32)
    fb = jnp.concatenate([jnp.full((HEADS, t), NEG, F32), f], axis=1).reshape(HEADS, nq + 1, 1, t)
    return pl.pallas_call(
        functools.partial(_bias_tile_kernel, t=t),
        grid=(HEADS, nq),
        in_specs=[pl.BlockSpec((None, None, 1, t), lambda h, d: (h, d, 0, 0)),
                  pl.BlockSpec((None, None, 1, t), lambda h, d: (h, d + 1, 0, 0))],
        out_specs=pl.BlockSpec((None, None, t, t), lambda h, d: (h, d, 0, 0)),
        out_shape=jax.ShapeDtypeStruct((HEADS, nq, t, t), F32),
        compiler_params=_cparams(("parallel", "parallel")),
        name="diff_bias_tiles",
    )(fb, fb)


def dil_bias_tiles(rel_bias):
    out = []
    for gi, (_, dil) in enumerate(DIL_GROUPS):
        lo = HEADS + gi * HEADS
        steps = np.arange(2 * DIL_STEPS)
        f = jnp.take(rel_bias[:, lo:lo + HEADS], _static_buckets(steps * dil), axis=0).T.astype(F32)
        f = jnp.where(steps[None, :] <= DIL_STEPS, f, NEG)
        tiles = _toeplitz_tiles(f, DIL_STEPS)
        general = jnp.concatenate([tiles[:, 1], tiles[:, 0]], axis=-1)
        first = jnp.concatenate([jnp.full_like(tiles[:, 1], NEG), tiles[:, 0]], axis=-1)
        out.append(jnp.stack([general, first], axis=1))
    return jnp.stack(out, axis=0)


def rope_tables(s):
    half = MLA_ROPE // 2
    inv = ROPE_THETA ** (-jnp.arange(half, dtype=F32) / half)
    ang = jnp.arange(s).astype(F32)[:, None] * inv[None, :]
    cos, sin = jnp.cos(ang), jnp.sin(ang)
    zero = jnp.zeros((s, LANE - MLA_ROPE), F32)
    return (jnp.concatenate([cos, cos, zero], axis=1), jnp.concatenate([-sin, sin, zero], axis=1))


def _rope_tile(t, c, s):
    half = MLA_ROPE // 2
    swapped = pltpu.roll(t, half, axis=1) + pltpu.roll(t, LANE - half, axis=1)
    return t * c + swapped * s


def _flash_kernel(lam_ref, qt_ref, k_ref, vt_ref, *rest, diff, t, scale, post_scale):
    if diff:
        bias_ref, g_ref, o_ref, m_sc, l_sc, acc_sc, s_sc = rest
    else:
        o_ref, m_sc, l_sc, acc_sc, s_sc = rest
    i = pl.program_id(2)
    m_sc[...] = jnp.full(m_sc.shape, NEG, F32)
    l_sc[...] = jnp.zeros(l_sc.shape, F32)
    acc_sc[...] = jnp.zeros(acc_sc.shape, F32)
    qts = [qt_ref[0], qt_ref[1]] if diff else [qt_ref[:, 0:t], qt_ref[:, t:2 * t]]

    def load_kv(j):
        start = pl.multiple_of(j * t, t)
        return k_ref[pl.ds(start, t), :], vt_ref[:, pl.ds(start, t)]

    ntiles = k_ref.shape[0] // t

    def scores_of(work):
        return [jnp.dot(kv[0], qts[c], preferred_element_type=F32) for c, kv, _, _ in work]

    def consume(work, scores):
        for (c, kv, bias, diagonal), s in zip(work, scores):
            if scale != 1.0:
                s = s * scale
            if bias is not None:
                s = bias + s
            elif diagonal:
                key_i = lax.broadcasted_iota(jnp.int32, (t, t), 0)
                qry_i = lax.broadcasted_iota(jnp.int32, (t, t), 1)
                s = jnp.where(key_i <= qry_i, s, NEG)
            m_prev = m_sc[c]
            m_new = jnp.maximum(m_prev, jnp.max(s, axis=0, keepdims=True))
            alpha = jnp.exp(m_prev - m_new)
            p = jnp.exp(s - m_new)
            l_sc[c] = alpha * l_sc[c] + jnp.sum(p, axis=0, keepdims=True)
            m_sc[c] = m_new
            acc_sc[c] = alpha * acc_sc[c] + jnp.dot(kv[1], p.astype(BF), preferred_element_type=F32)

    def step(work):
        consume(work, scores_of(work))

    def sweep(n, work):
        nq = n // 4

        def quad(jj):
            return work(4 * jj) + work(4 * jj + 1) + work(4 * jj + 2) + work(4 * jj + 3)

        def park(jj, slot):
            first = jnp.minimum(4 * jj, ntiles - 4)
            for uu in range(4):
                k_tile = load_kv(first + uu)[0]
                for c in range(2):
                    s_sc[slot, 2 * uu + c] = jnp.dot(k_tile, qts[c], preferred_element_type=F32)

        def consume_parked(jj, slot):
            items = quad(jj)
            consume(items, [s_sc[slot, idx] for idx in range(len(items))])

        @pl.when(nq > 0)
        def _():
            park(0, 0)

        def body(j2, carry):
            park(2 * j2 + 1, 1)
            consume_parked(2 * j2, 0)
            park(2 * j2 + 2, 0)
            consume_parked(2 * j2 + 1, 1)
            return carry

        lax.fori_loop(0, nq // 2, body, 0)

        @pl.when(nq % 2 == 1)
        def _():
            consume_parked(nq - 1, 0)

        done = nq * 4

        @pl.when(n % 4 >= 2)
        def _():
            step(work(done) + work(done + 1))

        @pl.when(n % 2 == 1)
        def _():
            step(work(n - 1))

    if diff:
        def work(j):
            kv = load_kv(j)
            bias = bias_ref[i - j]
            return [(0, kv, bias, False), (1, kv, bias, False)]

        sweep(i + 1, work)
        out_t = acc_sc[0] / l_sc[0] - lam_ref[0] * (acc_sc[1] / l_sc[1])
        o = out_t.T
        ms = jnp.mean(o * o, axis=-1, keepdims=True)
        o_ref[...] = (o * lax.rsqrt(ms + EPS) * g_ref[...] * post_scale).astype(o_ref.dtype)
    else:
        def work(j):
            kv = load_kv(j)
            return [(0, kv, None, False), (1, kv, None, False)]

        sweep(2 * i, work)
        kv = load_kv(2 * i)
        step([(0, kv, None, True), (1, kv, None, False)])
        step([(1, load_kv(2 * i + 1), None, True)])
        o_ref[0:t, :] = (acc_sc[0] / l_sc[0]).T.astype(o_ref.dtype)
        o_ref[t:2 * t, :] = (acc_sc[1] / l_sc[1]).T.astype(o_ref.dtype)


def flash_attention(q, k, vt, s, *, diff, dk, scale, bias=None, lam=None, gain=None, post_scale=1.0, name):
    m = k.shape[0]
    b = m // s
    t = min(ATT_T, s)
    lam = jnp.zeros((1,), F32) if lam is None else lam.reshape(1).astype(F32)
    if diff:
        nq = s // t
        q_spec = pl.BlockSpec((2, dk, t), lambda bb, h, i: (0, h, bb * nq + i))
        o_rows = t
    else:
        nq = s // (2 * t)
        q_spec = pl.BlockSpec((dk, 2 * t), lambda bb, h, i: (h, bb * nq + i))
        o_rows = 2 * t
    in_specs = [pl.BlockSpec(memory_space=pltpu.SMEM), q_spec,
                pl.BlockSpec((s, dk), lambda bb, h, i: (bb, h)),
                pl.BlockSpec((LANE, s), lambda bb, h, i: (h, bb))]
    args = [lam, q, k, vt]
    if diff:
        in_specs += [pl.BlockSpec((None, nq, t, t), lambda bb, h, i: (h, 0, 0, 0)),
                     pl.BlockSpec((1, LANE), lambda bb, h, i: (0, 0))]
        args += [bias, gain.reshape(1, LANE).astype(F32)]
    return pl.pallas_call(
        functools.partial(_flash_kernel, diff=diff, t=t, scale=scale, post_scale=post_scale),
        grid=(b, HEADS, nq),
        in_specs=in_specs,
        out_specs=pl.BlockSpec((o_rows, LANE), lambda bb, h, i: (bb * nq + i, h)),
        out_shape=jax.ShapeDtypeStruct((m, HEADS * LANE), BF),
        scratch_shapes=[pltpu.VMEM((2, 1, t), F32), pltpu.VMEM((2, 1, t), F32), pltpu.VMEM((2, LANE, t), F32),
                        pltpu.VMEM((2, 8, t, t), F32)],
        compiler_params=_cparams(("parallel", "parallel", "arbitrary")),
        name=name,
    )(*args)


def _prep_a_kernel(z_ref, qg_ref, kg_ref, qz_ref, kn_ref, vt_ref):
    lane = lax.broadcasted_iota(jnp.int32, (1, LANE), 1)
    lo = lane < DA_QK_DIM

    def norm_halves(x, g):
        x2 = x * x
        s_lo = jnp.sum(jnp.where(lo, x2, 0.0), axis=-1, keepdims=True)
        s_hi = jnp.sum(jnp.where(lo, 0.0, x2), axis=-1, keepdims=True)
        inv = lax.rsqrt(jnp.where(lo, s_lo, s_hi) * (1.0 / DA_QK_DIM) + EPS)
        return x * inv * g

    scale = DA_QK_DIM ** -0.5
    for h in range(HEADS):
        cq = slice(h * LANE, (h + 1) * LANE)
        ck = slice(HEADS * LANE + h * LANE, HEADS * LANE + (h + 1) * LANE)
        q = norm_halves(z_ref[:, cq], qg_ref[...]) * scale
        qt = q.T
        sub_lo = lax.broadcasted_iota(jnp.int32, (LANE, 1), 0) < DA_QK_DIM
        qz_ref[0, cq, :] = jnp.where(sub_lo, qt, 0.0).astype(BF)
        qz_ref[1, cq, :] = jnp.where(sub_lo, 0.0, qt).astype(BF)
        kn_ref[:, cq] = norm_halves(z_ref[:, ck], kg_ref[...]).astype(BF)
        cv = slice(2 * HEADS * LANE + h * LANE, 2 * HEADS * LANE + (h + 1) * LANE)
        vt_ref[cq, :] = z_ref[:, cv].T.astype(BF)


def prep_a(z, q_g, k_g, bm=256):
    m = z.shape[0]
    bm = min(bm, m)
    w = HEADS * LANE
    g2 = lambda g: jnp.concatenate([g, g]).reshape(1, LANE).astype(F32)
    return pl.pallas_call(
        _prep_a_kernel,
        grid=(m // bm,),
        in_specs=[pl.BlockSpec((bm, DA_COLS), lambda i: (i, Z_A // DA_COLS)),
                  pl.BlockSpec((1, LANE), lambda i: (0, 0)),
                  pl.BlockSpec((1, LANE), lambda i: (0, 0))],
        out_specs=[pl.BlockSpec((2, w, bm), lambda i: (0, 0, i)),
                   pl.BlockSpec((bm, w), lambda i: (i, 0)),
                   pl.BlockSpec((w, bm), lambda i: (0, i))],
        out_shape=[jax.ShapeDtypeStruct((2, w, m), BF), jax.ShapeDtypeStruct((m, w), BF),
                   jax.ShapeDtypeStruct((w, m), BF)],
        compiler_params=_cparams(("parallel",)),
        name="prep_a",
    )(z, g2(q_g), g2(k_g))


def mixer_a(z, s, lam, lam_init, q_g, k_g, o_g, bias_tiles):
    qz, kn, vt = prep_a(z, q_g, k_g)
    return flash_attention(qz, kn, vt, s, diff=True, dk=LANE, scale=1.0, bias=bias_tiles, lam=lam, gain=o_g,
                           post_scale=1.0 - lam_init, name="diff_attention")


def _prep_b_kernel(zq_ref, zkv_ref, zkr_ref, wq_ref, wkv_ref, cq_g, ckv_g, kr_g, qn_g, qr_g, kn_g, c_ref, s_ref,
                   qc_ref, kc_ref, vt_ref):
    def norm(x, g, width):
        ms = jnp.sum(x * x, axis=-1, keepdims=True) * (1.0 / width)
        return x * lax.rsqrt(ms + EPS) * g

    cos, sin = c_ref[...], s_ref[...]
    c_q = norm(zq_ref[...], cq_g[...], MLA_Q_RANK).astype(BF)
    c_kv = norm(zkv_ref[...], ckv_g[...], MLA_KV_RANK).astype(BF)
    q_up = jnp.dot(c_q, wq_ref[...], preferred_element_type=F32)
    kv_up = jnp.dot(c_kv, wkv_ref[...], preferred_element_type=F32)
    kr = _rope_tile(norm(zkr_ref[...], kr_g[...], MLA_ROPE), cos, sin).astype(BF)
    for h in range(HEADS):
        c0 = slice(2 * h * LANE, (2 * h + 1) * LANE)
        c1 = slice((2 * h + 1) * LANE, (2 * h + 2) * LANE)
        qc_ref[c0, :] = norm(q_up[:, c0], qn_g[...], MLA_NOPE).T.astype(BF)
        qc_ref[c1, :] = _rope_tile(norm(q_up[:, c1], qr_g[...], MLA_ROPE), cos, sin).T.astype(BF)
        kc_ref[:, c0] = norm(kv_up[:, c0], kn_g[...], MLA_NOPE).astype(BF)
        kc_ref[:, c1] = kr
        vt_ref[h * LANE:(h + 1) * LANE, :] = kv_up[:, c1].T.astype(BF)


def prep_b(z, w_uq, w_ukv, layer, cq_g, ckv_g, kr_g, qn_g, qr_g, kn_g, cos_t, sin_t, s, bm=256):
    m = z.shape[0]
    bm = min(bm, s)
    nsb = s // bm
    w2 = 2 * HEADS * LANE
    row = lambda g: g.reshape(1, -1).astype(F32)
    pad = lambda g: jnp.concatenate([g, jnp.zeros((LANE - MLA_ROPE,), g.dtype)])
    const = lambda i: (0, 0)
    return pl.pallas_call(
        _prep_b_kernel,
        grid=(m // bm,),
        in_specs=[pl.BlockSpec((bm, MLA_Q_RANK), lambda i: (i, Z_B // MLA_Q_RANK)),
                  pl.BlockSpec((bm, MLA_KV_RANK), lambda i: (i, (Z_B + MLA_Q_RANK) // MLA_KV_RANK)),
                  pl.BlockSpec((bm, LANE), lambda i: (i, (Z_B + MLA_Q_RANK + MLA_KV_RANK) // LANE)),
                  pl.BlockSpec((MLA_Q_RANK, w2), const),
                  pl.BlockSpec((None, MLA_KV_RANK, w2), lambda i: (layer, 0, 0)),
                  pl.BlockSpec((1, MLA_Q_RANK), const),
                  pl.BlockSpec((1, MLA_KV_RANK), const),
                  pl.BlockSpec((1, LANE), const),
                  pl.BlockSpec((1, LANE), const),
                  pl.BlockSpec((1, LANE), const),
                  pl.BlockSpec((1, LANE), const),
                  pl.BlockSpec((bm, LANE), lambda i: (i % nsb, 0)),
                  pl.BlockSpec((bm, LANE), lambda i: (i % nsb, 0))],
        out_specs=[pl.BlockSpec((w2, bm), lambda i: (0, i)),
                   pl.BlockSpec((bm, w2), lambda i: (i, 0)),
                   pl.BlockSpec((HEADS * LANE, bm), lambda i: (0, i))],
        out_shape=[jax.ShapeDtypeStruct((w2, m), BF), jax.ShapeDtypeStruct((m, w2), BF),
                   jax.ShapeDtypeStruct((HEADS * LANE, m), BF)],
        compiler_params=_cparams(("parallel",)),
        name="prep_b",
    )(z, z, z, w_uq, w_ukv, row(cq_g), row(ckv_g), row(pad(kr_g)), row(qn_g), row(pad(qr_g)), row(kn_g),
      cos_t, sin_t)


def pack_w_uq(w):
    w = w.reshape(MLA_Q_RANK, HEADS, MLA_NOPE + MLA_ROPE)
    w = jnp.pad(w, ((0, 0), (0, 0), (0, 2 * LANE - MLA_NOPE - MLA_ROPE)))
    return w.reshape(MLA_Q_RANK, HEADS * 2 * LANE).astype(BF)


def mixer_b(z, s, w_uq, w_ukv, cq_g, ckv_g, qn_g, kn_g, qr_g, kr_g, cos_t, sin_t, layer=0):
    qc, kc, vt = prep_b(z, pack_w_uq(w_uq), w_ukv, layer, cq_g, ckv_g, kr_g, qn_g, qr_g, kn_g, cos_t, sin_t, s)
    return flash_attention(qc, kc, vt, s, diff=False, dk=2 * LANE, scale=(MLA_NOPE + MLA_ROPE) ** -0.5,
                           name="mla_attention")


DIL_PAD = DIL_STEPS * max(d for _, d in DIL_GROUPS)


def _dil_kernel(zq_ref, zk_ref, zv_ref, bias_ref, qg_ref, kg_ref, o_ref, q_sc, k_sc, v_sc, og_sc, lse_sc, *, s):
    g = pl.program_id(2)
    ngroups = len(DIL_GROUPS)

    def norm(x, gain):
        ms = jnp.mean(x * x, axis=-1, keepdims=True)
        return x * lax.rsqrt(ms + EPS) * gain

    zeros = jnp.zeros((DIL_PAD, LANE), F32)
    k_sc[0:DIL_PAD, :] = zeros
    v_sc[0:DIL_PAD, :] = zeros
    q_sc[...] = norm(zq_ref[...], qg_ref[...])
    k_sc[DIL_PAD:, :] = norm(zk_ref[...], kg_ref[...])
    v_sc[DIL_PAD:, :] = zv_ref[...]
    scale = LANE ** -0.5

    def group(gi, dil):
        nsub = s // DIL_STEPS

        def body(tt, carry):
            c = tt % dil
            n = tt // dil
            q0 = c + dil * DIL_STEPS * n
            rows = pl.ds(q0, DIL_STEPS, stride=dil) if dil > 1 else pl.ds(q0, DIL_STEPS)
            k0 = q0 + DIL_PAD - dil * DIL_STEPS
            band = pl.ds(k0, 2 * DIL_STEPS, stride=dil) if dil > 1 else pl.ds(k0, 2 * DIL_STEPS)
            qs = q_sc[rows, :].astype(BF)
            ks = k_sc[band, :].astype(BF)
            vs = v_sc[band, :].astype(BF)
            first = jnp.where(n == 0, 1, 0)
            logits = lax.dot_general(qs, ks, (((1,), (1,)), ((), ())), preferred_element_type=F32) * scale
            logits = logits + bias_ref[first]
            mx = jnp.max(logits, axis=-1, keepdims=True)
            e = jnp.exp(logits - mx)
            den = jnp.sum(e, axis=-1, keepdims=True)
            o = jnp.dot((e / den).astype(BF), vs, preferred_element_type=F32)
            og_sc[gi, rows, :] = o
            lse_sc[gi, rows, :] = jnp.broadcast_to(mx + jnp.log(den), (DIL_STEPS, LANE))
            return carry

        lax.fori_loop(0, nsub, body, 0, unroll=8)

    for gi, (_, dil) in enumerate(DIL_GROUPS):
        pl.when(g == gi)(functools.partial(group, gi, dil))

    @pl.when(g == ngroups - 1)
    def _():
        lses = [lse_sc[gi] for gi in range(ngroups)]
        mx = functools.reduce(jnp.maximum, lses)
        ws = [jnp.exp(l - mx) for l in lses]
        tot = functools.reduce(lambda a, b2: a + b2, ws)
        acc = ws[0] * og_sc[0]
        for gi in range(1, ngroups):
            acc = acc + ws[gi] * og_sc[gi]
        o_ref[...] = (acc / tot).astype(o_ref.dtype)


def mixer_c(z, s, q_g, k_g, bias_c):
    m = z.shape[0]
    b = m // s
    ngroups = len(DIL_GROUPS)
    cb = Z_C // LANE

    def col(which):
        return lambda bb, h, g: (bb, cb + (g * 3 + which) * HEADS + h)

    row = lambda g: g.reshape(1, LANE).astype(F32)
    return pl.pallas_call(
        functools.partial(_dil_kernel, s=s),
        grid=(b, HEADS, ngroups),
        in_specs=[pl.BlockSpec((s, LANE), col(0)),
                  pl.BlockSpec((s, LANE), col(1)),
                  pl.BlockSpec((s, LANE), col(2)),
                  pl.BlockSpec((None, None, 2, DIL_STEPS, 2 * DIL_STEPS), lambda bb, h, g: (g, h, 0, 0, 0)),
                  pl.BlockSpec((1, LANE), lambda bb, h, g: (0, 0)),
                  pl.BlockSpec((1, LANE), lambda bb, h, g: (0, 0))],
        out_specs=pl.BlockSpec((s, LANE), lambda bb, h, g: (bb, h)),
        out_shape=jax.ShapeDtypeStruct((m, HEADS * LANE), BF),
        scratch_shapes=[pltpu.VMEM((s, LANE), F32), pltpu.VMEM((DIL_PAD + s, LANE), F32),
                        pltpu.VMEM((DIL_PAD + s, LANE), F32), pltpu.VMEM((ngroups, s, LANE), F32),
                        pltpu.VMEM((ngroups, s, LANE), F32)],
        compiler_params=_cparams(("parallel", "parallel", "arbitrary")),
        name="dilated_attention",
    )(z, z, z, bias_c, row(q_g), row(k_g))


CONV_PAD = 8
GDN_GROUP = 4
GDN_BASE = 8


def _gdn_kernel(par_ref, zq_ref, zk_ref, zv_ref, zg_ref, zab_ref, wq_ref, wk_ref, wv_ref, og_ref, o_ref,
                x_sc, q_sc, k_sc, v_sc, g_sc, b_sc, u_sc, w_sc, a_sc, st_sc, *, s):
    h = pl.program_id(1)
    c = GDN_CHUNK

    def conv_silu(z_ref, w_ref):
        x_sc[0:CONV_PAD, :] = jnp.zeros((CONV_PAD, LANE), F32)
        x_sc[CONV_PAD:, :] = z_ref[...]
        y = x_sc[CONV_PAD:, :] * w_ref[CONV_W - 1:CONV_W, :]
        for i in range(CONV_W - 1):
            off = CONV_PAD - (CONV_W - 1) + i
            y = y + x_sc[off:off + s, :] * w_ref[i:i + 1, :]
        return y * jax.nn.sigmoid(y)

    def l2(x):
        return x * lax.rsqrt(jnp.sum(x * x, axis=-1, keepdims=True) + EPS)

    q_sc[...] = l2(conv_silu(zq_ref, wq_ref)) * (LANE ** -0.5)
    k_sc[...] = l2(conv_silu(zk_ref, wk_ref))
    v_sc[...] = conv_silu(zv_ref, wv_ref)
    lane = lax.broadcasted_iota(jnp.int32, (1, LANE), 1)
    ab = zab_ref[...]
    a_col = jnp.sum(jnp.where(lane == h, ab, 0.0), axis=-1, keepdims=True)
    b_col = jnp.sum(jnp.where(lane == h + HEADS, ab, 0.0), axis=-1, keepdims=True)
    a_neg_exp = par_ref[0, h]
    dt_bias = par_ref[1, h]
    g_sc[...] = jnp.broadcast_to(a_neg_exp * jax.nn.softplus(a_col + dt_bias), (s, LANE))
    b_sc[...] = jnp.broadcast_to(jax.nn.sigmoid(b_col), (s, LANE))
    st_sc[...] = jnp.zeros(st_sc.shape, F32)

    gr = GDN_GROUP * c
    r_i = lax.broadcasted_iota(jnp.int32, (gr, gr), 0)
    c_i = lax.broadcasted_iota(jnp.int32, (gr, gr), 1)
    blk = {}
    size = GDN_BASE
    while size <= c:
        blk[size] = (r_i // size) == (c_i // size)
        size *= 2
    same = blk[c]
    tril = same & (r_i >= c_i)
    strict = same & (r_i > c_i)
    tril_b = tril.astype(BF)
    triu_b = (same & (r_i <= c_i)).astype(BF)
    eye = (r_i == c_i).astype(F32)

    def mm(a, b2):
        return jnp.dot(a.astype(BF), b2.astype(BF), preferred_element_type=F32)

    def mm_t(a, b2):
        return lax.dot_general(a.astype(BF), b2.astype(BF), (((1,), (1,)), ((), ())), preferred_element_type=F32)

    def split2(x):
        hi = x.astype(BF)
        return hi, (x - hi.astype(F32)).astype(BF)

    def split3(x):
        hi = x.astype(BF)
        r1 = x - hi.astype(F32)
        mid = r1.astype(BF)
        return hi, mid, (r1 - mid.astype(F32)).astype(BF)

    def mm_hi(a, b2):
        a_hi, a_lo = split2(a)
        b_hi, b_lo = split2(b2)
        d = functools.partial(jnp.dot, preferred_element_type=F32)
        return d(a_hi, b_hi) + (d(a_hi, b_lo) + d(a_lo, b_hi))

    def widen(x):
        return jnp.concatenate([x] * (gr // LANE), axis=1)

    def local_group(gi, carry):
        base = pl.multiple_of(gi * gr, gr)
        rows = pl.ds(base, gr)
        q = q_sc[rows, :]
        k = k_sc[rows, :]
        beta = b_sc[rows, :]
        parts = split3(g_sc[rows, :])
        gc = sum(jnp.dot(tril_b, part, preferred_element_type=F32) for part in parts)
        g_row = sum(lax.dot_general(widen(part), triu_b, (((0,), (0,)), ((), ())), preferred_element_type=F32)
                    for part in parts)
        decay = jnp.exp(jnp.where(tril, widen(gc) - g_row, NEG))
        kb = k * beta
        vb = v_sc[rows, :] * beta
        lower = jnp.where(strict, mm_t(kb, k) * decay, 0.0)
        neg = -jnp.where(blk[GDN_BASE], lower, 0.0)
        p1 = mm_hi(neg, neg)
        tmat = eye + neg
        tmat = tmat + mm_hi(tmat, p1)
        tmat = tmat + mm_hi(tmat, mm_hi(p1, p1))
        size = GDN_BASE
        while size < c:
            off = jnp.where(blk[2 * size] & jnp.logical_not(blk[size]), lower, 0.0)
            tmat = tmat - mm_hi(mm_hi(tmat, off), tmat)
            size *= 2
        eg = jnp.exp(gc)
        u_sc[rows, :] = mm(tmat, vb)
        w_sc[rows, :] = mm(tmat, kb * eg)
        intra = mm_t(q, k) * decay
        q_sc[rows, :] = q * eg
        for uu in range(GDN_GROUP):
            sl = slice(uu * c, (uu + 1) * c)
            crow = pl.ds(base + uu * c, c)
            g_last = gc[(uu + 1) * c - 1:(uu + 1) * c, :]
            a_sc[crow, 0:c] = intra[sl, sl]
            k_sc[crow, :] = k[sl] * jnp.exp(g_last - gc[sl])
            g_sc[crow, :] = jnp.broadcast_to(jnp.exp(g_last), (c, LANE))
        return carry


    def scan(n, carry):
        rows = pl.ds(pl.multiple_of(n * c, c), c)
        state = st_sc[...]
        v_new = u_sc[rows, :] - mm(w_sc[rows, :], state)
        o = mm(q_sc[rows, :], state) + mm(a_sc[rows, 0:c], v_new)
        decay_last = g_sc[pl.ds(pl.multiple_of(n * c, c), 1), :]
        st_sc[...] = state * decay_last + lax.dot_general(
            k_sc[rows, :].astype(BF), v_new.astype(BF), (((0,), (0,)), ((), ())), preferred_element_type=F32)
        u_sc[rows, :] = o
        return carry

    ngroups = s // gr
    local_group(0, 0)

    def pipelined(g, carry):
        local_group(g + 1, 0)
        for uu in range(GDN_GROUP):
            scan(g * GDN_GROUP + uu, 0)
        return carry

    lax.fori_loop(0, ngroups - 1, pipelined, 0)
    for uu in range(GDN_GROUP):
        scan((ngroups - 1) * GDN_GROUP + uu, 0)
    o = u_sc[...]
    ms = jnp.mean(o * o, axis=-1, keepdims=True)
    gate = zg_ref[...]
    o_ref[...] = (o * lax.rsqrt(ms + EPS) * og_ref[...] * (gate * jax.nn.sigmoid(gate))).astype(o_ref.dtype)


def mixer_d(z, s, conv_w, a_log, dt_bias, o_g):
    m = z.shape[0]
    b = m // s
    cb = Z_D // LANE
    par = jnp.stack([-jnp.exp(a_log.astype(F32)), dt_bias.astype(F32)], axis=0)
    cw = conv_w.astype(F32)

    def col(which):
        return lambda bb, h: (bb, cb + which * HEADS + h)

    return pl.pallas_call(
        functools.partial(_gdn_kernel, s=s),
        grid=(b, HEADS),
        in_specs=[pl.BlockSpec(memory_space=pltpu.SMEM),
                  pl.BlockSpec((s, LANE), col(0)),
                  pl.BlockSpec((s, LANE), col(1)),
                  pl.BlockSpec((s, LANE), col(2)),
                  pl.BlockSpec((s, LANE), col(3)),
                  pl.BlockSpec((s, LANE), lambda bb, h: (bb, cb + 4 * HEADS)),
                  pl.BlockSpec((CONV_W, LANE), lambda bb, h: (0, h)),
                  pl.BlockSpec((CONV_W, LANE), lambda bb, h: (0, HEADS + h)),
                  pl.BlockSpec((CONV_W, LANE), lambda bb, h: (0, 2 * HEADS + h)),
                  pl.BlockSpec((1, LANE), lambda bb, h: (0, 0))],
        out_specs=pl.BlockSpec((s, LANE), lambda bb, h: (bb, h)),
        out_shape=jax.ShapeDtypeStruct((m, HEADS * LANE), BF),
        scratch_shapes=[pltpu.VMEM((CONV_PAD + s, LANE), F32), pltpu.VMEM((s, LANE), F32),
                        pltpu.VMEM((s, LANE), F32), pltpu.VMEM((s, LANE), F32), pltpu.VMEM((s, LANE), F32),
                        pltpu.VMEM((s, LANE), F32), pltpu.VMEM((s, LANE), F32), pltpu.VMEM((s, LANE), F32),
                        pltpu.VMEM((s, LANE), F32), pltpu.VMEM((LANE, LANE), F32)],
        compiler_params=_cparams(("parallel", "arbitrary")),
        name="gated_deltanet",
    )(par, z, z, z, z, z, cw, cw, cw, o_g.reshape(1, LANE).astype(F32))


def _cast_kernel(x_ref, o_ref):
    o_ref[...] = x_ref[...].astype(o_ref.dtype)


def cast_bf16(w, bm=512):
    r, c = w.shape
    bm = min(bm, r)
    return pl.pallas_call(
        _cast_kernel,
        grid=(r // bm,),
        in_specs=[pl.BlockSpec((bm, c), lambda i: (i, 0))],
        out_specs=pl.BlockSpec((bm, c), lambda i: (i, 0)),
        out_shape=jax.ShapeDtypeStruct((r, c), BF),
        compiler_params=_cparams(("parallel",)),
        name="cast_bf16",
    )(w)


IN_COLS = DA_COLS + MLA_COLS + DIL_COLS + GDN_QKV + 2 * HEADS + HEADS * LANE


def _pack_w_in_kernel(w_ref, o_ref):
    rows = w_ref.shape[0]

    def put(dst, src, width):
        o_ref[:, dst:dst + width] = w_ref[:, src:src + width]

    def zero(dst, width):
        o_ref[:, dst:dst + width] = jnp.zeros((rows, width), BF)

    c0 = DA_COLS + MLA_COLS + DIL_COLS
    put(0, 0, DA_COLS + MLA_COLS)
    zero(Z_B + MLA_COLS, Z_C - Z_B - MLA_COLS)
    put(Z_C, DA_COLS + MLA_COLS, DIL_COLS)
    put(Z_D, c0, GDN_QKV)
    put(Z_D + GDN_QKV, c0 + GDN_QKV + 2 * HEADS, HEADS * LANE)
    put(Z_D + GDN_QKV + HEADS * LANE, c0 + GDN_QKV, 2 * HEADS)
    used = Z_D + GDN_QKV + HEADS * LANE + 2 * HEADS
    zero(used, Z_COLS - used)


def pack_w_in(w, bm=256):
    r = w.shape[0]
    return pl.pallas_call(
        _pack_w_in_kernel,
        grid=(r // bm,),
        in_specs=[pl.BlockSpec((bm, IN_COLS), lambda i: (i, 0))],
        out_specs=pl.BlockSpec((bm, Z_COLS), lambda i: (i, 0)),
        out_shape=jax.ShapeDtypeStruct((r, Z_COLS), BF),
        compiler_params=_cparams(("parallel",)),
        name="pack_w_in",
    )(w)


def _cast_stacked(w, bm=512):
    lead, c = w.shape[:-1], w.shape[-1]
    return cast_bf16(w.reshape(-1, c), bm).reshape(*lead, c)


def kernel(x, p, rel_bias, norm_mix_g, w_in, da_lambda, da_q_g, da_k_g, da_o_g, mla_w_uq, mla_w_ukv, mla_cq_g, mla_ckv_g, mla_qn_g, mla_kn_g, mla_qr_g, mla_kr_g, dil_q_g, dil_k_g, gdn_conv_w, gdn_a_log, gdn_dt_bias, gdn_o_g, w_bgate, b_bgate, w_branch, w_out, norm_ffn_g, w_ffn_in, w_ffn_out, norm_ple_g, w_ple, w_ple_gate):
    b, s, d = x.shape
    m = b * s
    bias_a = diff_bias_tiles(rel_bias, s)
    bias_c = dil_bias_tiles(rel_bias)
    cos_t, sin_t = rope_tables(s)
    w_in_p = pack_w_in(w_in.astype(BF).reshape(DEPTH * d, IN_COLS)).reshape(DEPTH, d, Z_COLS)
    w_branch_b = _cast_stacked(w_branch)
    w_out_b = _cast_stacked(w_out)
    w_ffn_out_b = _cast_stacked(w_ffn_out)
    w_ple_b = _cast_stacked(w_ple)
    w_ukv_b = _cast_stacked(mla_w_ukv)
    p_r = p.reshape(DEPTH, m, PLE_DIM)
    xr = x.reshape(m, d)
    for i in range(DEPTH):
        h = rmsnorm_rows(xr, norm_mix_g[i])
        z = matmul(h, w_in_p, F32, layer=i, name="in_proj")
        lam_init = 0.8 - 0.6 * math.exp(-0.3 * i)
        lq1, lk1, lq2, lk2 = (da_lambda[i, j].astype(F32) for j in range(4))
        lam = jnp.exp(jnp.sum(lq1 * lk1)) - jnp.exp(jnp.sum(lq2 * lk2)) + lam_init
        o_a = mixer_a(z, s, lam, lam_init, da_q_g[i], da_k_g[i], da_o_g[i], bias_a)
        o_b = mixer_b(z, s, mla_w_uq[i], w_ukv_b, mla_cq_g[i], mla_ckv_g[i], mla_qn_g[i], mla_kn_g[i],
                      mla_qr_g[i], mla_kr_g[i], cos_t, sin_t, layer=i)
        o_c = mixer_c(z, s, dil_q_g[i], dil_k_g[i], bias_c)
        o_d = mixer_d(z, s, gdn_conv_w[i], gdn_a_log[i], gdn_dt_bias[i], gdn_o_g[i])
        o_all = jnp.stack([o_a, o_b, o_c, o_d], axis=0)
        merged = gated_merge(h, w_bgate, b_bgate[i].reshape(N_BRANCH, 1, d), o_all, w_branch_b, i)
        xr = matmul_residual(xr, merged, w_out_b, bn=1024, layer=i, name="out_proj")
        hf = rmsnorm_rows(xr, norm_ffn_g[i])
        act = swiglu_in(hf, w_ffn_in, i)
        xr = matmul_residual(xr, act, w_ffn_out_b, bm=512, bn=512, layer=i, name="ffn_out")
        hp = rmsnorm_rows(xr, norm_ple_g[i])
        xr = ple_update(xr, p_r, w_ple_b, hp, w_ple_gate, i)
    return xr.reshape(b, s, d)
```

```python
import functools
import math

import numpy as np
import jax
import jax.numpy as jnp
from jax import lax
from jax.experimental import pallas as pl
from jax.experimental.pallas import tpu as pltpu

DEPTH = 2
EPS = 1e-6
PLE_DIM = 256
HEADS = 8
N_BRANCH = 4
BRANCH_W = 1024
DA_QK_DIM = 64
MLA_Q_RANK = 1024
MLA_KV_RANK = 512
MLA_NOPE = 128
MLA_ROPE = 64
ROPE_THETA = 10000.0
DIL_GROUPS = ((128, 1), (512, 4), (2048, 16))
DIL_STEPS = 128
GDN_CHUNK = 64
CONV_W = 4
N_BUCKETS = 32
MAX_DIST = 2048

DA_COLS = 3072
MLA_COLS = 1600
DIL_COLS = 9216
GDN_QKV = 3072

LANE = 128
Z_A = 0
Z_B = Z_A + DA_COLS
Z_C = Z_B + MLA_COLS + 64
Z_D = Z_C + DIL_COLS
Z_COLS = 18 * 1024

NEG = -1e30
ATT_T = 256
FLASH_GROUP = 2
VMEM_LIMIT = 56 * 1024 * 1024
BF = jnp.bfloat16
F32 = jnp.float32


def _cparams(sem):
    return pltpu.CompilerParams(dimension_semantics=sem, vmem_limit_bytes=VMEM_LIMIT)


def _rmsnorm_kernel(x_ref, g_ref, o_ref):
    x = x_ref[...]
    ms = jnp.mean(x * x, axis=-1, keepdims=True)
    o_ref[...] = (x * lax.rsqrt(ms + EPS) * g_ref[...]).astype(o_ref.dtype)


def rmsnorm_rows(x, g, bm=512):
    m = x.shape[0]
    d = g.shape[-1]
    return pl.pallas_call(
        _rmsnorm_kernel,
        grid=(m // bm,),
        in_specs=[pl.BlockSpec((bm, d), lambda i: (i, 0)),
                  pl.BlockSpec((1, d), lambda i: (0, 0))],
        out_specs=pl.BlockSpec((bm, d), lambda i: (i, 0)),
        out_shape=jax.ShapeDtypeStruct((m, d), BF),
        compiler_params=_cparams(("parallel",)),
        name="rmsnorm_rows",
    )(x, g.reshape(1, d))


def _mm_kernel(x_ref, w_ref, o_ref):
    o_ref[...] = jnp.dot(x_ref[...], w_ref[...], preferred_element_type=F32).astype(o_ref.dtype)


def _layer_spec(w, block, index_map, layer):
    if layer is None:
        return pl.BlockSpec(block, index_map)
    return pl.BlockSpec((None,) + block, lambda *g: (layer,) + index_map(*g))


def matmul(x, w, out_dtype, bm=1024, bn=1024, layer=None, name="matmul"):
    m, k = x.shape
    n = w.shape[-1]
    bn = min(bn, n)
    bm = min(bm, m)
    return pl.pallas_call(
        _mm_kernel,
        grid=(m // bm, n // bn),
        in_specs=[pl.BlockSpec((bm, k), lambda i, j: (i, 0)),
                  _layer_spec(w, (k, bn), lambda i, j: (0, j), layer)],
        out_specs=pl.BlockSpec((bm, bn), lambda i, j: (i, j)),
        out_shape=jax.ShapeDtypeStruct((m, n), out_dtype),
        compiler_params=_cparams(("parallel", "parallel")),
        name=name,
    )(x, w)


def _mm_residual_kernel(r_ref, x_ref, w_ref, o_ref, *acc, nk):
    part = jnp.dot(x_ref[...], w_ref[...], preferred_element_type=F32)
    if nk == 1:
        o_ref[...] = r_ref[...] + part
    else:
        acc_ref, = acc

        @pl.when(pl.program_id(2) == 0)
        def _():
            acc_ref[...] = r_ref[...]

        total = acc_ref[...] + part
        acc_ref[...] = total
        o_ref[...] = total


def matmul_residual(r, x, w, bm=1024, bn=512, bk=None, layer=None, name="matmul_residual"):
    m, kdim = x.shape
    n = w.shape[-1]
    bk = kdim if bk is None else bk
    bm = min(bm, m)
    nk = kdim // bk
    return pl.pallas_call(
        functools.partial(_mm_residual_kernel, nk=nk),
        grid=(m // bm, n // bn, nk),
        in_specs=[pl.BlockSpec((bm, bn), lambda i, j, k: (i, j)),
                  pl.BlockSpec((bm, bk), lambda i, j, k: (i, k)),
                  _layer_spec(w, (bk, bn), lambda i, j, k: (k, j), layer)],
        out_specs=pl.BlockSpec((bm, bn), lambda i, j, k: (i, j)),
        out_shape=jax.ShapeDtypeStruct((m, n), F32),
        scratch_shapes=[pltpu.VMEM((bm, bn), F32)] if nk > 1 else [],
        compiler_params=_cparams(("parallel", "parallel", "arbitrary")),
        name=name,
    )(r, x, w)


def _merge_kernel(h_ref, wg_ref, b_ref, o_ref, wb_ref, out_ref, acc_ref):
    n = pl.program_id(2)
    gate = jax.nn.sigmoid(jnp.dot(h_ref[...], wg_ref[...].astype(BF), preferred_element_type=F32) + b_ref[...])
    term = gate * jnp.dot(o_ref[...], wb_ref[...], preferred_element_type=F32)

    @pl.when(n == 0)
    def _():
        acc_ref[...] = jnp.zeros(acc_ref.shape, F32)

    total = acc_ref[...] + term
    acc_ref[...] = total
    out_ref[...] = total.astype(out_ref.dtype)


def gated_merge(h, w_bgate, b_bgate, o_all, w_branch, layer, bm=1024, bn=512):
    m, d = h.shape
    bm = min(bm, m)
    return pl.pallas_call(
        _merge_kernel,
        grid=(m // bm, d // bn, N_BRANCH),
        in_specs=[pl.BlockSpec((bm, d), lambda i, j, n: (i, 0)),
                  pl.BlockSpec((None, None, d, bn), lambda i, j, n: (layer, n, 0, j)),
                  pl.BlockSpec((None, 1, bn), lambda i, j, n: (n, 0, j)),
                  pl.BlockSpec((None, bm, BRANCH_W), lambda i, j, n: (n, i, 0)),
                  pl.BlockSpec((None, None, BRANCH_W, bn), lambda i, j, n: (layer, n, 0, j))],
        out_specs=pl.BlockSpec((bm, bn), lambda i, j, n: (i, j)),
        out_shape=jax.ShapeDtypeStruct((m, d), BF),
        scratch_shapes=[pltpu.VMEM((bm, bn), F32)],
        compiler_params=_cparams(("parallel", "parallel", "arbitrary")),
        name="gated_merge",
    )(h, w_bgate, b_bgate, o_all, w_branch)


def _swiglu_kernel(x_ref, wg_ref, wu_ref, o_ref):
    x = x_ref[...]
    g = jnp.dot(x, wg_ref[...].astype(BF), preferred_element_type=F32)
    u = jnp.dot(x, wu_ref[...].astype(BF), preferred_element_type=F32)
    o_ref[...] = (g * jax.nn.sigmoid(g) * u).astype(o_ref.dtype)


def swiglu_in(x, w, layer, bm=2048, bn=256):
    m, k = x.shape
    n = w.shape[-1] // 2
    bm = min(bm, m)
    nb = n // bn
    return pl.pallas_call(
        _swiglu_kernel,
        grid=(m // bm, nb),
        in_specs=[pl.BlockSpec((bm, k), lambda i, j: (i, 0)),
                  pl.BlockSpec((None, k, bn), lambda i, j: (layer, 0, j)),
                  pl.BlockSpec((None, k, bn), lambda i, j: (layer, 0, nb + j))],
        out_specs=pl.BlockSpec((bm, bn), lambda i, j: (i, j)),
        out_shape=jax.ShapeDtypeStruct((m, n), BF),
        compiler_params=_cparams(("parallel", "parallel")),
        name="swiglu_in",
    )(x, w, w)


def _ple_kernel(r_ref, p_ref, wp_ref, h_ref, wg_ref, o_ref):
    e = jnp.dot(p_ref[...].astype(BF), wp_ref[...], preferred_element_type=F32)
    g = jnp.dot(h_ref[...], wg_ref[...].astype(BF), preferred_element_type=F32)
    o_ref[...] = r_ref[...] + e * jax.nn.sigmoid(g)


def ple_update(r, p, w_ple, hp, w_gate, layer, bm=1024, bn=512):
    m, d = r.shape
    kp = p.shape[-1]
    bm = min(bm, m)
    return pl.pallas_call(
        _ple_kernel,
        grid=(m // bm, d // bn),
        in_specs=[pl.BlockSpec((bm, bn), lambda i, j: (i, j)),
                  pl.BlockSpec((None, bm, kp), lambda i, j: (layer, i, 0)),
                  pl.BlockSpec((None, kp, bn), lambda i, j: (layer, 0, j)),
                  pl.BlockSpec((bm, d), lambda i, j: (i, 0)),
                  pl.BlockSpec((None, d, bn), lambda i, j: (layer, 0, j))],
        out_specs=pl.BlockSpec((bm, bn), lambda i, j: (i, j)),
        out_shape=jax.ShapeDtypeStruct((m, d), F32),
        compiler_params=_cparams(("parallel", "parallel")),
        name="ple_update",
    )(r, p, w_ple, hp, w_gate)


def _static_buckets(dist):
    max_exact = N_BUCKETS // 2
    d = np.maximum(np.asarray(dist), 0)
    large = max_exact + (np.log(np.maximum(d, 1).astype(np.float32) / np.float32(max_exact))
                         / np.float32(math.log(MAX_DIST / max_exact))
                         * np.float32(N_BUCKETS - max_exact)).astype(np.int32)
    large = np.minimum(large, N_BUCKETS - 1)
    return np.where(d < max_exact, d, large).astype(np.int32)


def _toeplitz_tiles(f, t):
    hh, n = f.shape
    big = n + t - 1
    fpad = jnp.concatenate([jnp.full((hh, t - 1), NEG, f.dtype), f], axis=1)
    flat = jnp.tile(fpad, (1, t + 1))[:, :t * (big + 1)]
    w = flat.reshape(hh, t, big + 1)[:, :, :n]
    w = jnp.flip(w.reshape(hh, t, n // t, t), axis=3)
    return w.transpose(0, 2, 1, 3)


def _bias_tile_kernel(prev_ref, cur_ref, o_ref, *, t):
    row = jnp.concatenate([prev_ref[...], cur_ref[...]], axis=1)
    rolled = pltpu.roll(jnp.broadcast_to(row, (t, 2 * t)), 0, 1, stride=1, stride_axis=0)
    o_ref[...] = rolled[:, t:]


def diff_bias_tiles(rel_bias, s):
    t = min(ATT_T, s)
    nq = s // t
    f = jnp.take(rel_bias[:, :HEADS], _static_buckets(np.arange(s)), axis=0).T.astype(F32)
    fb = jnp.concatenate([jnp.full((HEADS, t), NEG, F32), f], axis=1).reshape(HEADS, nq + 1, 1, t)
    return pl.pallas_call(
        functools.partial(_bias_tile_kernel, t=t),
        grid=(HEADS, nq),
        in_specs=[pl.BlockSpec((None, None, 1, t), lambda h, d: (h, d, 0, 0)),
                  pl.BlockSpec((None, None, 1, t), lambda h, d: (h, d + 1, 0, 0))],
        out_specs=pl.BlockSpec((None, None, t, t), lambda h, d: (h, d, 0, 0)),
        out_shape=jax.ShapeDtypeStruct((HEADS, nq, t, t), F32),
        compiler_params=_cparams(("parallel", "parallel")),
        name="diff_bias_tiles",
    )(fb, fb)


def dil_bias_tiles(rel_bias):
    out = []
    for gi, (_, dil) in enumerate(DIL_GROUPS):
        lo = HEADS + gi * HEADS
        steps = np.arange(2 * DIL_STEPS)
        f = jnp.take(rel_bias[:, lo:lo + HEADS], _static_buckets(steps * dil), axis=0).T.astype(F32)
        f = jnp.where(steps[None, :] <= DIL_STEPS, f, NEG)
        tiles = _toeplitz_tiles(f, DIL_STEPS)
        general = jnp.concatenate([tiles[:, 1], tiles[:, 0]], axis=-1)
        first = jnp.concatenate([jnp.full_like(tiles[:, 1], NEG), tiles[:, 0]], axis=-1)
        out.append(jnp.stack([general, first], axis=1))
    return jnp.stack(out, axis=0)


def rope_tables(s):
    half = MLA_ROPE // 2
    inv = ROPE_THETA ** (-jnp.arange(half, dtype=F32) / half)
    ang = jnp.arange(s).astype(F32)[:, None] * inv[None, :]
    cos, sin = jnp.cos(ang), jnp.sin(ang)
    zero = jnp.zeros((s, LANE - MLA_ROPE), F32)
    return (jnp.concatenate([cos, cos, zero], axis=1), jnp.concatenate([-sin, sin, zero], axis=1))


def _rope_tile(t, c, s):
    half = MLA_ROPE // 2
    swapped = pltpu.roll(t, half, axis=1) + pltpu.roll(t, LANE - half, axis=1)
    return t * c + swapped * s


def _flash_kernel(lam_ref, qt_ref, k_ref, vt_ref, *rest, diff, t, scale, post_scale):
    if diff:
        bias_ref, g_ref, o_ref, m_sc, l_sc, acc_sc, s_sc = rest
    else:
        o_ref, m_sc, l_sc, acc_sc, s_sc = rest
    i = pl.program_id(2)
    m_sc[...] = jnp.full(m_sc.shape, NEG, F32)
    l_sc[...] = jnp.zeros(l_sc.shape, F32)
    acc_sc[...] = jnp.zeros(acc_sc.shape, F32)
    qts = [qt_ref[0], qt_ref[1]] if diff else [qt_ref[:, 0:t], qt_ref[:, t:2 * t]]

    def load_kv(j):
        start = pl.multiple_of(j * t, t)
        return k_ref[pl.ds(start, t), :], vt_ref[:, pl.ds(start, t)]

    ntiles = k_ref.shape[0] // t

    def scores_of(work):
        return [jnp.dot(kv[0], qts[c], preferred_element_type=F32) for c, kv, _, _ in work]

    def consume(work, scores):
        for (c, kv, bias, diagonal), s in zip(work, scores):
            if scale != 1.0:
                s = s * scale
            if bias is not None:
                s = bias + s
            elif diagonal:
                key_i = lax.broadcasted_iota(jnp.int32, (t, t), 0)
                qry_i = lax.broadcasted_iota(jnp.int32, (t, t), 1)
                s = jnp.where(key_i <= qry_i, s, NEG)
            m_prev = m_sc[c]
            m_new = jnp.maximum(m_prev, jnp.max(s, axis=0, keepdims=True))
            alpha = jnp.exp(m_prev - m_new)
            p = jnp.exp(s - m_new)
            l_sc[c] = alpha * l_sc[c] + jnp.sum(p, axis=0, keepdims=True)
            m_sc[c] = m_new
            acc_sc[c] = alpha * acc_sc[c] + jnp.dot(kv[1], p.astype(BF), preferred_element_type=F32)

    def step(work):
        consume(work, scores_of(work))

    def sweep(n, work):
        gsz = FLASH_GROUP
        ng = n // gsz

        def group(jj):
            items = []
            for uu in range(gsz):
                items = items + work(gsz * jj + uu)
            return items

        def park(jj, slot):
            first = jnp.minimum(gsz * jj, ntiles - gsz)
            for uu in range(gsz):
                k_tile = load_kv(first + uu)[0]
                for c in range(2):
                    s_sc[slot, 2 * uu + c] = jnp.dot(k_tile, qts[c], preferred_element_type=F32)

        def consume_parked(jj, slot):
            items = group(jj)
            consume(items, [s_sc[slot, idx] for idx in range(len(items))])

        @pl.when(ng > 0)
        def _():
            park(0, 0)

        def body(j2, carry):
            park(2 * j2 + 1, 1)
            consume_parked(2 * j2, 0)
            park(2 * j2 + 2, 0)
            consume_parked(2 * j2 + 1, 1)
            return carry

        lax.fori_loop(0, ng // 2, body, 0)

        @pl.when(ng % 2 == 1)
        def _():
            consume_parked(ng - 1, 0)

        for rr in range(gsz - 1):
            @pl.when(n % gsz > rr)
            def _(rr=rr):
                step(work(ng * gsz + rr))

    if diff:
        def work(j):
            kv = load_kv(j)
            bias = bias_ref[i - j]
            return [(0, kv, bias, False), (1, kv, bias, False)]

        sweep(i + 1, work)
        out_t = acc_sc[0] / l_sc[0] - lam_ref[0] * (acc_sc[1] / l_sc[1])
        o = out_t.T
        ms = jnp.mean(o * o, axis=-1, keepdims=True)
        o_ref[...] = (o * lax.rsqrt(ms + EPS) * g_ref[...] * post_scale).astype(o_ref.dtype)
    else:
        def work(j):
            kv = load_kv(j)
            return [(0, kv, None, False), (1, kv, None, False)]

        sweep(2 * i, work)
        kv = load_kv(2 * i)
        step([(0, kv, None, True), (1, kv, None, False)])
        step([(1, load_kv(2 * i + 1), None, True)])
        o_ref[0:t, :] = (acc_sc[0] / l_sc[0]).T.astype(o_ref.dtype)
        o_ref[t:2 * t, :] = (acc_sc[1] / l_sc[1]).T.astype(o_ref.dtype)


def flash_attention(q, k, vt, s, *, diff, dk, scale, bias=None, lam=None, gain=None, post_scale=1.0, name):
    m = k.shape[0]
    b = m // s
    t = min(ATT_T, s)
    lam = jnp.zeros((1,), F32) if lam is None else lam.reshape(1).astype(F32)
    if diff:
        nq = s // t
        q_spec = pl.BlockSpec((2, dk, t), lambda bb, h, i: (0, h, bb * nq + i))
        o_rows = t
    else:
        nq = s // (2 * t)
        q_spec = pl.BlockSpec((dk, 2 * t), lambda bb, h, i: (h, bb * nq + i))
        o_rows = 2 * t
    in_specs = [pl.BlockSpec(memory_space=pltpu.SMEM), q_spec,
                pl.BlockSpec((s, dk), lambda bb, h, i: (bb, h)),
                pl.BlockSpec((LANE, s), lambda bb, h, i: (h, bb))]
    args = [lam, q, k, vt]
    if diff:
        in_specs += [pl.BlockSpec((None, nq, t, t), lambda bb, h, i: (h, 0, 0, 0)),
                     pl.BlockSpec((1, LANE), lambda bb, h, i: (0, 0))]
        args += [bias, gain.reshape(1, LANE).astype(F32)]
    return pl.pallas_call(
        functools.partial(_flash_kernel, diff=diff, t=t, scale=scale, post_scale=post_scale),
        grid=(b, HEADS, nq),
        in_specs=in_specs,
        out_specs=pl.BlockSpec((o_rows, LANE), lambda bb, h, i: (bb * nq + i, h)),
        out_shape=jax.ShapeDtypeStruct((m, HEADS * LANE), BF),
        scratch_shapes=[pltpu.VMEM((2, 1, t), F32), pltpu.VMEM((2, 1, t), F32), pltpu.VMEM((2, LANE, t), F32),
                        pltpu.VMEM((2, 2 * FLASH_GROUP, t, t), F32)],
        compiler_params=_cparams(("parallel", "parallel", "arbitrary")),
        name=name,
    )(*args)


def _prep_a_kernel(z_ref, qg_ref, kg_ref, qz_ref, kn_ref, vt_ref):
    lane = lax.broadcasted_iota(jnp.int32, (1, LANE), 1)
    lo = lane < DA_QK_DIM

    def norm_halves(x, g):
        x2 = x * x
        s_lo = jnp.sum(jnp.where(lo, x2, 0.0), axis=-1, keepdims=True)
        s_hi = jnp.sum(jnp.where(lo, 0.0, x2), axis=-1, keepdims=True)
        inv = lax.rsqrt(jnp.where(lo, s_lo, s_hi) * (1.0 / DA_QK_DIM) + EPS)
        return x * inv * g

    scale = DA_QK_DIM ** -0.5
    for h in range(HEADS):
        cq = slice(h * LANE, (h + 1) * LANE)
        ck = slice(HEADS * LANE + h * LANE, HEADS * LANE + (h + 1) * LANE)
        q = norm_halves(z_ref[:, cq], qg_ref[...]) * scale
        qt = q.T
        sub_lo = lax.broadcasted_iota(jnp.int32, (LANE, 1), 0) < DA_QK_DIM
        qz_ref[0, cq, :] = jnp.where(sub_lo, qt, 0.0).astype(BF)
        qz_ref[1, cq, :] = jnp.where(sub_lo, 0.0, qt).astype(BF)
        kn_ref[:, cq] = norm_halves(z_ref[:, ck], kg_ref[...]).astype(BF)
        cv = slice(2 * HEADS * LANE + h * LANE, 2 * HEADS * LANE + (h + 1) * LANE)
        vt_ref[cq, :] = z_ref[:, cv].T.astype(BF)


def prep_a(z, q_g, k_g, bm=256):
    m = z.shape[0]
    bm = min(bm, m)
    w = HEADS * LANE
    g2 = lambda g: jnp.concatenate([g, g]).reshape(1, LANE).astype(F32)
    return pl.pallas_call(
        _prep_a_kernel,
        grid=(m // bm,),
        in_specs=[pl.BlockSpec((bm, DA_COLS), lambda i: (i, Z_A // DA_COLS)),
                  pl.BlockSpec((1, LANE), lambda i: (0, 0)),
                  pl.BlockSpec((1, LANE), lambda i: (0, 0))],
        out_specs=[pl.BlockSpec((2, w, bm), lambda i: (0, 0, i)),
                   pl.BlockSpec((bm, w), lambda i: (i, 0)),
                   pl.BlockSpec((w, bm), lambda i: (0, i))],
        out_shape=[jax.ShapeDtypeStruct((2, w, m), BF), jax.ShapeDtypeStruct((m, w), BF),
                   jax.ShapeDtypeStruct((w, m), BF)],
        compiler_params=_cparams(("parallel",)),
        name="prep_a",
    )(z, g2(q_g), g2(k_g))


def mixer_a(z, s, lam, lam_init, q_g, k_g, o_g, bias_tiles):
    qz, kn, vt = prep_a(z, q_g, k_g)
    return flash_attention(qz, kn, vt, s, diff=True, dk=LANE, scale=1.0, bias=bias_tiles, lam=lam, gain=o_g,
                           post_scale=1.0 - lam_init, name="diff_attention")


def _prep_b_kernel(zq_ref, zkv_ref, zkr_ref, wq_ref, wkv_ref, cq_g, ckv_g, kr_g, qn_g, qr_g, kn_g, c_ref, s_ref,
                   qc_ref, kc_ref, vt_ref):
    def norm(x, g, width):
        ms = jnp.sum(x * x, axis=-1, keepdims=True) * (1.0 / width)
        return x * lax.rsqrt(ms + EPS) * g

    cos, sin = c_ref[...], s_ref[...]
    c_q = norm(zq_ref[...], cq_g[...], MLA_Q_RANK).astype(BF)
    c_kv = norm(zkv_ref[...], ckv_g[...], MLA_KV_RANK).astype(BF)
    q_up = jnp.dot(c_q, wq_ref[...], preferred_element_type=F32)
    kv_up = jnp.dot(c_kv, wkv_ref[...], preferred_element_type=F32)
    kr = _rope_tile(norm(zkr_ref[...], kr_g[...], MLA_ROPE), cos, sin).astype(BF)
    for h in range(HEADS):
        c0 = slice(2 * h * LANE, (2 * h + 1) * LANE)
        c1 = slice((2 * h + 1) * LANE, (2 * h + 2) * LANE)
        qc_ref[c0, :] = norm(q_up[:, c0], qn_g[...], MLA_NOPE).T.astype(BF)
        qc_ref[c1, :] = _rope_tile(norm(q_up[:, c1], qr_g[...], MLA_ROPE), cos, sin).T.astype(BF)
        kc_ref[:, c0] = norm(kv_up[:, c0], kn_g[...], MLA_NOPE).astype(BF)
        kc_ref[:, c1] = kr
        vt_ref[h * LANE:(h + 1) * LANE, :] = kv_up[:, c1].T.astype(BF)


def prep_b(z, w_uq, w_ukv, layer, cq_g, ckv_g, kr_g, qn_g, qr_g, kn_g, cos_t, sin_t, s, bm=256):
    m = z.shape[0]
    bm = min(bm, s)
    nsb = s // bm
    w2 = 2 * HEADS * LANE
    row = lambda g: g.reshape(1, -1).astype(F32)
    pad = lambda g: jnp.concatenate([g, jnp.zeros((LANE - MLA_ROPE,), g.dtype)])
    const = lambda i: (0, 0)
    return pl.pallas_call(
        _prep_b_kernel,
        grid=(m // bm,),
        in_specs=[pl.BlockSpec((bm, MLA_Q_RANK), lambda i: (i, Z_B // MLA_Q_RANK)),
                  pl.BlockSpec((bm, MLA_KV_RANK), lambda i: (i, (Z_B + MLA_Q_RANK) // MLA_KV_RANK)),
                  pl.BlockSpec((bm, LANE), lambda i: (i, (Z_B + MLA_Q_RANK + MLA_KV_RANK) // LANE)),
                  pl.BlockSpec((MLA_Q_RANK, w2), const),
                  pl.BlockSpec((None, MLA_KV_RANK, w2), lambda i: (layer, 0, 0)),
                  pl.BlockSpec((1, MLA_Q_RANK), const),
                  pl.BlockSpec((1, MLA_KV_RANK), const),
                  pl.BlockSpec((1, LANE), const),
                  pl.BlockSpec((1, LANE), const),
                  pl.BlockSpec((1, LANE), const),
                  pl.BlockSpec((1, LANE), const),
                  pl.BlockSpec((bm, LANE), lambda i: (i % nsb, 0)),
                  pl.BlockSpec((bm, LANE), lambda i: (i % nsb, 0))],
        out_specs=[pl.BlockSpec((w2, bm), lambda i: (0, i)),
                   pl.BlockSpec((bm, w2), lambda i: (i, 0)),
                   pl.BlockSpec((HEADS * LANE, bm), lambda i: (0, i))],
        out_shape=[jax.ShapeDtypeStruct((w2, m), BF), jax.ShapeDtypeStruct((m, w2), BF),
                   jax.ShapeDtypeStruct((HEADS * LANE, m), BF)],
        compiler_params=_cparams(("parallel",)),
        name="prep_b",
    )(z, z, z, w_uq, w_ukv, row(cq_g), row(ckv_g), row(pad(kr_g)), row(qn_g), row(pad(qr_g)), row(kn_g),
      cos_t, sin_t)


def pack_w_uq(w):
    w = w.reshape(MLA_Q_RANK, HEADS, MLA_NOPE + MLA_ROPE)
    w = jnp.pad(w, ((0, 0), (0, 0), (0, 2 * LANE - MLA_NOPE - MLA_ROPE)))
    return w.reshape(MLA_Q_RANK, HEADS * 2 * LANE).astype(BF)


def mixer_b(z, s, w_uq, w_ukv, cq_g, ckv_g, qn_g, kn_g, qr_g, kr_g, cos_t, sin_t, layer=0):
    qc, kc, vt = prep_b(z, pack_w_uq(w_uq), w_ukv, layer, cq_g, ckv_g, kr_g, qn_g, qr_g, kn_g, cos_t, sin_t, s)
    return flash_attention(qc, kc, vt, s, diff=False, dk=2 * LANE, scale=(MLA_NOPE + MLA_ROPE) ** -0.5,
                           name="mla_attention")


DIL_PAD = DIL_STEPS * max(d for _, d in DIL_GROUPS)


def _dil_kernel(zq_ref, zk_ref, zv_ref, bias_ref, qg_ref, kg_ref, o_ref, q_sc, k_sc, v_sc, og_sc, lse_sc, *, s):
    g = pl.program_id(2)
    ngroups = len(DIL_GROUPS)

    def norm(x, gain):
        ms = jnp.mean(x * x, axis=-1, keepdims=True)
        return x * lax.rsqrt(ms + EPS) * gain

    zeros = jnp.zeros((DIL_PAD, LANE), F32)
    k_sc[0:DIL_PAD, :] = zeros
    v_sc[0:DIL_PAD, :] = zeros
    q_sc[...] = norm(zq_ref[...], qg_ref[...])
    k_sc[DIL_PAD:, :] = norm(zk_ref[...], kg_ref[...])
    v_sc[DIL_PAD:, :] = zv_ref[...]
    scale = LANE ** -0.5

    def group(gi, dil):
        nsub = s // DIL_STEPS

        def body(tt, carry):
            c = tt % dil
            n = tt // dil
            q0 = c + dil * DIL_STEPS * n
            rows = pl.ds(q0, DIL_STEPS, stride=dil) if dil > 1 else pl.ds(q0, DIL_STEPS)
            k0 = q0 + DIL_PAD - dil * DIL_STEPS
            band = pl.ds(k0, 2 * DIL_STEPS, stride=dil) if dil > 1 else pl.ds(k0, 2 * DIL_STEPS)
            qs = q_sc[rows, :].astype(BF)
            ks = k_sc[band, :].astype(BF)
            vs = v_sc[band, :].astype(BF)
            first = jnp.where(n == 0, 1, 0)
            logits = lax.dot_general(qs, ks, (((1,), (1,)), ((), ())), preferred_element_type=F32) * scale
            logits = logits + bias_ref[first]
            mx = jnp.max(logits, axis=-1, keepdims=True)
            e = jnp.exp(logits - mx)
            den = jnp.sum(e, axis=-1, keepdims=True)
            o = jnp.dot((e / den).astype(BF), vs, preferred_element_type=F32)
            og_sc[gi, rows, :] = o
            lse_sc[gi, rows, :] = jnp.broadcast_to(mx + jnp.log(den), (DIL_STEPS, LANE))
            return carry

        lax.fori_loop(0, nsub, body, 0, unroll=8)

    for gi, (_, dil) in enumerate(DIL_GROUPS):
        pl.when(g == gi)(functools.partial(group, gi, dil))

    @pl.when(g == ngroups - 1)
    def _():
        lses = [lse_sc[gi] for gi in range(ngroups)]
        mx = functools.reduce(jnp.maximum, lses)
        ws = [jnp.exp(l - mx) for l in lses]
        tot = functools.reduce(lambda a, b2: a + b2, ws)
        acc = ws[0] * og_sc[0]
        for gi in range(1, ngroups):
            acc = acc + ws[gi] * og_sc[gi]
        o_ref[...] = (acc / tot).astype(o_ref.dtype)


def mixer_c(z, s, q_g, k_g, bias_c):
    m = z.shape[0]
    b = m // s
    ngroups = len(DIL_GROUPS)
    cb = Z_C // LANE

    def col(which):
        return lambda bb, h, g: (bb, cb + (g * 3 + which) * HEADS + h)

    row = lambda g: g.reshape(1, LANE).astype(F32)
    return pl.pallas_call(
        functools.partial(_dil_kernel, s=s),
        grid=(b, HEADS, ngroups),
        in_specs=[pl.BlockSpec((s, LANE), col(0)),
                  pl.BlockSpec((s, LANE), col(1)),
                  pl.BlockSpec((s, LANE), col(2)),
                  pl.BlockSpec((None, None, 2, DIL_STEPS, 2 * DIL_STEPS), lambda bb, h, g: (g, h, 0, 0, 0)),
                  pl.BlockSpec((1, LANE), lambda bb, h, g: (0, 0)),
                  pl.BlockSpec((1, LANE), lambda bb, h, g: (0, 0))],
        out_specs=pl.BlockSpec((s, LANE), lambda bb, h, g: (bb, h)),
        out_shape=jax.ShapeDtypeStruct((m, HEADS * LANE), BF),
        scratch_shapes=[pltpu.VMEM((s, LANE), F32), pltpu.VMEM((DIL_PAD + s, LANE), F32),
                        pltpu.VMEM((DIL_PAD + s, LANE), F32), pltpu.VMEM((ngroups, s, LANE), F32),
                        pltpu.VMEM((ngroups, s, LANE), F32)],
        compiler_params=_cparams(("parallel", "parallel", "arbitrary")),
        name="dilated_attention",
    )(z, z, z, bias_c, row(q_g), row(k_g))


CONV_PAD = 8
GDN_GROUP = 4
GDN_BASE = 8


def _gdn_kernel(par_ref, zq_ref, zk_ref, zv_ref, zg_ref, zab_ref, wq_ref, wk_ref, wv_ref, og_ref, o_ref,
                x_sc, q_sc, k_sc, v_sc, g_sc, b_sc, u_sc, w_sc, a_sc, st_sc, *, s):
    h = pl.program_id(1)
    c = GDN_CHUNK

    def conv_silu(z_ref, w_ref):
        x_sc[0:CONV_PAD, :] = jnp.zeros((CONV_PAD, LANE), F32)
        x_sc[CONV_PAD:, :] = z_ref[...]
        y = x_sc[CONV_PAD:, :] * w_ref[CONV_W - 1:CONV_W, :]
        for i in range(CONV_W - 1):
            off = CONV_PAD - (CONV_W - 1) + i
            y = y + x_sc[off:off + s, :] * w_ref[i:i + 1, :]
        return y * jax.nn.sigmoid(y)

    def l2(x):
        return x * lax.rsqrt(jnp.sum(x * x, axis=-1, keepdims=True) + EPS)

    q_sc[...] = l2(conv_silu(zq_ref, wq_ref)) * (LANE ** -0.5)
    k_sc[...] = l2(conv_silu(zk_ref, wk_ref))
    v_sc[...] = conv_silu(zv_ref, wv_ref)
    lane = lax.broadcasted_iota(jnp.int32, (1, LANE), 1)
    ab = zab_ref[...]
    a_col = jnp.sum(jnp.where(lane == h, ab, 0.0), axis=-1, keepdims=True)
    b_col = jnp.sum(jnp.where(lane == h + HEADS, ab, 0.0), axis=-1, keepdims=True)
    a_neg_exp = par_ref[0, h]
    dt_bias = par_ref[1, h]
    g_sc[...] = jnp.broadcast_to(a_neg_exp * jax.nn.softplus(a_col + dt_bias), (s, LANE))
    b_sc[...] = jnp.broadcast_to(jax.nn.sigmoid(b_col), (s, LANE))
    st_sc[...] = jnp.zeros(st_sc.shape, F32)

    gr = GDN_GROUP * c
    r_i = lax.broadcasted_iota(jnp.int32, (gr, gr), 0)
    c_i = lax.broadcasted_iota(jnp.int32, (gr, gr), 1)
    blk = {}
    size = GDN_BASE
    while size <= c:
        blk[size] = (r_i // size) == (c_i // size)
        size *= 2
    same = blk[c]
    tril = same & (r_i >= c_i)
    strict = same & (r_i > c_i)
    tril_b = tril.astype(BF)
    triu_b = (same & (r_i <= c_i)).astype(BF)
    eye = (r_i == c_i).astype(F32)

    def mm(a, b2):
        return jnp.dot(a.astype(BF), b2.astype(BF), preferred_element_type=F32)

    def mm_t(a, b2):
        return lax.dot_general(a.astype(BF), b2.astype(BF), (((1,), (1,)), ((), ())), preferred_element_type=F32)

    def split2(x):
        hi = x.astype(BF)
        return hi, (x - hi.astype(F32)).astype(BF)

    def split3(x):
        hi = x.astype(BF)
        r1 = x - hi.astype(F32)
        mid = r1.astype(BF)
        return hi, mid, (r1 - mid.astype(F32)).astype(BF)

    def mm_hi(a, b2):
        a_hi, a_lo = split2(a)
        b_hi, b_lo = split2(b2)
        d = functools.partial(jnp.dot, preferred_element_type=F32)
        return d(a_hi, b_hi) + (d(a_hi, b_lo) + d(a_lo, b_hi))

    def widen(x):
        return jnp.concatenate([x] * (gr // LANE), axis=1)

    def local_group(gi, carry):
        base = pl.multiple_of(gi * gr, gr)
        rows = pl.ds(base, gr)
        q = q_sc[rows, :]
        k = k_sc[rows, :]
        beta = b_sc[rows, :]
        parts = split3(g_sc[rows, :])
        gc = sum(jnp.dot(tril_b, part, preferred_element_type=F32) for part in parts)
        g_row = sum(lax.dot_general(widen(part), triu_b, (((0,), (0,)), ((), ())), preferred_element_type=F32)
                    for part in parts)
        decay = jnp.exp(jnp.where(tril, widen(gc) - g_row, NEG))
        kb = k * beta
        vb = v_sc[rows, :] * beta
        lower = jnp.where(strict, mm_t(kb, k) * decay, 0.0)
        neg = -jnp.where(blk[GDN_BASE], lower, 0.0)
        p1 = mm_hi(neg, neg)
        tmat = eye + neg
        tmat = tmat + mm_hi(tmat, p1)
        tmat = tmat + mm_hi(tmat, mm_hi(p1, p1))
        size = GDN_BASE
        while size < c:
            off = jnp.where(blk[2 * size] & jnp.logical_not(blk[size]), lower, 0.0)
            tmat = tmat - mm_hi(mm_hi(tmat, off), tmat)
            size *= 2
        eg = jnp.exp(gc)
        u_sc[rows, :] = mm(tmat, vb)
        w_sc[rows, :] = mm(tmat, kb * eg)
        intra = mm_t(q, k) * decay
        q_sc[rows, :] = q * eg
        for uu in range(GDN_GROUP):
            sl = slice(uu * c, (uu + 1) * c)
            crow = pl.ds(base + uu * c, c)
            g_last = gc[(uu + 1) * c - 1:(uu + 1) * c, :]
            a_sc[crow, 0:c] = intra[sl, sl]
            k_sc[crow, :] = k[sl] * jnp.exp(g_last - gc[sl])
            g_sc[crow, :] = jnp.broadcast_to(jnp.exp(g_last), (c, LANE))
        return carry


    def scan(n, carry):
        rows = pl.ds(pl.multiple_of(n * c, c), c)
        state = st_sc[...]
        v_new = u_sc[rows, :] - mm(w_sc[rows, :], state)
        o = mm(q_sc[rows, :], state) + mm(a_sc[rows, 0:c], v_new)
        decay_last = g_sc[pl.ds(pl.multiple_of(n * c, c), 1), :]
        st_sc[...] = state * decay_last + lax.dot_general(
            k_sc[rows, :].astype(BF), v_new.astype(BF), (((0,), (0,)), ((), ())), preferred_element_type=F32)
        u_sc[rows, :] = o
        return carry

    ngroups = s // gr
    local_group(0, 0)

    def pipelined(g, carry):
        local_group(g + 1, 0)
        for uu in range(GDN_GROUP):
            scan(g * GDN_GROUP + uu, 0)
        return carry

    lax.fori_loop(0, ngroups - 1, pipelined, 0)
    for uu in range(GDN_GROUP):
        scan((ngroups - 1) * GDN_GROUP + uu, 0)
    o = u_sc[...]
    ms = jnp.mean(o * o, axis=-1, keepdims=True)
    gate = zg_ref[...]
    o_ref[...] = (o * lax.rsqrt(ms + EPS) * og_ref[...] * (gate * jax.nn.sigmoid(gate))).astype(o_ref.dtype)


def mixer_d(z, s, conv_w, a_log, dt_bias, o_g):
    m = z.shape[0]
    b = m // s
    cb = Z_D // LANE
    par = jnp.stack([-jnp.exp(a_log.astype(F32)), dt_bias.astype(F32)], axis=0)
    cw = conv_w.astype(F32)

    def col(which):
        return lambda bb, h: (bb, cb + which * HEADS + h)

    return pl.pallas_call(
        functools.partial(_gdn_kernel, s=s),
        grid=(b, HEADS),
        in_specs=[pl.BlockSpec(memory_space=pltpu.SMEM),
                  pl.BlockSpec((s, LANE), col(0)),
                  pl.BlockSpec((s, LANE), col(1)),
                  pl.BlockSpec((s, LANE), col(2)),
                  pl.BlockSpec((s, LANE), col(3)),
                  pl.BlockSpec((s, LANE), lambda bb, h: (bb, cb + 4 * HEADS)),
                  pl.BlockSpec((CONV_W, LANE), lambda bb, h: (0, h)),
                  pl.BlockSpec((CONV_W, LANE), lambda bb, h: (0, HEADS + h)),
                  pl.BlockSpec((CONV_W, LANE), lambda bb, h: (0, 2 * HEADS + h)),
                  pl.BlockSpec((1, LANE), lambda bb, h: (0, 0))],
        out_specs=pl.BlockSpec((s, LANE), lambda bb, h: (bb, h)),
        out_shape=jax.ShapeDtypeStruct((m, HEADS * LANE), BF),
        scratch_shapes=[pltpu.VMEM((CONV_PAD + s, LANE), F32), pltpu.VMEM((s, LANE), F32),
                        pltpu.VMEM((s, LANE), F32), pltpu.VMEM((s, LANE), F32), pltpu.VMEM((s, LANE), F32),
                        pltpu.VMEM((s, LANE), F32), pltpu.VMEM((s, LANE), F32), pltpu.VMEM((s, LANE), F32),
                        pltpu.VMEM((s, LANE), F32), pltpu.VMEM((LANE, LANE), F32)],
        compiler_params=_cparams(("parallel", "arbitrary")),
        name="gated_deltanet",
    )(par, z, z, z, z, z, cw, cw, cw, o_g.reshape(1, LANE).astype(F32))


def _cast_kernel(x_ref, o_ref):
    o_ref[...] = x_ref[...].astype(o_ref.dtype)


def cast_bf16(w, bm=512):
    r, c = w.shape
    bm = min(bm, r)
    return pl.pallas_call(
        _cast_kernel,
        grid=(r // bm,),
        in_specs=[pl.BlockSpec((bm, c), lambda i: (i, 0))],
        out_specs=pl.BlockSpec((bm, c), lambda i: (i, 0)),
        out_shape=jax.ShapeDtypeStruct((r, c), BF),
        compiler_params=_cparams(("parallel",)),
        name="cast_bf16",
    )(w)


IN_COLS = DA_COLS + MLA_COLS + DIL_COLS + GDN_QKV + 2 * HEADS + HEADS * LANE


def _pack_w_in_kernel(w_ref, o_ref):
    rows = w_ref.shape[0]

    def put(dst, src, width):
        o_ref[:, dst:dst + width] = w_ref[:, src:src + width]

    def zero(dst, width):
        o_ref[:, dst:dst + width] = jnp.zeros((rows, width), BF)

    c0 = DA_COLS + MLA_COLS + DIL_COLS
    put(0, 0, DA_COLS + MLA_COLS)
    zero(Z_B + MLA_COLS, Z_C - Z_B - MLA_COLS)
    put(Z_C, DA_COLS + MLA_COLS, DIL_COLS)
    put(Z_D, c0, GDN_QKV)
    put(Z_D + GDN_QKV, c0 + GDN_QKV + 2 * HEADS, HEADS * LANE)
    put(Z_D + GDN_QKV + HEADS * LANE, c0 + GDN_QKV, 2 * HEADS)
    used = Z_D + GDN_QKV + HEADS * LANE + 2 * HEADS
    zero(used, Z_COLS - used)


def pack_w_in(w, bm=256):
    r = w.shape[0]
    return pl.pallas_call(
        _pack_w_in_kernel,
        grid=(r // bm,),
        in_specs=[pl.BlockSpec((bm, IN_COLS), lambda i: (i, 0))],
        out_specs=pl.BlockSpec((bm, Z_COLS), lambda i: (i, 0)),
        out_shape=jax.ShapeDtypeStruct((r, Z_COLS), BF),
        compiler_params=_cparams(("parallel",)),
        name="pack_w_in",
    )(w)


def _cast_stacked(w, bm=512):
    lead, c = w.shape[:-1], w.shape[-1]
    return cast_bf16(w.reshape(-1, c), bm).reshape(*lead, c)


def kernel(x, p, rel_bias, norm_mix_g, w_in, da_lambda, da_q_g, da_k_g, da_o_g, mla_w_uq, mla_w_ukv, mla_cq_g, mla_ckv_g, mla_qn_g, mla_kn_g, mla_qr_g, mla_kr_g, dil_q_g, dil_k_g, gdn_conv_w, gdn_a_log, gdn_dt_bias, gdn_o_g, w_bgate, b_bgate, w_branch, w_out, norm_ffn_g, w_ffn_in, w_ffn_out, norm_ple_g, w_ple, w_ple_gate):
    b, s, d = x.shape
    m = b * s
    bias_a = diff_bias_tiles(rel_bias, s)
    bias_c = dil_bias_tiles(rel_bias)
    cos_t, sin_t = rope_tables(s)
    w_in_p = pack_w_in(w_in.astype(BF).reshape(DEPTH * d, IN_COLS)).reshape(DEPTH, d, Z_COLS)
    w_branch_b = _cast_stacked(w_branch)
    w_out_b = _cast_stacked(w_out)
    w_ffn_out_b = _cast_stacked(w_ffn_out)
    w_ple_b = _cast_stacked(w_ple)
    w_ukv_b = _cast_stacked(mla_w_ukv)
    p_r = p.reshape(DEPTH, m, PLE_DIM)
    xr = x.reshape(m, d)
    for i in range(DEPTH):
        h = rmsnorm_rows(xr, norm_mix_g[i])
        z = matmul(h, w_in_p, F32, layer=i, name="in_proj")
        lam_init = 0.8 - 0.6 * math.exp(-0.3 * i)
        lq1, lk1, lq2, lk2 = (da_lambda[i, j].astype(F32) for j in range(4))
        lam = jnp.exp(jnp.sum(lq1 * lk1)) - jnp.exp(jnp.sum(lq2 * lk2)) + lam_init
        o_a = mixer_a(z, s, lam, lam_init, da_q_g[i], da_k_g[i], da_o_g[i], bias_a)
        o_b = mixer_b(z, s, mla_w_uq[i], w_ukv_b, mla_cq_g[i], mla_ckv_g[i], mla_qn_g[i], mla_kn_g[i],
                      mla_qr_g[i], mla_kr_g[i], cos_t, sin_t, layer=i)
        o_c = mixer_c(z, s, dil_q_g[i], dil_k_g[i], bias_c)
        o_d = mixer_d(z, s, gdn_conv_w[i], gdn_a_log[i], gdn_dt_bias[i], gdn_o_g[i])
        o_all = jnp.stack([o_a, o_b, o_c, o_d], axis=0)
        merged = gated_merge(h, w_bgate, b_bgate[i].reshape(N_BRANCH, 1, d), o_all, w_branch_b, i)
        xr = matmul_residual(xr, merged, w_out_b, bn=1024, layer=i, name="out_proj")
        hf = rmsnorm_rows(xr, norm_ffn_g[i])
        act = swiglu_in(hf, w_ffn_in, i)
        xr = matmul_residual(xr, act, w_ffn_out_b, bm=512, bn=512, layer=i, name="ffn_out")
        hp = rmsnorm_rows(xr, norm_ple_g[i])
        xr = ple_update(xr, p_r, w_ple_b, hp, w_ple_gate, i)
    return xr.reshape(b, s, d)
```

```python
import functools
import math

import numpy as np
import jax
import jax.numpy as jnp
from jax import lax
from jax.experimental import pallas as pl
from jax.experimental.pallas import tpu as pltpu

DEPTH = 2
EPS = 1e-6
PLE_DIM = 256
HEADS = 8
N_BRANCH = 4
BRANCH_W = 1024
DA_QK_DIM = 64
MLA_Q_RANK = 1024
MLA_KV_RANK = 512
MLA_NOPE = 128
MLA_ROPE = 64
ROPE_THETA = 10000.0
DIL_GROUPS = ((128, 1), (512, 4), (2048, 16))
DIL_STEPS = 128
GDN_CHUNK = 64
CONV_W = 4
N_BUCKETS = 32
MAX_DIST = 2048

DA_COLS = 3072
MLA_COLS = 1600
DIL_COLS = 9216
GDN_QKV = 3072

LANE = 128
Z_A = 0
Z_B = Z_A + DA_COLS
Z_C = Z_B + MLA_COLS + 64
Z_D = Z_C + DIL_COLS
Z_COLS = 18 * 1024

NEG = -1e30
ATT_T = 256
FLASH_GROUP = 2
VMEM_LIMIT = 56 * 1024 * 1024
BF = jnp.bfloat16
F32 = jnp.float32


def _cparams(sem):
    return pltpu.CompilerParams(dimension_semantics=sem, vmem_limit_bytes=VMEM_LIMIT)


def _rmsnorm_kernel(x_ref, g_ref, o_ref):
    x = x_ref[...]
    ms = jnp.mean(x * x, axis=-1, keepdims=True)
    o_ref[...] = (x * lax.rsqrt(ms + EPS) * g_ref[...]).astype(o_ref.dtype)


def rmsnorm_rows(x, g, bm=512):
    m = x.shape[0]
    d = g.shape[-1]
    return pl.pallas_call(
        _rmsnorm_kernel,
        grid=(m // bm,),
        in_specs=[pl.BlockSpec((bm, d), lambda i: (i, 0)),
                  pl.BlockSpec((1, d), lambda i: (0, 0))],
        out_specs=pl.BlockSpec((bm, d), lambda i: (i, 0)),
        out_shape=jax.ShapeDtypeStruct((m, d), BF),
        compiler_params=_cparams(("parallel",)),
        name="rmsnorm_rows",
    )(x, g.reshape(1, d))


def _mm_kernel(x_ref, w_ref, o_ref):
    o_ref[...] = jnp.dot(x_ref[...], w_ref[...], preferred_element_type=F32).astype(o_ref.dtype)


def _layer_spec(w, block, index_map, layer):
    if layer is None:
        return pl.BlockSpec(block, index_map)
    return pl.BlockSpec((None,) + block, lambda *g: (layer,) + index_map(*g))


def matmul(x, w, out_dtype, bm=1024, bn=1024, layer=None, name="matmul"):
    m, k = x.shape
    n = w.shape[-1]
    bn = min(bn, n)
    bm = min(bm, m)
    return pl.pallas_call(
        _mm_kernel,
        grid=(m // bm, n // bn),
        in_specs=[pl.BlockSpec((bm, k), lambda i, j: (i, 0)),
                  _layer_spec(w, (k, bn), lambda i, j: (0, j), layer)],
        out_specs=pl.BlockSpec((bm, bn), lambda i, j: (i, j)),
        out_shape=jax.ShapeDtypeStruct((m, n), out_dtype),
        compiler_params=_cparams(("parallel", "parallel")),
        name=name,
    )(x, w)


def _mm_residual_kernel(r_ref, x_ref, w_ref, o_ref, *acc, nk):
    part = jnp.dot(x_ref[...], w_ref[...], preferred_element_type=F32)
    if nk == 1:
        o_ref[...] = r_ref[...] + part
    else:
        acc_ref, = acc

        @pl.when(pl.program_id(2) == 0)
        def _():
            acc_ref[...] = r_ref[...]

        total = acc_ref[...] + part
        acc_ref[...] = total
        o_ref[...] = total


def matmul_residual(r, x, w, bm=1024, bn=512, bk=None, layer=None, name="matmul_residual"):
    m, kdim = x.shape
    n = w.shape[-1]
    bk = kdim if bk is None else bk
    bm = min(bm, m)
    nk = kdim // bk
    return pl.pallas_call(
        functools.partial(_mm_residual_kernel, nk=nk),
        grid=(m // bm, n // bn, nk),
        in_specs=[pl.BlockSpec((bm, bn), lambda i, j, k: (i, j)),
                  pl.BlockSpec((bm, bk), lambda i, j, k: (i, k)),
                  _layer_spec(w, (bk, bn), lambda i, j, k: (k, j), layer)],
        out_specs=pl.BlockSpec((bm, bn), lambda i, j, k: (i, j)),
        out_shape=jax.ShapeDtypeStruct((m, n), F32),
        scratch_shapes=[pltpu.VMEM((bm, bn), F32)] if nk > 1 else [],
        compiler_params=_cparams(("parallel", "parallel", "arbitrary")),
        name=name,
    )(r, x, w)


def _merge_kernel(h_ref, wg_ref, b_ref, o_ref, wb_ref, out_ref, acc_ref):
    n = pl.program_id(2)
    gate = jax.nn.sigmoid(jnp.dot(h_ref[...], wg_ref[...].astype(BF), preferred_element_type=F32) + b_ref[...])
    term = gate * jnp.dot(o_ref[...], wb_ref[...], preferred_element_type=F32)

    @pl.when(n == 0)
    def _():
        acc_ref[...] = jnp.zeros(acc_ref.shape, F32)

    total = acc_ref[...] + term
    acc_ref[...] = total
    out_ref[...] = total.astype(out_ref.dtype)


def gated_merge(h, w_bgate, b_bgate, o_all, w_branch, layer, bm=1024, bn=512):
    m, d = h.shape
    bm = min(bm, m)
    return pl.pallas_call(
        _merge_kernel,
        grid=(m // bm, d // bn, N_BRANCH),
        in_specs=[pl.BlockSpec((bm, d), lambda i, j, n: (i, 0)),
                  pl.BlockSpec((None, None, d, bn), lambda i, j, n: (layer, n, 0, j)),
                  pl.BlockSpec((None, 1, bn), lambda i, j, n: (n, 0, j)),
                  pl.BlockSpec((None, bm, BRANCH_W), lambda i, j, n: (n, i, 0)),
                  pl.BlockSpec((None, None, BRANCH_W, bn), lambda i, j, n: (layer, n, 0, j))],
        out_specs=pl.BlockSpec((bm, bn), lambda i, j, n: (i, j)),
        out_shape=jax.ShapeDtypeStruct((m, d), BF),
        scratch_shapes=[pltpu.VMEM((bm, bn), F32)],
        compiler_params=_cparams(("parallel", "parallel", "arbitrary")),
        name="gated_merge",
    )(h, w_bgate, b_bgate, o_all, w_branch)


def _swiglu_kernel(x_ref, wg_ref, wu_ref, o_ref):
    x = x_ref[...]
    g = jnp.dot(x, wg_ref[...].astype(BF), preferred_element_type=F32)
    u = jnp.dot(x, wu_ref[...].astype(BF), preferred_element_type=F32)
    o_ref[...] = (g * jax.nn.sigmoid(g) * u).astype(o_ref.dtype)


def swiglu_in(x, w, layer, bm=2048, bn=256):
    m, k = x.shape
    n = w.shape[-1] // 2
    bm = min(bm, m)
    nb = n // bn
    return pl.pallas_call(
        _swiglu_kernel,
        grid=(m // bm, nb),
        in_specs=[pl.BlockSpec((bm, k), lambda i, j: (i, 0)),
                  pl.BlockSpec((None, k, bn), lambda i, j: (layer, 0, j)),
                  pl.BlockSpec((None, k, bn), lambda i, j: (layer, 0, nb + j))],
        out_specs=pl.BlockSpec((bm, bn), lambda i, j: (i, j)),
        out_shape=jax.ShapeDtypeStruct((m, n), BF),
        compiler_params=_cparams(("parallel", "parallel")),
        name="swiglu_in",
    )(x, w, w)


def _ple_kernel(r_ref, p_ref, wp_ref, h_ref, wg_ref, o_ref):
    e = jnp.dot(p_ref[...].astype(BF), wp_ref[...], preferred_element_type=F32)
    g = jnp.dot(h_ref[...], wg_ref[...].astype(BF), preferred_element_type=F32)
    o_ref[...] = r_ref[...] + e * jax.nn.sigmoid(g)


def ple_update(r, p, w_ple, hp, w_gate, layer, bm=1024, bn=512):
    m, d = r.shape
    kp = p.shape[-1]
    bm = min(bm, m)
    return pl.pallas_call(
        _ple_kernel,
        grid=(m // bm, d // bn),
        in_specs=[pl.BlockSpec((bm, bn), lambda i, j: (i, j)),
                  pl.BlockSpec((None, bm, kp), lambda i, j: (layer, i, 0)),
                  pl.BlockSpec((None, kp, bn), lambda i, j: (layer, 0, j)),
                  pl.BlockSpec((bm, d), lambda i, j: (i, 0)),
                  pl.BlockSpec((None, d, bn), lambda i, j: (layer, 0, j))],
        out_specs=pl.BlockSpec((bm, bn), lambda i, j: (i, j)),
        out_shape=jax.ShapeDtypeStruct((m, d), F32),
        compiler_params=_cparams(("parallel", "parallel")),
        name="ple_update",
    )(r, p, w_ple, hp, w_gate)


def _static_buckets(dist):
    max_exact = N_BUCKETS // 2
    d = np.maximum(np.asarray(dist), 0)
    large = max_exact + (np.log(np.maximum(d, 1).astype(np.float32) / np.float32(max_exact))
                         / np.float32(math.log(MAX_DIST / max_exact))
                         * np.float32(N_BUCKETS - max_exact)).astype(np.int32)
    large = np.minimum(large, N_BUCKETS - 1)
    return np.where(d < max_exact, d, large).astype(np.int32)


def _toeplitz_tiles(f, t):
    hh, n = f.shape
    big = n + t - 1
    fpad = jnp.concatenate([jnp.full((hh, t - 1), NEG, f.dtype), f], axis=1)
    flat = jnp.tile(fpad, (1, t + 1))[:, :t * (big + 1)]
    w = flat.reshape(hh, t, big + 1)[:, :, :n]
    w = jnp.flip(w.reshape(hh, t, n // t, t), axis=3)
    return w.transpose(0, 2, 1, 3)


def _bias_tile_kernel(prev_ref, cur_ref, o_ref, *, t):
    row = jnp.concatenate([prev_ref[...], cur_ref[...]], axis=1)
    rolled = pltpu.roll(jnp.broadcast_to(row, (t, 2 * t)), 0, 1, stride=1, stride_axis=0)
    o_ref[...] = rolled[:, t:]


def diff_bias_tiles(rel_bias, s):
    t = min(ATT_T, s)
    nq = s // t
    f = jnp.take(rel_bias[:, :HEADS], _static_buckets(np.arange(s)), axis=0).T.astype(F32)
    fb = jnp.concatenate([jnp.full((HEADS, t), NEG, F32), f], axis=1).reshape(HEADS, nq + 1, 1, t)
    return pl.pallas_call(
        functools.partial(_bias_tile_kernel, t=t),
        grid=(HEADS, nq),
        in_specs=[pl.BlockSpec((None, None, 1, t), lambda h, d: (h, d, 0, 0)),
                  pl.BlockSpec((None, None, 1, t), lambda h, d: (h, d + 1, 0, 0))],
        out_specs=pl.BlockSpec((None, None, t, t), lambda h, d: (h, d, 0, 0)),
        out_shape=jax.ShapeDtypeStruct((HEADS, nq, t, t), F32),
        compiler_params=_cparams(("parallel", "parallel")),
        name="diff_bias_tiles",
    )(fb, fb)


def dil_bias_tiles(rel_bias):
    out = []
    for gi, (_, dil) in enumerate(DIL_GROUPS):
        lo = HEADS + gi * HEADS
        steps = np.arange(2 * DIL_STEPS)
        f = jnp.take(rel_bias[:, lo:lo + HEADS], _static_buckets(steps * dil), axis=0).T.astype(F32)
        f = jnp.where(steps[None, :] <= DIL_STEPS, f, NEG)
        tiles = _toeplitz_tiles(f, DIL_STEPS)
        general = jnp.concatenate([tiles[:, 1], tiles[:, 0]], axis=-1)
        first = jnp.concatenate([jnp.full_like(tiles[:, 1], NEG), tiles[:, 0]], axis=-1)
        out.append(jnp.stack([general, first], axis=1))
    return jnp.stack(out, axis=0)


def rope_tables(s):
    half = MLA_ROPE // 2
    inv = ROPE_THETA ** (-jnp.arange(half, dtype=F32) / half)
    ang = jnp.arange(s).astype(F32)[:, None] * inv[None, :]
    cos, sin = jnp.cos(ang), jnp.sin(ang)
    zero = jnp.zeros((s, LANE - MLA_ROPE), F32)
    return (jnp.concatenate([cos, cos, zero], axis=1), jnp.concatenate([-sin, sin, zero], axis=1))


def _rope_tile(t, c, s):
    half = MLA_ROPE // 2
    swapped = pltpu.roll(t, half, axis=1) + pltpu.roll(t, LANE - half, axis=1)
    return t * c + swapped * s


def _flash_kernel(lam_ref, qt_ref, k_ref, vt_ref, *rest, diff, t, scale, post_scale):
    if diff:
        bias_ref, g_ref, o_ref, m_sc, l_sc, acc_sc, s_sc = rest
    else:
        o_ref, m_sc, l_sc, acc_sc, s_sc = rest
    i = pl.program_id(2)
    m_sc[...] = jnp.full(m_sc.shape, NEG, F32)
    l_sc[...] = jnp.zeros(l_sc.shape, F32)
    acc_sc[...] = jnp.zeros(acc_sc.shape, F32)
    qts = [qt_ref[0], qt_ref[1]] if diff else [qt_ref[:, 0:t], qt_ref[:, t:2 * t]]

    def load_kv(j):
        start = pl.multiple_of(j * t, t)
        return k_ref[pl.ds(start, t), :], vt_ref[:, pl.ds(start, t)]

    ntiles = k_ref.shape[0] // t

    def scores_of(work):
        return [jnp.dot(kv[0], qts[c], preferred_element_type=F32) for c, kv, _, _ in work]

    def consume(work, scores):
        for (c, kv, bias, diagonal), s in zip(work, scores):
            if scale != 1.0:
                s = s * scale
            if bias is not None:
                s = bias + s
            elif diagonal:
                key_i = lax.broadcasted_iota(jnp.int32, (t, t), 0)
                qry_i = lax.broadcasted_iota(jnp.int32, (t, t), 1)
                s = jnp.where(key_i <= qry_i, s, NEG)
            m_prev = m_sc[c]
            m_new = jnp.maximum(m_prev, jnp.max(s, axis=0, keepdims=True))
            alpha = jnp.exp(m_prev - m_new)
            p = jnp.exp(s - m_new)
            l_sc[c] = alpha * l_sc[c] + jnp.sum(p, axis=0, keepdims=True)
            m_sc[c] = m_new
            acc_sc[c] = alpha * acc_sc[c] + jnp.dot(kv[1], p.astype(BF), preferred_element_type=F32)

    def step(work):
        consume(work, scores_of(work))

    def sweep(n, work):
        gsz = FLASH_GROUP
        ng = n // gsz

        def group(jj):
            items = []
            for uu in range(gsz):
                items = items + work(gsz * jj + uu)
            return items

        def park(jj, slot):
            first = jnp.minimum(gsz * jj, ntiles - gsz)
            for uu in range(gsz):
                k_tile = load_kv(first + uu)[0]
                for c in range(2):
                    s_sc[slot, 2 * uu + c] = jnp.dot(k_tile, qts[c], preferred_element_type=F32)

        def consume_parked(jj, slot):
            items = group(jj)
            consume(items, [s_sc[slot, idx] for idx in range(len(items))])

        @pl.when(ng > 0)
        def _():
            park(0, 0)

        def body(j2, carry):
            park(2 * j2 + 1, 1)
            consume_parked(2 * j2, 0)
            park(2 * j2 + 2, 0)
            consume_parked(2 * j2 + 1, 1)
            return carry

        lax.fori_loop(0, ng // 2, body, 0)

        @pl.when(ng % 2 == 1)
        def _():
            consume_parked(ng - 1, 0)

        for rr in range(gsz - 1):
            @pl.when(n % gsz > rr)
            def _(rr=rr):
                step(work(ng * gsz + rr))

    if diff:
        def work(j):
            kv = load_kv(j)
            bias = bias_ref[i - j]
            return [(0, kv, bias, False), (1, kv, bias, False)]

        sweep(i + 1, work)
        out_t = acc_sc[0] / l_sc[0] - lam_ref[0] * (acc_sc[1] / l_sc[1])
        o = out_t.T
        ms = jnp.mean(o * o, axis=-1, keepdims=True)
        o_ref[...] = (o * lax.rsqrt(ms + EPS) * g_ref[...] * post_scale).astype(o_ref.dtype)
    else:
        def work(j):
            kv = load_kv(j)
            return [(0, kv, None, False), (1, kv, None, False)]

        sweep(2 * i, work)
        kv = load_kv(2 * i)
        step([(0, kv, None, True), (1, kv, None, False)])
        step([(1, load_kv(2 * i + 1), None, True)])
        o_ref[0:t, :] = (acc_sc[0] / l_sc[0]).T.astype(o_ref.dtype)
        o_ref[t:2 * t, :] = (acc_sc[1] / l_sc[1]).T.astype(o_ref.dtype)


def flash_attention(q, k, vt, s, *, diff, dk, scale, bias=None, lam=None, gain=None, post_scale=1.0, name):
    m = k.shape[0]
    b = m // s
    t = min(ATT_T, s)
    lam = jnp.zeros((1,), F32) if lam is None else lam.reshape(1).astype(F32)
    if diff:
        nq = s // t
        q_spec = pl.BlockSpec((2, dk, t), lambda bb, h, i: (0, h, bb * nq + i))
        o_rows = t
    else:
        nq = s // (2 * t)
        q_spec = pl.BlockSpec((dk, 2 * t), lambda bb, h, i: (h, bb * nq + i))
        o_rows = 2 * t
    in_specs = [pl.BlockSpec(memory_space=pltpu.SMEM), q_spec,
                pl.BlockSpec((s, dk), lambda bb, h, i: (bb, h)),
                pl.BlockSpec((LANE, s), lambda bb, h, i: (h, bb))]
    args = [lam, q, k, vt]
    if diff:
        in_specs += [pl.BlockSpec((None, nq, t, t), lambda bb, h, i: (h, 0, 0, 0)),
                     pl.BlockSpec((1, LANE), lambda bb, h, i: (0, 0))]
        args += [bias, gain.reshape(1, LANE).astype(F32)]
    return pl.pallas_call(
        functools.partial(_flash_kernel, diff=diff, t=t, scale=scale, post_scale=post_scale),
        grid=(b, HEADS, nq),
        in_specs=in_specs,
        out_specs=pl.BlockSpec((o_rows, LANE), lambda bb, h, i: (bb * nq + i, h)),
        out_shape=jax.ShapeDtypeStruct((m, HEADS * LANE), BF),
        scratch_shapes=[pltpu.VMEM((2, 1, t), F32), pltpu.VMEM((2, 1, t), F32), pltpu.VMEM((2, LANE, t), F32),
                        pltpu.VMEM((2, 2 * FLASH_GROUP, t, t), F32)],
        compiler_params=_cparams(("parallel", "parallel", "arbitrary")),
        name=name,
    )(*args)


def _prep_a_kernel(z_ref, qg_ref, kg_ref, qz_ref, kn_ref, vt_ref):
    lane = lax.broadcasted_iota(jnp.int32, (1, LANE), 1)
    lo = lane < DA_QK_DIM

    def norm_halves(x, g):
        x2 = x * x
        s_lo = jnp.sum(jnp.where(lo, x2, 0.0), axis=-1, keepdims=True)
        s_hi = jnp.sum(jnp.where(lo, 0.0, x2), axis=-1, keepdims=True)
        inv = lax.rsqrt(jnp.where(lo, s_lo, s_hi) * (1.0 / DA_QK_DIM) + EPS)
        return x * inv * g

    scale = DA_QK_DIM ** -0.5
    for h in range(HEADS):
        cq = slice(h * LANE, (h + 1) * LANE)
        ck = slice(HEADS * LANE + h * LANE, HEADS * LANE + (h + 1) * LANE)
        q = norm_halves(z_ref[:, cq], qg_ref[...]) * scale
        qt = q.T
        sub_lo = lax.broadcasted_iota(jnp.int32, (LANE, 1), 0) < DA_QK_DIM
        qz_ref[0, cq, :] = jnp.where(sub_lo, qt, 0.0).astype(BF)
        qz_ref[1, cq, :] = jnp.where(sub_lo, 0.0, qt).astype(BF)
        kn_ref[:, cq] = norm_halves(z_ref[:, ck], kg_ref[...]).astype(BF)
        cv = slice(2 * HEADS * LANE + h * LANE, 2 * HEADS * LANE + (h + 1) * LANE)
        vt_ref[cq, :] = z_ref[:, cv].T.astype(BF)


def prep_a(z, q_g, k_g, bm=256):
    m = z.shape[0]
    bm = min(bm, m)
    w = HEADS * LANE
    g2 = lambda g: jnp.concatenate([g, g]).reshape(1, LANE).astype(F32)
    return pl.pallas_call(
        _prep_a_kernel,
        grid=(m // bm,),
        in_specs=[pl.BlockSpec((bm, DA_COLS), lambda i: (i, Z_A // DA_COLS)),
                  pl.BlockSpec((1, LANE), lambda i: (0, 0)),
                  pl.BlockSpec((1, LANE), lambda i: (0, 0))],
        out_specs=[pl.BlockSpec((2, w, bm), lambda i: (0, 0, i)),
                   pl.BlockSpec((bm, w), lambda i: (i, 0)),
                   pl.BlockSpec((w, bm), lambda i: (0, i))],
        out_shape=[jax.ShapeDtypeStruct((2, w, m), BF), jax.ShapeDtypeStruct((m, w), BF),
                   jax.ShapeDtypeStruct((w, m), BF)],
        compiler_params=_cparams(("parallel",)),
        name="prep_a",
    )(z, g2(q_g), g2(k_g))


def mixer_a(z, s, lam, lam_init, q_g, k_g, o_g, bias_tiles):
    qz, kn, vt = prep_a(z, q_g, k_g)
    return flash_attention(qz, kn, vt, s, diff=True, dk=LANE, scale=1.0, bias=bias_tiles, lam=lam, gain=o_g,
                           post_scale=1.0 - lam_init, name="diff_attention")


def _prep_b_kernel(zq_ref, zkv_ref, zkr_ref, wq_ref, wkv_ref, cq_g, ckv_g, kr_g, qn_g, qr_g, kn_g, c_ref, s_ref,
                   qc_ref, kc_ref, vt_ref):
    def norm(x, g, width):
        ms = jnp.sum(x * x, axis=-1, keepdims=True) * (1.0 / width)
        return x * lax.rsqrt(ms + EPS) * g

    cos, sin = c_ref[...], s_ref[...]
    c_q = norm(zq_ref[...], cq_g[...], MLA_Q_RANK).astype(BF)
    c_kv = norm(zkv_ref[...], ckv_g[...], MLA_KV_RANK).astype(BF)
    q_up = jnp.dot(c_q, wq_ref[...], preferred_element_type=F32)
    kv_up = jnp.dot(c_kv, wkv_ref[...], preferred_element_type=F32)
    kr = _rope_tile(norm(zkr_ref[...], kr_g[...], MLA_ROPE), cos, sin).astype(BF)
    for h in range(HEADS):
        c0 = slice(2 * h * LANE, (2 * h + 1) * LANE)
        c1 = slice((2 * h + 1) * LANE, (2 * h + 2) * LANE)
        qc_ref[c0, :] = norm(q_up[:, c0], qn_g[...], MLA_NOPE).T.astype(BF)
        qc_ref[c1, :] = _rope_tile(norm(q_up[:, c1], qr_g[...], MLA_ROPE), cos, sin).T.astype(BF)
        kc_ref[:, c0] = norm(kv_up[:, c0], kn_g[...], MLA_NOPE).astype(BF)
        kc_ref[:, c1] = kr
        vt_ref[h * LANE:(h + 1) * LANE, :] = kv_up[:, c1].T.astype(BF)


def prep_b(z, w_uq, w_ukv, layer, cq_g, ckv_g, kr_g, qn_g, qr_g, kn_g, cos_t, sin_t, s, bm=256):
    m = z.shape[0]
    bm = min(bm, s)
    nsb = s // bm
    w2 = 2 * HEADS * LANE
    row = lambda g: g.reshape(1, -1).astype(F32)
    pad = lambda g: jnp.concatenate([g, jnp.zeros((LANE - MLA_ROPE,), g.dtype)])
    const = lambda i: (0, 0)
    return pl.pallas_call(
        _prep_b_kernel,
        grid=(m // bm,),
        in_specs=[pl.BlockSpec((bm, MLA_Q_RANK), lambda i: (i, Z_B // MLA_Q_RANK)),
                  pl.BlockSpec((bm, MLA_KV_RANK), lambda i: (i, (Z_B + MLA_Q_RANK) // MLA_KV_RANK)),
                  pl.BlockSpec((bm, LANE), lambda i: (i, (Z_B + MLA_Q_RANK + MLA_KV_RANK) // LANE)),
                  pl.BlockSpec((MLA_Q_RANK, w2), const),
                  pl.BlockSpec((None, MLA_KV_RANK, w2), lambda i: (layer, 0, 0)),
                  pl.BlockSpec((1, MLA_Q_RANK), const),
                  pl.BlockSpec((1, MLA_KV_RANK), const),
                  pl.BlockSpec((1, LANE), const),
                  pl.BlockSpec((1, LANE), const),
                  pl.BlockSpec((1, LANE), const),
                  pl.BlockSpec((1, LANE), const),
                  pl.BlockSpec((bm, LANE), lambda i: (i % nsb, 0)),
                  pl.BlockSpec((bm, LANE), lambda i: (i % nsb, 0))],
        out_specs=[pl.BlockSpec((w2, bm), lambda i: (0, i)),
                   pl.BlockSpec((bm, w2), lambda i: (i, 0)),
                   pl.BlockSpec((HEADS * LANE, bm), lambda i: (0, i))],
        out_shape=[jax.ShapeDtypeStruct((w2, m), BF), jax.ShapeDtypeStruct((m, w2), BF),
                   jax.ShapeDtypeStruct((HEADS * LANE, m), BF)],
        compiler_params=_cparams(("parallel",)),
        name="prep_b",
    )(z, z, z, w_uq, w_ukv, row(cq_g), row(ckv_g), row(pad(kr_g)), row(qn_g), row(pad(qr_g)), row(kn_g),
      cos_t, sin_t)


def pack_w_uq(w):
    w = w.reshape(MLA_Q_RANK, HEADS, MLA_NOPE + MLA_ROPE)
    w = jnp.pad(w, ((0, 0), (0, 0), (0, 2 * LANE - MLA_NOPE - MLA_ROPE)))
    return w.reshape(MLA_Q_RANK, HEADS * 2 * LANE).astype(BF)


def mixer_b(z, s, w_uq, w_ukv, cq_g, ckv_g, qn_g, kn_g, qr_g, kr_g, cos_t, sin_t, layer=0):
    qc, kc, vt = prep_b(z, pack_w_uq(w_uq), w_ukv, layer, cq_g, ckv_g, kr_g, qn_g, qr_g, kn_g, cos_t, sin_t, s)
    return flash_attention(qc, kc, vt, s, diff=False, dk=2 * LANE, scale=(MLA_NOPE + MLA_ROPE) ** -0.5,
                           name="mla_attention")


DIL_PAD = DIL_STEPS * max(d for _, d in DIL_GROUPS)


def _dil_kernel(zq_ref, zk_ref, zv_ref, bias_ref, qg_ref, kg_ref, o_ref, q_sc, k_sc, v_sc, og_sc, lse_sc, *, s):
    g = pl.program_id(2)
    ngroups = len(DIL_GROUPS)

    def norm(x, gain):
        ms = jnp.mean(x * x, axis=-1, keepdims=True)
        return x * lax.rsqrt(ms + EPS) * gain

    zeros = jnp.zeros((DIL_PAD, LANE), F32)
    k_sc[0:DIL_PAD, :] = zeros
    v_sc[0:DIL_PAD, :] = zeros
    q_sc[...] = norm(zq_ref[...], qg_ref[...])
    k_sc[DIL_PAD:, :] = norm(zk_ref[...], kg_ref[...])
    v_sc[DIL_PAD:, :] = zv_ref[...]
    scale = LANE ** -0.5

    def group(gi, dil):
        nsub = s // DIL_STEPS

        def body(tt, carry):
            c = tt % dil
            n = tt // dil
            q0 = c + dil * DIL_STEPS * n
            rows = pl.ds(q0, DIL_STEPS, stride=dil) if dil > 1 else pl.ds(q0, DIL_STEPS)
            k0 = q0 + DIL_PAD - dil * DIL_STEPS
            band = pl.ds(k0, 2 * DIL_STEPS, stride=dil) if dil > 1 else pl.ds(k0, 2 * DIL_STEPS)
            qs = q_sc[rows, :].astype(BF)
            ks = k_sc[band, :].astype(BF)
            vs = v_sc[band, :].astype(BF)
            first = jnp.where(n == 0, 1, 0)
            logits = lax.dot_general(qs, ks, (((1,), (1,)), ((), ())), preferred_element_type=F32) * scale
            logits = logits + bias_ref[first]
            mx = jnp.max(logits, axis=-1, keepdims=True)
            e = jnp.exp(logits - mx)
            den = jnp.sum(e, axis=-1, keepdims=True)
            o = jnp.dot((e / den).astype(BF), vs, preferred_element_type=F32)
            og_sc[gi, rows, :] = o
            lse_sc[gi, rows, :] = jnp.broadcast_to(mx + jnp.log(den), (DIL_STEPS, LANE))
            return carry

        lax.fori_loop(0, nsub, body, 0, unroll=16)

    for gi, (_, dil) in enumerate(DIL_GROUPS):
        pl.when(g == gi)(functools.partial(group, gi, dil))

    @pl.when(g == ngroups - 1)
    def _():
        lses = [lse_sc[gi] for gi in range(ngroups)]
        mx = functools.reduce(jnp.maximum, lses)
        ws = [jnp.exp(l - mx) for l in lses]
        tot = functools.reduce(lambda a, b2: a + b2, ws)
        acc = ws[0] * og_sc[0]
        for gi in range(1, ngroups):
            acc = acc + ws[gi] * og_sc[gi]
        o_ref[...] = (acc / tot).astype(o_ref.dtype)


def mixer_c(z, s, q_g, k_g, bias_c):
    m = z.shape[0]
    b = m // s
    ngroups = len(DIL_GROUPS)
    cb = Z_C // LANE

    def col(which):
        return lambda bb, h, g: (bb, cb + (g * 3 + which) * HEADS + h)

    row = lambda g: g.reshape(1, LANE).astype(F32)
    return pl.pallas_call(
        functools.partial(_dil_kernel, s=s),
        grid=(b, HEADS, ngroups),
        in_specs=[pl.BlockSpec((s, LANE), col(0)),
                  pl.BlockSpec((s, LANE), col(1)),
                  pl.BlockSpec((s, LANE), col(2)),
                  pl.BlockSpec((None, None, 2, DIL_STEPS, 2 * DIL_STEPS), lambda bb, h, g: (g, h, 0, 0, 0)),
                  pl.BlockSpec((1, LANE), lambda bb, h, g: (0, 0)),
                  pl.BlockSpec((1, LANE), lambda bb, h, g: (0, 0))],
        out_specs=pl.BlockSpec((s, LANE), lambda bb, h, g: (bb, h)),
        out_shape=jax.ShapeDtypeStruct((m, HEADS * LANE), BF),
        scratch_shapes=[pltpu.VMEM((s, LANE), F32), pltpu.VMEM((DIL_PAD + s, LANE), F32),
                        pltpu.VMEM((DIL_PAD + s, LANE), F32), pltpu.VMEM((ngroups, s, LANE), F32),
                        pltpu.VMEM((ngroups, s, LANE), F32)],
        compiler_params=_cparams(("parallel", "parallel", "arbitrary")),
        name="dilated_attention",
    )(z, z, z, bias_c, row(q_g), row(k_g))


CONV_PAD = 8
GDN_GROUP = 4
GDN_BASE = 8


def _gdn_kernel(par_ref, zq_ref, zk_ref, zv_ref, zg_ref, zab_ref, wq_ref, wk_ref, wv_ref, og_ref, o_ref,
                x_sc, q_sc, k_sc, v_sc, g_sc, b_sc, u_sc, w_sc, a_sc, st_sc, *, s):
    h = pl.program_id(1)
    c = GDN_CHUNK

    def conv_silu(z_ref, w_ref):
        x_sc[0:CONV_PAD, :] = jnp.zeros((CONV_PAD, LANE), F32)
        x_sc[CONV_PAD:, :] = z_ref[...]
        y = x_sc[CONV_PAD:, :] * w_ref[CONV_W - 1:CONV_W, :]
        for i in range(CONV_W - 1):
            off = CONV_PAD - (CONV_W - 1) + i
            y = y + x_sc[off:off + s, :] * w_ref[i:i + 1, :]
        return y * jax.nn.sigmoid(y)

    def l2(x):
        return x * lax.rsqrt(jnp.sum(x * x, axis=-1, keepdims=True) + EPS)

    q_sc[...] = l2(conv_silu(zq_ref, wq_ref)) * (LANE ** -0.5)
    k_sc[...] = l2(conv_silu(zk_ref, wk_ref))
    v_sc[...] = conv_silu(zv_ref, wv_ref)
    lane = lax.broadcasted_iota(jnp.int32, (1, LANE), 1)
    ab = zab_ref[...]
    a_col = jnp.sum(jnp.where(lane == h, ab, 0.0), axis=-1, keepdims=True)
    b_col = jnp.sum(jnp.where(lane == h + HEADS, ab, 0.0), axis=-1, keepdims=True)
    a_neg_exp = par_ref[0, h]
    dt_bias = par_ref[1, h]
    g_sc[...] = jnp.broadcast_to(a_neg_exp * jax.nn.softplus(a_col + dt_bias), (s, LANE))
    b_sc[...] = jnp.broadcast_to(jax.nn.sigmoid(b_col), (s, LANE))
    st_sc[...] = jnp.zeros(st_sc.shape, F32)

    gr = GDN_GROUP * c
    r_i = lax.broadcasted_iota(jnp.int32, (gr, gr), 0)
    c_i = lax.broadcasted_iota(jnp.int32, (gr, gr), 1)
    blk = {}
    size = GDN_BASE
    while size <= c:
        blk[size] = (r_i // size) == (c_i // size)
        size *= 2
    same = blk[c]
    tril = same & (r_i >= c_i)
    strict = same & (r_i > c_i)
    tril_b = tril.astype(BF)
    triu_b = (same & (r_i <= c_i)).astype(BF)
    eye = (r_i == c_i).astype(F32)

    def mm(a, b2):
        return jnp.dot(a.astype(BF), b2.astype(BF), preferred_element_type=F32)

    def mm_t(a, b2):
        return lax.dot_general(a.astype(BF), b2.astype(BF), (((1,), (1,)), ((), ())), preferred_element_type=F32)

    def split2(x):
        hi = x.astype(BF)
        return hi, (x - hi.astype(F32)).astype(BF)

    def split3(x):
        hi = x.astype(BF)
        r1 = x - hi.astype(F32)
        mid = r1.astype(BF)
        return hi, mid, (r1 - mid.astype(F32)).astype(BF)

    def mm_hi(a, b2):
        a_hi, a_lo = split2(a)
        b_hi, b_lo = split2(b2)
        d = functools.partial(jnp.dot, preferred_element_type=F32)
        return d(a_hi, b_hi) + (d(a_hi, b_lo) + d(a_lo, b_hi))

    def widen(x):
        return jnp.concatenate([x] * (gr // LANE), axis=1)

    def local_group(gi, carry):
        base = pl.multiple_of(gi * gr, gr)
        rows = pl.ds(base, gr)
        q = q_sc[rows, :]
        k = k_sc[rows, :]
        beta = b_sc[rows, :]
        parts = split3(g_sc[rows, :])
        gc = sum(jnp.dot(tril_b, part, preferred_element_type=F32) for part in parts)
        g_row = sum(lax.dot_general(widen(part), triu_b, (((0,), (0,)), ((), ())), preferred_element_type=F32)
                    for part in parts)
        decay = jnp.exp(jnp.where(tril, widen(gc) - g_row, NEG))
        kb = k * beta
        vb = v_sc[rows, :] * beta
        lower = jnp.where(strict, mm_t(kb, k) * decay, 0.0)
        neg = -jnp.where(blk[GDN_BASE], lower, 0.0)
        p1 = mm_hi(neg, neg)
        tmat = eye + neg
        tmat = tmat + mm_hi(tmat, p1)
        tmat = tmat + mm_hi(tmat, mm_hi(p1, p1))
        size = GDN_BASE
        while size < c:
            off = jnp.where(blk[2 * size] & jnp.logical_not(blk[size]), lower, 0.0)
            tmat = tmat - mm_hi(mm_hi(tmat, off), tmat)
            size *= 2
        eg = jnp.exp(gc)
        u_sc[rows, :] = mm(tmat, vb)
        w_sc[rows, :] = mm(tmat, kb * eg)
        intra = mm_t(q, k) * decay
        q_sc[rows, :] = q * eg
        for uu in range(GDN_GROUP):
            sl = slice(uu * c, (uu + 1) * c)
            crow = pl.ds(base + uu * c, c)
            g_last = gc[(uu + 1) * c - 1:(uu + 1) * c, :]
            a_sc[crow, 0:c] = intra[sl, sl]
            k_sc[crow, :] = k[sl] * jnp.exp(g_last - gc[sl])
            g_sc[crow, :] = jnp.broadcast_to(jnp.exp(g_last), (c, LANE))
        return carry


    def scan(n, carry):
        rows = pl.ds(pl.multiple_of(n * c, c), c)
        state = st_sc[...]
        v_new = u_sc[rows, :] - mm(w_sc[rows, :], state)
        o = mm(q_sc[rows, :], state) + mm(a_sc[rows, 0:c], v_new)
        decay_last = g_sc[pl.ds(pl.multiple_of(n * c, c), 1), :]
        st_sc[...] = state * decay_last + lax.dot_general(
            k_sc[rows, :].astype(BF), v_new.astype(BF), (((0,), (0,)), ((), ())), preferred_element_type=F32)
        u_sc[rows, :] = o
        return carry

    ngroups = s // gr
    local_group(0, 0)

    def pipelined(g, carry):
        local_group(g + 1, 0)
        for uu in range(GDN_GROUP):
            scan(g * GDN_GROUP + uu, 0)
        return carry

    lax.fori_loop(0, ngroups - 1, pipelined, 0)
    for uu in range(GDN_GROUP):
        scan((ngroups - 1) * GDN_GROUP + uu, 0)
    o = u_sc[...]
    ms = jnp.mean(o * o, axis=-1, keepdims=True)
    gate = zg_ref[...]
    o_ref[...] = (o * lax.rsqrt(ms + EPS) * og_ref[...] * (gate * jax.nn.sigmoid(gate))).astype(o_ref.dtype)


def mixer_d(z, s, conv_w, a_log, dt_bias, o_g):
    m = z.shape[0]
    b = m // s
    cb = Z_D // LANE
    par = jnp.stack([-jnp.exp(a_log.astype(F32)), dt_bias.astype(F32)], axis=0)
    cw = conv_w.astype(F32)

    def col(which):
        return lambda bb, h: (bb, cb + which * HEADS + h)

    return pl.pallas_call(
        functools.partial(_gdn_kernel, s=s),
        grid=(b, HEADS),
        in_specs=[pl.BlockSpec(memory_space=pltpu.SMEM),
                  pl.BlockSpec((s, LANE), col(0)),
                  pl.BlockSpec((s, LANE), col(1)),
                  pl.BlockSpec((s, LANE), col(2)),
                  pl.BlockSpec((s, LANE), col(3)),
                  pl.BlockSpec((s, LANE), lambda bb, h: (bb, cb + 4 * HEADS)),
                  pl.BlockSpec((CONV_W, LANE), lambda bb, h: (0, h)),
                  pl.BlockSpec((CONV_W, LANE), lambda bb, h: (0, HEADS + h)),
                  pl.BlockSpec((CONV_W, LANE), lambda bb, h: (0, 2 * HEADS + h)),
                  pl.BlockSpec((1, LANE), lambda bb, h: (0, 0))],
        out_specs=pl.BlockSpec((s, LANE), lambda bb, h: (bb, h)),
        out_shape=jax.ShapeDtypeStruct((m, HEADS * LANE), BF),
        scratch_shapes=[pltpu.VMEM((CONV_PAD + s, LANE), F32), pltpu.VMEM((s, LANE), F32),
                        pltpu.VMEM((s, LANE), F32), pltpu.VMEM((s, LANE), F32), pltpu.VMEM((s, LANE), F32),
                        pltpu.VMEM((s, LANE), F32), pltpu.VMEM((s, LANE), F32), pltpu.VMEM((s, LANE), F32),
                        pltpu.VMEM((s, LANE), F32), pltpu.VMEM((LANE, LANE), F32)],
        compiler_params=_cparams(("parallel", "arbitrary")),
        name="gated_deltanet",
    )(par, z, z, z, z, z, cw, cw, cw, o_g.reshape(1, LANE).astype(F32))


def _cast_kernel(x_ref, o_ref):
    o_ref[...] = x_ref[...].astype(o_ref.dtype)


def cast_bf16(w, bm=512):
    r, c = w.shape
    bm = min(bm, r)
    return pl.pallas_call(
        _cast_kernel,
        grid=(r // bm,),
        in_specs=[pl.BlockSpec((bm, c), lambda i: (i, 0))],
        out_specs=pl.BlockSpec((bm, c), lambda i: (i, 0)),
        out_shape=jax.ShapeDtypeStruct((r, c), BF),
        compiler_params=_cparams(("parallel",)),
        name="cast_bf16",
    )(w)


IN_COLS = DA_COLS + MLA_COLS + DIL_COLS + GDN_QKV + 2 * HEADS + HEADS * LANE


def _pack_w_in_kernel(w_ref, o_ref):
    rows = w_ref.shape[0]

    def put(dst, src, width):
        o_ref[:, dst:dst + width] = w_ref[:, src:src + width]

    def zero(dst, width):
        o_ref[:, dst:dst + width] = jnp.zeros((rows, width), BF)

    c0 = DA_COLS + MLA_COLS + DIL_COLS
    put(0, 0, DA_COLS + MLA_COLS)
    zero(Z_B + MLA_COLS, Z_C - Z_B - MLA_COLS)
    put(Z_C, DA_COLS + MLA_COLS, DIL_COLS)
    put(Z_D, c0, GDN_QKV)
    put(Z_D + GDN_QKV, c0 + GDN_QKV + 2 * HEADS, HEADS * LANE)
    put(Z_D + GDN_QKV + HEADS * LANE, c0 + GDN_QKV, 2 * HEADS)
    used = Z_D + GDN_QKV + HEADS * LANE + 2 * HEADS
    zero(used, Z_COLS - used)


def pack_w_in(w, bm=256):
    r = w.shape[0]
    return pl.pallas_call(
        _pack_w_in_kernel,
        grid=(r // bm,),
        in_specs=[pl.BlockSpec((bm, IN_COLS), lambda i: (i, 0))],
        out_specs=pl.BlockSpec((bm, Z_COLS), lambda i: (i, 0)),
        out_shape=jax.ShapeDtypeStruct((r, Z_COLS), BF),
        compiler_params=_cparams(("parallel",)),
        name="pack_w_in",
    )(w)


def _cast_stacked(w, bm=512):
    lead, c = w.shape[:-1], w.shape[-1]
    return cast_bf16(w.reshape(-1, c), bm).reshape(*lead, c)


def kernel(x, p, rel_bias, norm_mix_g, w_in, da_lambda, da_q_g, da_k_g, da_o_g, mla_w_uq, mla_w_ukv, mla_cq_g, mla_ckv_g, mla_qn_g, mla_kn_g, mla_qr_g, mla_kr_g, dil_q_g, dil_k_g, gdn_conv_w, gdn_a_log, gdn_dt_bias, gdn_o_g, w_bgate, b_bgate, w_branch, w_out, norm_ffn_g, w_ffn_in, w_ffn_out, norm_ple_g, w_ple, w_ple_gate):
    b, s, d = x.shape
    m = b * s
    bias_a = diff_bias_tiles(rel_bias, s)
    bias_c = dil_bias_tiles(rel_bias)
    cos_t, sin_t = rope_tables(s)
    w_in_p = pack_w_in(w_in.astype(BF).reshape(DEPTH * d, IN_COLS)).reshape(DEPTH, d, Z_COLS)
    w_branch_b = _cast_stacked(w_branch)
    w_out_b = _cast_stacked(w_out)
    w_ffn_out_b = _cast_stacked(w_ffn_out)
    w_ple_b = _cast_stacked(w_ple)
    w_ukv_b = _cast_stacked(mla_w_ukv)
    p_r = p.reshape(DEPTH, m, PLE_DIM)
    xr = x.reshape(m, d)
    for i in range(DEPTH):
        h = rmsnorm_rows(xr, norm_mix_g[i])
        z = matmul(h, w_in_p, F32, layer=i, name="in_proj")
        lam_init = 0.8 - 0.6 * math.exp(-0.3 * i)
        lq1, lk1, lq2, lk2 = (da_lambda[i, j].astype(F32) for j in range(4))
        lam = jnp.exp(jnp.sum(lq1 * lk1)) - jnp.exp(jnp.sum(lq2 * lk2)) + lam_init
        o_a = mixer_a(z, s, lam, lam_init, da_q_g[i], da_k_g[i], da_o_g[i], bias_a)
        o_b = mixer_b(z, s, mla_w_uq[i], w_ukv_b, mla_cq_g[i], mla_ckv_g[i], mla_qn_g[i], mla_kn_g[i],
                      mla_qr_g[i], mla_kr_g[i], cos_t, sin_t, layer=i)
        o_c = mixer_c(z, s, dil_q_g[i], dil_k_g[i], bias_c)
        o_d = mixer_d(z, s, gdn_conv_w[i], gdn_a_log[i], gdn_dt_bias[i], gdn_o_g[i])
        o_all = jnp.stack([o_a, o_b, o_c, o_d], axis=0)
        merged = gated_merge(h, w_bgate, b_bgate[i].reshape(N_BRANCH, 1, d), o_all, w_branch_b, i)
        xr = matmul_residual(xr, merged, w_out_b, bn=1024, layer=i, name="out_proj")
        hf = rmsnorm_rows(xr, norm_ffn_g[i])
        act = swiglu_in(hf, w_ffn_in, i)
        xr = matmul_residual(xr, act, w_ffn_out_b, bm=512, bn=512, layer=i, name="ffn_out")
        hp = rmsnorm_rows(xr, norm_ple_g[i])
        xr = ple_update(xr, p_r, w_ple_b, hp, w_ple_gate, i)
    return xr.reshape(b, s, d)
```

```python
import functools
import math

import numpy as np
import jax
import jax.numpy as jnp
from jax import lax
from jax.experimental import pallas as pl
from jax.experimental.pallas import tpu as pltpu

DEPTH = 2
EPS = 1e-6
PLE_DIM = 256
HEADS = 8
N_BRANCH = 4
BRANCH_W = 1024
DA_QK_DIM = 64
MLA_Q_RANK = 1024
MLA_KV_RANK = 512
MLA_NOPE = 128
MLA_ROPE = 64
ROPE_THETA = 10000.0
DIL_GROUPS = ((128, 1), (512, 4), (2048, 16))
DIL_STEPS = 128
GDN_CHUNK = 64
CONV_W = 4
N_BUCKETS = 32
MAX_DIST = 2048

DA_COLS = 3072
MLA_COLS = 1600
DIL_COLS = 9216
GDN_QKV = 3072

LANE = 128
Z_A = 0
Z_B = Z_A + DA_COLS
Z_C = Z_B + MLA_COLS + 64
Z_D = Z_C + DIL_COLS
Z_COLS = 18 * 1024

NEG = -1e30
LOG2E = math.log2(math.e)
ATT_T = 256
FLASH_GROUP = 2
VMEM_LIMIT = 56 * 1024 * 1024
BF = jnp.bfloat16
F32 = jnp.float32


def _cparams(sem):
    return pltpu.CompilerParams(dimension_semantics=sem, vmem_limit_bytes=VMEM_LIMIT)


def _rmsnorm_kernel(x_ref, g_ref, o_ref):
    x = x_ref[...]
    ms = jnp.mean(x * x, axis=-1, keepdims=True)
    o_ref[...] = (x * lax.rsqrt(ms + EPS) * g_ref[...]).astype(o_ref.dtype)


def rmsnorm_rows(x, g, bm=512):
    m = x.shape[0]
    d = g.shape[-1]
    return pl.pallas_call(
        _rmsnorm_kernel,
        grid=(m // bm,),
        in_specs=[pl.BlockSpec((bm, d), lambda i: (i, 0)),
                  pl.BlockSpec((1, d), lambda i: (0, 0))],
        out_specs=pl.BlockSpec((bm, d), lambda i: (i, 0)),
        out_shape=jax.ShapeDtypeStruct((m, d), BF),
        compiler_params=_cparams(("parallel",)),
        name="rmsnorm_rows",
    )(x, g.reshape(1, d))


def _mm_kernel(x_ref, w_ref, o_ref):
    o_ref[...] = jnp.dot(x_ref[...], w_ref[...], preferred_element_type=F32).astype(o_ref.dtype)


def _layer_spec(w, block, index_map, layer):
    if layer is None:
        return pl.BlockSpec(block, index_map)
    return pl.BlockSpec((None,) + block, lambda *g: (layer,) + index_map(*g))


def matmul(x, w, out_dtype, bm=1024, bn=1024, layer=None, name="matmul"):
    m, k = x.shape
    n = w.shape[-1]
    bn = min(bn, n)
    bm = min(bm, m)
    return pl.pallas_call(
        _mm_kernel,
        grid=(m // bm, n // bn),
        in_specs=[pl.BlockSpec((bm, k), lambda i, j: (i, 0)),
                  _layer_spec(w, (k, bn), lambda i, j: (0, j), layer)],
        out_specs=pl.BlockSpec((bm, bn), lambda i, j: (i, j)),
        out_shape=jax.ShapeDtypeStruct((m, n), out_dtype),
        compiler_params=_cparams(("parallel", "parallel")),
        name=name,
    )(x, w)


def _mm_residual_kernel(r_ref, x_ref, w_ref, o_ref, *acc, nk):
    part = jnp.dot(x_ref[...], w_ref[...], preferred_element_type=F32)
    if nk == 1:
        o_ref[...] = r_ref[...] + part
    else:
        acc_ref, = acc

        @pl.when(pl.program_id(2) == 0)
        def _():
            acc_ref[...] = r_ref[...]

        total = acc_ref[...] + part
        acc_ref[...] = total
        o_ref[...] = total


def matmul_residual(r, x, w, bm=1024, bn=512, bk=None, layer=None, name="matmul_residual"):
    m, kdim = x.shape
    n = w.shape[-1]
    bk = kdim if bk is None else bk
    bm = min(bm, m)
    nk = kdim // bk
    return pl.pallas_call(
        functools.partial(_mm_residual_kernel, nk=nk),
        grid=(m // bm, n // bn, nk),
        in_specs=[pl.BlockSpec((bm, bn), lambda i, j, k: (i, j)),
                  pl.BlockSpec((bm, bk), lambda i, j, k: (i, k)),
                  _layer_spec(w, (bk, bn), lambda i, j, k: (k, j), layer)],
        out_specs=pl.BlockSpec((bm, bn), lambda i, j, k: (i, j)),
        out_shape=jax.ShapeDtypeStruct((m, n), F32),
        scratch_shapes=[pltpu.VMEM((bm, bn), F32)] if nk > 1 else [],
        compiler_params=_cparams(("parallel", "parallel", "arbitrary")),
        name=name,
    )(r, x, w)


def _merge_kernel(h_ref, wg_ref, b_ref, o_ref, wb_ref, out_ref, acc_ref):
    n = pl.program_id(2)
    gate = jax.nn.sigmoid(jnp.dot(h_ref[...], wg_ref[...].astype(BF), preferred_element_type=F32) + b_ref[...])
    term = gate * jnp.dot(o_ref[...], wb_ref[...], preferred_element_type=F32)

    @pl.when(n == 0)
    def _():
        acc_ref[...] = jnp.zeros(acc_ref.shape, F32)

    total = acc_ref[...] + term
    acc_ref[...] = total
    out_ref[...] = total.astype(out_ref.dtype)


def gated_merge(h, w_bgate, b_bgate, o_all, w_branch, layer, bm=1024, bn=512):
    m, d = h.shape
    bm = min(bm, m)
    return pl.pallas_call(
        _merge_kernel,
        grid=(m // bm, d // bn, N_BRANCH),
        in_specs=[pl.BlockSpec((bm, d), lambda i, j, n: (i, 0)),
                  pl.BlockSpec((None, None, d, bn), lambda i, j, n: (layer, n, 0, j)),
                  pl.BlockSpec((None, 1, bn), lambda i, j, n: (n, 0, j)),
                  pl.BlockSpec((None, bm, BRANCH_W), lambda i, j, n: (n, i, 0)),
                  pl.BlockSpec((None, None, BRANCH_W, bn), lambda i, j, n: (layer, n, 0, j))],
        out_specs=pl.BlockSpec((bm, bn), lambda i, j, n: (i, j)),
        out_shape=jax.ShapeDtypeStruct((m, d), BF),
        scratch_shapes=[pltpu.VMEM((bm, bn), F32)],
        compiler_params=_cparams(("parallel", "parallel", "arbitrary")),
        name="gated_merge",
    )(h, w_bgate, b_bgate, o_all, w_branch)


def _swiglu_kernel(x_ref, wg_ref, wu_ref, o_ref):
    x = x_ref[...]
    g = jnp.dot(x, wg_ref[...].astype(BF), preferred_element_type=F32)
    u = jnp.dot(x, wu_ref[...].astype(BF), preferred_element_type=F32)
    o_ref[...] = (g * jax.nn.sigmoid(g) * u).astype(o_ref.dtype)


def swiglu_in(x, w, layer, bm=2048, bn=256):
    m, k = x.shape
    n = w.shape[-1] // 2
    bm = min(bm, m)
    nb = n // bn
    return pl.pallas_call(
        _swiglu_kernel,
        grid=(m // bm, nb),
        in_specs=[pl.BlockSpec((bm, k), lambda i, j: (i, 0)),
                  pl.BlockSpec((None, k, bn), lambda i, j: (layer, 0, j)),
                  pl.BlockSpec((None, k, bn), lambda i, j: (layer, 0, nb + j))],
        out_specs=pl.BlockSpec((bm, bn), lambda i, j: (i, j)),
        out_shape=jax.ShapeDtypeStruct((m, n), BF),
        compiler_params=_cparams(("parallel", "parallel")),
        name="swiglu_in",
    )(x, w, w)


def _ple_kernel(r_ref, p_ref, wp_ref, h_ref, wg_ref, o_ref):
    e = jnp.dot(p_ref[...].astype(BF), wp_ref[...], preferred_element_type=F32)
    g = jnp.dot(h_ref[...], wg_ref[...].astype(BF), preferred_element_type=F32)
    o_ref[...] = r_ref[...] + e * jax.nn.sigmoid(g)


def ple_update(r, p, w_ple, hp, w_gate, layer, bm=1024, bn=512):
    m, d = r.shape
    kp = p.shape[-1]
    bm = min(bm, m)
    return pl.pallas_call(
        _ple_kernel,
        grid=(m // bm, d // bn),
        in_specs=[pl.BlockSpec((bm, bn), lambda i, j: (i, j)),
                  pl.BlockSpec((None, bm, kp), lambda i, j: (layer, i, 0)),
                  pl.BlockSpec((None, kp, bn), lambda i, j: (layer, 0, j)),
                  pl.BlockSpec((bm, d), lambda i, j: (i, 0)),
                  pl.BlockSpec((None, d, bn), lambda i, j: (layer, 0, j))],
        out_specs=pl.BlockSpec((bm, bn), lambda i, j: (i, j)),
        out_shape=jax.ShapeDtypeStruct((m, d), F32),
        compiler_params=_cparams(("parallel", "parallel")),
        name="ple_update",
    )(r, p, w_ple, hp, w_gate)


def _static_buckets(dist):
    max_exact = N_BUCKETS // 2
    d = np.maximum(np.asarray(dist), 0)
    large = max_exact + (np.log(np.maximum(d, 1).astype(np.float32) / np.float32(max_exact))
                         / np.float32(math.log(MAX_DIST / max_exact))
                         * np.float32(N_BUCKETS - max_exact)).astype(np.int32)
    large = np.minimum(large, N_BUCKETS - 1)
    return np.where(d < max_exact, d, large).astype(np.int32)


def _toeplitz_tiles(f, t):
    hh, n = f.shape
    big = n + t - 1
    fpad = jnp.concatenate([jnp.full((hh, t - 1), NEG, f.dtype), f], axis=1)
    flat = jnp.tile(fpad, (1, t + 1))[:, :t * (big + 1)]
    w = flat.reshape(hh, t, big + 1)[:, :, :n]
    w = jnp.flip(w.reshape(hh, t, n // t, t), axis=3)
    return w.transpose(0, 2, 1, 3)


def _bias_tile_kernel(prev_ref, cur_ref, o_ref, *, t):
    row = jnp.concatenate([prev_ref[...], cur_ref[...]], axis=1)
    rolled = pltpu.roll(jnp.broadcast_to(row, (t, 2 * t)), 0, 1, stride=1, stride_axis=0)
    o_ref[...] = rolled[:, t:]


def diff_bias_tiles(rel_bias, s):
    t = min(ATT_T, s)
    nq = s // t
    f = jnp.take(rel_bias[:, :HEADS], _static_buckets(np.arange(s)), axis=0).T.astype(F32)
    fb = jnp.concatenate([jnp.full((HEADS, t), NEG, F32), f], axis=1).reshape(HEADS, nq + 1, 1, t)
    return pl.pallas_call(
        functools.partial(_bias_tile_kernel, t=t),
        grid=(HEADS, nq),
        in_specs=[pl.BlockSpec((None, None, 1, t), lambda h, d: (h, d, 0, 0)),
                  pl.BlockSpec((None, None, 1, t), lambda h, d: (h, d + 1, 0, 0))],
        out_specs=pl.BlockSpec((None, None, t, t), lambda h, d: (h, d, 0, 0)),
        out_shape=jax.ShapeDtypeStruct((HEADS, nq, t, t), F32),
        compiler_params=_cparams(("parallel", "parallel")),
        name="diff_bias_tiles",
    )(fb, fb)


def dil_bias_tiles(rel_bias):
    out = []
    for gi, (_, dil) in enumerate(DIL_GROUPS):
        lo = HEADS + gi * HEADS
        steps = np.arange(2 * DIL_STEPS)
        f = jnp.take(rel_bias[:, lo:lo + HEADS], _static_buckets(steps * dil), axis=0).T.astype(F32)
        f = jnp.where(steps[None, :] <= DIL_STEPS, f, NEG)
        tiles = _toeplitz_tiles(f, DIL_STEPS)
        general = jnp.concatenate([tiles[:, 1], tiles[:, 0]], axis=-1)
        first = jnp.concatenate([jnp.full_like(tiles[:, 1], NEG), tiles[:, 0]], axis=-1)
        out.append(jnp.stack([general, first], axis=1))
    return jnp.stack(out, axis=0)


def rope_tables(s):
    half = MLA_ROPE // 2
    inv = ROPE_THETA ** (-jnp.arange(half, dtype=F32) / half)
    ang = jnp.arange(s).astype(F32)[:, None] * inv[None, :]
    cos, sin = jnp.cos(ang), jnp.sin(ang)
    zero = jnp.zeros((s, LANE - MLA_ROPE), F32)
    return (jnp.concatenate([cos, cos, zero], axis=1), jnp.concatenate([-sin, sin, zero], axis=1))


def _rope_tile(t, c, s):
    half = MLA_ROPE // 2
    swapped = pltpu.roll(t, half, axis=1) + pltpu.roll(t, LANE - half, axis=1)
    return t * c + swapped * s


def _flash_kernel(lam_ref, qt_ref, k_ref, vt_ref, *rest, diff, t, scale, post_scale):
    if diff:
        bias_ref, g_ref, o_ref, m_sc, l_sc, acc_sc, s_sc = rest
    else:
        o_ref, m_sc, l_sc, acc_sc, s_sc = rest
    i = pl.program_id(2)
    m_sc[...] = jnp.full(m_sc.shape, NEG, F32)
    l_sc[...] = jnp.zeros(l_sc.shape, F32)
    acc_sc[...] = jnp.zeros(acc_sc.shape, F32)
    qts = [qt_ref[0], qt_ref[1]] if diff else [qt_ref[:, 0:t], qt_ref[:, t:2 * t]]

    def load_kv(j):
        start = pl.multiple_of(j * t, t)
        return k_ref[pl.ds(start, t), :], vt_ref[:, pl.ds(start, t)]

    ntiles = k_ref.shape[0] // t

    def scores_of(work):
        return [jnp.dot(kv[0], qts[c], preferred_element_type=F32) for c, kv, _, _ in work]

    def consume(work, scores):
        for (c, kv, bias, diagonal), s in zip(work, scores):
            if scale != 1.0:
                s = s * (scale * LOG2E)
            if bias is not None:
                s = bias + s
            elif diagonal:
                key_i = lax.broadcasted_iota(jnp.int32, (t, t), 0)
                qry_i = lax.broadcasted_iota(jnp.int32, (t, t), 1)
                s = jnp.where(key_i <= qry_i, s, NEG)
            m_prev = m_sc[c]
            m_new = jnp.maximum(m_prev, jnp.max(s, axis=0, keepdims=True))
            ex = jnp.exp2 if scale != 1.0 else jnp.exp
            alpha = ex(m_prev - m_new)
            p = ex(s - m_new)
            l_sc[c] = alpha * l_sc[c] + jnp.sum(p, axis=0, keepdims=True)
            m_sc[c] = m_new
            acc_sc[c] = alpha * acc_sc[c] + jnp.dot(kv[1], p.astype(BF), preferred_element_type=F32)

    def step(work):
        consume(work, scores_of(work))

    def sweep(n, work):
        gsz = FLASH_GROUP
        ng = n // gsz

        def group(jj):
            items = []
            for uu in range(gsz):
                items = items + work(gsz * jj + uu)
            return items

        def park(jj, slot):
            first = jnp.minimum(gsz * jj, ntiles - gsz)
            for uu in range(gsz):
                k_tile = load_kv(first + uu)[0]
                for c in range(2):
                    s_sc[slot, 2 * uu + c] = jnp.dot(k_tile, qts[c], preferred_element_type=F32)

        def consume_parked(jj, slot):
            items = group(jj)
            consume(items, [s_sc[slot, idx] for idx in range(len(items))])

        @pl.when(ng > 0)
        def _():
            park(0, 0)

        def body(j2, carry):
            park(2 * j2 + 1, 1)
            consume_parked(2 * j2, 0)
            park(2 * j2 + 2, 0)
            consume_parked(2 * j2 + 1, 1)
            return carry

        lax.fori_loop(0, ng // 2, body, 0)

        @pl.when(ng % 2 == 1)
        def _():
            consume_parked(ng - 1, 0)

        for rr in range(gsz - 1):
            @pl.when(n % gsz > rr)
            def _(rr=rr):
                step(work(ng * gsz + rr))

    if diff:
        def work(j):
            kv = load_kv(j)
            bias = bias_ref[i - j]
            return [(0, kv, bias, False), (1, kv, bias, False)]

        sweep(i + 1, work)
        out_t = acc_sc[0] / l_sc[0] - lam_ref[0] * (acc_sc[1] / l_sc[1])
        o = out_t.T
        ms = jnp.mean(o * o, axis=-1, keepdims=True)
        o_ref[...] = (o * lax.rsqrt(ms + EPS) * g_ref[...] * post_scale).astype(o_ref.dtype)
    else:
        def work(j):
            kv = load_kv(j)
            return [(0, kv, None, False), (1, kv, None, False)]

        sweep(2 * i, work)
        kv = load_kv(2 * i)
        step([(0, kv, None, True), (1, kv, None, False)])
        step([(1, load_kv(2 * i + 1), None, True)])
        o_ref[0:t, :] = (acc_sc[0] / l_sc[0]).T.astype(o_ref.dtype)
        o_ref[t:2 * t, :] = (acc_sc[1] / l_sc[1]).T.astype(o_ref.dtype)


def flash_attention(q, k, vt, s, *, diff, dk, scale, bias=None, lam=None, gain=None, post_scale=1.0, name):
    m = k.shape[0]
    b = m // s
    t = min(ATT_T, s)
    lam = jnp.zeros((1,), F32) if lam is None else lam.reshape(1).astype(F32)
    if diff:
        nq = s // t
        q_spec = pl.BlockSpec((2, dk, t), lambda bb, h, i: (0, h, bb * nq + i))
        o_rows = t
    else:
        nq = s // (2 * t)
        q_spec = pl.BlockSpec((dk, 2 * t), lambda bb, h, i: (h, bb * nq + i))
        o_rows = 2 * t
    in_specs = [pl.BlockSpec(memory_space=pltpu.SMEM), q_spec,
                pl.BlockSpec((s, dk), lambda bb, h, i: (bb, h)),
                pl.BlockSpec((LANE, s), lambda bb, h, i: (h, bb))]
    args = [lam, q, k, vt]
    if diff:
        in_specs += [pl.BlockSpec((None, nq, t, t), lambda bb, h, i: (h, 0, 0, 0)),
                     pl.BlockSpec((1, LANE), lambda bb, h, i: (0, 0))]
        args += [bias, gain.reshape(1, LANE).astype(F32)]
    return pl.pallas_call(
        functools.partial(_flash_kernel, diff=diff, t=t, scale=scale, post_scale=post_scale),
        grid=(b, HEADS, nq),
        in_specs=in_specs,
        out_specs=pl.BlockSpec((o_rows, LANE), lambda bb, h, i: (bb * nq + i, h)),
        out_shape=jax.ShapeDtypeStruct((m, HEADS * LANE), BF),
        scratch_shapes=[pltpu.VMEM((2, 1, t), F32), pltpu.VMEM((2, 1, t), F32), pltpu.VMEM((2, LANE, t), F32),
                        pltpu.VMEM((2, 2 * FLASH_GROUP, t, t), F32)],
        compiler_params=_cparams(("parallel", "parallel", "arbitrary")),
        name=name,
    )(*args)


def _prep_a_kernel(z_ref, qg_ref, kg_ref, qz_ref, kn_ref, vt_ref):
    lane = lax.broadcasted_iota(jnp.int32, (1, LANE), 1)
    lo = lane < DA_QK_DIM

    def norm_halves(x, g):
        x2 = x * x
        s_lo = jnp.sum(jnp.where(lo, x2, 0.0), axis=-1, keepdims=True)
        s_hi = jnp.sum(jnp.where(lo, 0.0, x2), axis=-1, keepdims=True)
        inv = lax.rsqrt(jnp.where(lo, s_lo, s_hi) * (1.0 / DA_QK_DIM) + EPS)
        return x * inv * g

    scale = DA_QK_DIM ** -0.5
    for h in range(HEADS):
        cq = slice(h * LANE, (h + 1) * LANE)
        ck = slice(HEADS * LANE + h * LANE, HEADS * LANE + (h + 1) * LANE)
        q = norm_halves(z_ref[:, cq], qg_ref[...]) * scale
        qt = q.T
        sub_lo = lax.broadcasted_iota(jnp.int32, (LANE, 1), 0) < DA_QK_DIM
        qz_ref[0, cq, :] = jnp.where(sub_lo, qt, 0.0).astype(BF)
        qz_ref[1, cq, :] = jnp.where(sub_lo, 0.0, qt).astype(BF)
        kn_ref[:, cq] = norm_halves(z_ref[:, ck], kg_ref[...]).astype(BF)
        cv = slice(2 * HEADS * LANE + h * LANE, 2 * HEADS * LANE + (h + 1) * LANE)
        vt_ref[cq, :] = z_ref[:, cv].T.astype(BF)


def prep_a(z, q_g, k_g, bm=256):
    m = z.shape[0]
    bm = min(bm, m)
    w = HEADS * LANE
    g2 = lambda g: jnp.concatenate([g, g]).reshape(1, LANE).astype(F32)
    return pl.pallas_call(
        _prep_a_kernel,
        grid=(m // bm,),
        in_specs=[pl.BlockSpec((bm, DA_COLS), lambda i: (i, Z_A // DA_COLS)),
                  pl.BlockSpec((1, LANE), lambda i: (0, 0)),
                  pl.BlockSpec((1, LANE), lambda i: (0, 0))],
        out_specs=[pl.BlockSpec((2, w, bm), lambda i: (0, 0, i)),
                   pl.BlockSpec((bm, w), lambda i: (i, 0)),
                   pl.BlockSpec((w, bm), lambda i: (0, i))],
        out_shape=[jax.ShapeDtypeStruct((2, w, m), BF), jax.ShapeDtypeStruct((m, w), BF),
                   jax.ShapeDtypeStruct((w, m), BF)],
        compiler_params=_cparams(("parallel",)),
        name="prep_a",
    )(z, g2(q_g), g2(k_g))


def mixer_a(z, s, lam, lam_init, q_g, k_g, o_g, bias_tiles):
    qz, kn, vt = prep_a(z, q_g, k_g)
    return flash_attention(qz, kn, vt, s, diff=True, dk=LANE, scale=1.0, bias=bias_tiles, lam=lam, gain=o_g,
                           post_scale=1.0 - lam_init, name="diff_attention")


def _prep_b_kernel(zq_ref, zkv_ref, zkr_ref, wq_ref, wkv_ref, cq_g, ckv_g, kr_g, qn_g, qr_g, kn_g, c_ref, s_ref,
                   qc_ref, kc_ref, vt_ref):
    def norm(x, g, width):
        ms = jnp.sum(x * x, axis=-1, keepdims=True) * (1.0 / width)
        return x * lax.rsqrt(ms + EPS) * g

    cos, sin = c_ref[...], s_ref[...]
    c_q = norm(zq_ref[...], cq_g[...], MLA_Q_RANK).astype(BF)
    c_kv = norm(zkv_ref[...], ckv_g[...], MLA_KV_RANK).astype(BF)
    q_up = jnp.dot(c_q, wq_ref[...], preferred_element_type=F32)
    kv_up = jnp.dot(c_kv, wkv_ref[...], preferred_element_type=F32)
    kr = _rope_tile(norm(zkr_ref[...], kr_g[...], MLA_ROPE), cos, sin).astype(BF)
    for h in range(HEADS):
        c0 = slice(2 * h * LANE, (2 * h + 1) * LANE)
        c1 = slice((2 * h + 1) * LANE, (2 * h + 2) * LANE)
        qc_ref[c0, :] = norm(q_up[:, c0], qn_g[...], MLA_NOPE).T.astype(BF)
        qc_ref[c1, :] = _rope_tile(norm(q_up[:, c1], qr_g[...], MLA_ROPE), cos, sin).T.astype(BF)
        kc_ref[:, c0] = norm(kv_up[:, c0], kn_g[...], MLA_NOPE).astype(BF)
        kc_ref[:, c1] = kr
        vt_ref[h * LANE:(h + 1) * LANE, :] = kv_up[:, c1].T.astype(BF)


def prep_b(z, w_uq, w_ukv, layer, cq_g, ckv_g, kr_g, qn_g, qr_g, kn_g, cos_t, sin_t, s, bm=256):
    m = z.shape[0]
    bm = min(bm, s)
    nsb = s // bm
    w2 = 2 * HEADS * LANE
    row = lambda g: g.reshape(1, -1).astype(F32)
    pad = lambda g: jnp.concatenate([g, jnp.zeros((LANE - MLA_ROPE,), g.dtype)])
    const = lambda i: (0, 0)
    return pl.pallas_call(
        _prep_b_kernel,
        grid=(m // bm,),
        in_specs=[pl.BlockSpec((bm, MLA_Q_RANK), lambda i: (i, Z_B // MLA_Q_RANK)),
                  pl.BlockSpec((bm, MLA_KV_RANK), lambda i: (i, (Z_B + MLA_Q_RANK) // MLA_KV_RANK)),
                  pl.BlockSpec((bm, LANE), lambda i: (i, (Z_B + MLA_Q_RANK + MLA_KV_RANK) // LANE)),
                  pl.BlockSpec((MLA_Q_RANK, w2), const),
                  pl.BlockSpec((None, MLA_KV_RANK, w2), lambda i: (layer, 0, 0)),
                  pl.BlockSpec((1, MLA_Q_RANK), const),
                  pl.BlockSpec((1, MLA_KV_RANK), const),
                  pl.BlockSpec((1, LANE), const),
                  pl.BlockSpec((1, LANE), const),
                  pl.BlockSpec((1, LANE), const),
                  pl.BlockSpec((1, LANE), const),
                  pl.BlockSpec((bm, LANE), lambda i: (i % nsb, 0)),
                  pl.BlockSpec((bm, LANE), lambda i: (i % nsb, 0))],
        out_specs=[pl.BlockSpec((w2, bm), lambda i: (0, i)),
                   pl.BlockSpec((bm, w2), lambda i: (i, 0)),
                   pl.BlockSpec((HEADS * LANE, bm), lambda i: (0, i))],
        out_shape=[jax.ShapeDtypeStruct((w2, m), BF), jax.ShapeDtypeStruct((m, w2), BF),
                   jax.ShapeDtypeStruct((HEADS * LANE, m), BF)],
        compiler_params=_cparams(("parallel",)),
        name="prep_b",
    )(z, z, z, w_uq, w_ukv, row(cq_g), row(ckv_g), row(pad(kr_g)), row(qn_g), row(pad(qr_g)), row(kn_g),
      cos_t, sin_t)


def pack_w_uq(w):
    w = w.reshape(MLA_Q_RANK, HEADS, MLA_NOPE + MLA_ROPE)
    w = jnp.pad(w, ((0, 0), (0, 0), (0, 2 * LANE - MLA_NOPE - MLA_ROPE)))
    return w.reshape(MLA_Q_RANK, HEADS * 2 * LANE).astype(BF)


def mixer_b(z, s, w_uq, w_ukv, cq_g, ckv_g, qn_g, kn_g, qr_g, kr_g, cos_t, sin_t, layer=0):
    qc, kc, vt = prep_b(z, pack_w_uq(w_uq), w_ukv, layer, cq_g, ckv_g, kr_g, qn_g, qr_g, kn_g, cos_t, sin_t, s)
    return flash_attention(qc, kc, vt, s, diff=False, dk=2 * LANE, scale=(MLA_NOPE + MLA_ROPE) ** -0.5,
                           name="mla_attention")


DIL_PAD = DIL_STEPS * max(d for _, d in DIL_GROUPS)


def _dil_kernel(zq_ref, zk_ref, zv_ref, bias_ref, qg_ref, kg_ref, o_ref, q_sc, k_sc, v_sc, og_sc, lse_sc, *, s):
    g = pl.program_id(2)
    ngroups = len(DIL_GROUPS)

    def norm(x, gain):
        ms = jnp.mean(x * x, axis=-1, keepdims=True)
        return x * lax.rsqrt(ms + EPS) * gain

    zeros = jnp.zeros((DIL_PAD, LANE), F32)
    k_sc[0:DIL_PAD, :] = zeros
    v_sc[0:DIL_PAD, :] = zeros
    q_sc[...] = norm(zq_ref[...], qg_ref[...])
    k_sc[DIL_PAD:, :] = norm(zk_ref[...], kg_ref[...])
    v_sc[DIL_PAD:, :] = zv_ref[...]
    scale = LANE ** -0.5

    def group(gi, dil):
        nsub = s // DIL_STEPS

        def body(tt, carry):
            c = tt % dil
            n = tt // dil
            q0 = c + dil * DIL_STEPS * n
            rows = pl.ds(q0, DIL_STEPS, stride=dil) if dil > 1 else pl.ds(q0, DIL_STEPS)
            k0 = q0 + DIL_PAD - dil * DIL_STEPS
            band = pl.ds(k0, 2 * DIL_STEPS, stride=dil) if dil > 1 else pl.ds(k0, 2 * DIL_STEPS)
            qs = q_sc[rows, :].astype(BF)
            ks = k_sc[band, :].astype(BF)
            vs = v_sc[band, :].astype(BF)
            first = jnp.where(n == 0, 1, 0)
            logits = lax.dot_general(qs, ks, (((1,), (1,)), ((), ())), preferred_element_type=F32) * scale
            logits = logits + bias_ref[first]
            mx = jnp.max(logits, axis=-1, keepdims=True)
            e = jnp.exp(logits - mx)
            den = jnp.sum(e, axis=-1, keepdims=True)
            o = jnp.dot((e / den).astype(BF), vs, preferred_element_type=F32)
            og_sc[gi, rows, :] = o
            lse_sc[gi, rows, :] = jnp.broadcast_to(mx + jnp.log(den), (DIL_STEPS, LANE))
            return carry

        lax.fori_loop(0, nsub, body, 0, unroll=16)

    for gi, (_, dil) in enumerate(DIL_GROUPS):
        pl.when(g == gi)(functools.partial(group, gi, dil))

    @pl.when(g == ngroups - 1)
    def _():
        lses = [lse_sc[gi] for gi in range(ngroups)]
        mx = functools.reduce(jnp.maximum, lses)
        ws = [jnp.exp(l - mx) for l in lses]
        tot = functools.reduce(lambda a, b2: a + b2, ws)
        acc = ws[0] * og_sc[0]
        for gi in range(1, ngroups):
            acc = acc + ws[gi] * og_sc[gi]
        o_ref[...] = (acc / tot).astype(o_ref.dtype)


def mixer_c(z, s, q_g, k_g, bias_c):
    m = z.shape[0]
    b = m // s
    ngroups = len(DIL_GROUPS)
    cb = Z_C // LANE

    def col(which):
        return lambda bb, h, g: (bb, cb + (g * 3 + which) * HEADS + h)

    row = lambda g: g.reshape(1, LANE).astype(F32)
    return pl.pallas_call(
        functools.partial(_dil_kernel, s=s),
        grid=(b, HEADS, ngroups),
        in_specs=[pl.BlockSpec((s, LANE), col(0)),
                  pl.BlockSpec((s, LANE), col(1)),
                  pl.BlockSpec((s, LANE), col(2)),
                  pl.BlockSpec((None, None, 2, DIL_STEPS, 2 * DIL_STEPS), lambda bb, h, g: (g, h, 0, 0, 0)),
                  pl.BlockSpec((1, LANE), lambda bb, h, g: (0, 0)),
                  pl.BlockSpec((1, LANE), lambda bb, h, g: (0, 0))],
        out_specs=pl.BlockSpec((s, LANE), lambda bb, h, g: (bb, h)),
        out_shape=jax.ShapeDtypeStruct((m, HEADS * LANE), BF),
        scratch_shapes=[pltpu.VMEM((s, LANE), F32), pltpu.VMEM((DIL_PAD + s, LANE), F32),
                        pltpu.VMEM((DIL_PAD + s, LANE), F32), pltpu.VMEM((ngroups, s, LANE), F32),
                        pltpu.VMEM((ngroups, s, LANE), F32)],
        compiler_params=_cparams(("parallel", "parallel", "arbitrary")),
        name="dilated_attention",
    )(z, z, z, bias_c, row(q_g), row(k_g))


CONV_PAD = 8
GDN_GROUP = 4
GDN_BASE = 8


def _gdn_kernel(par_ref, zq_ref, zk_ref, zv_ref, zg_ref, zab_ref, wq_ref, wk_ref, wv_ref, og_ref, o_ref,
                x_sc, q_sc, k_sc, v_sc, g_sc, b_sc, u_sc, w_sc, a_sc, st_sc, *, s):
    h = pl.program_id(1)
    c = GDN_CHUNK

    def conv_silu(z_ref, w_ref):
        x_sc[0:CONV_PAD, :] = jnp.zeros((CONV_PAD, LANE), F32)
        x_sc[CONV_PAD:, :] = z_ref[...]
        y = x_sc[CONV_PAD:, :] * w_ref[CONV_W - 1:CONV_W, :]
        for i in range(CONV_W - 1):
            off = CONV_PAD - (CONV_W - 1) + i
            y = y + x_sc[off:off + s, :] * w_ref[i:i + 1, :]
        return y * jax.nn.sigmoid(y)

    def l2(x):
        return x * lax.rsqrt(jnp.sum(x * x, axis=-1, keepdims=True) + EPS)

    q_sc[...] = l2(conv_silu(zq_ref, wq_ref)) * (LANE ** -0.5)
    k_sc[...] = l2(conv_silu(zk_ref, wk_ref))
    v_sc[...] = conv_silu(zv_ref, wv_ref)
    lane = lax.broadcasted_iota(jnp.int32, (1, LANE), 1)
    ab = zab_ref[...]
    a_col = jnp.sum(jnp.where(lane == h, ab, 0.0), axis=-1, keepdims=True)
    b_col = jnp.sum(jnp.where(lane == h + HEADS, ab, 0.0), axis=-1, keepdims=True)
    a_neg_exp = par_ref[0, h]
    dt_bias = par_ref[1, h]
    g_sc[...] = jnp.broadcast_to(a_neg_exp * jax.nn.softplus(a_col + dt_bias), (s, LANE))
    b_sc[...] = jnp.broadcast_to(jax.nn.sigmoid(b_col), (s, LANE))
    st_sc[...] = jnp.zeros(st_sc.shape, F32)

    gr = GDN_GROUP * c
    r_i = lax.broadcasted_iota(jnp.int32, (gr, gr), 0)
    c_i = lax.broadcasted_iota(jnp.int32, (gr, gr), 1)
    blk = {}
    size = GDN_BASE
    while size <= c:
        blk[size] = (r_i // size) == (c_i // size)
        size *= 2
    same = blk[c]
    tril = same & (r_i >= c_i)
    strict = same & (r_i > c_i)
    tril_b = tril.astype(BF)
    triu_b = (same & (r_i <= c_i)).astype(BF)
    eye = (r_i == c_i).astype(F32)

    def mm(a, b2):
        return jnp.dot(a.astype(BF), b2.astype(BF), preferred_element_type=F32)

    def mm_t(a, b2):
        return lax.dot_general(a.astype(BF), b2.astype(BF), (((1,), (1,)), ((), ())), preferred_element_type=F32)

    def split2(x):
        hi = x.astype(BF)
        return hi, (x - hi.astype(F32)).astype(BF)

    def split3(x):
        hi = x.astype(BF)
        r1 = x - hi.astype(F32)
        mid = r1.astype(BF)
        return hi, mid, (r1 - mid.astype(F32)).astype(BF)

    def mm_hi(a, b2):
        a_hi, a_lo = split2(a)
        b_hi, b_lo = split2(b2)
        d = functools.partial(jnp.dot, preferred_element_type=F32)
        return d(a_hi, b_hi) + (d(a_hi, b_lo) + d(a_lo, b_hi))

    def widen(x):
        return jnp.concatenate([x] * (gr // LANE), axis=1)

    def local_group(gi, carry):
        base = pl.multiple_of(gi * gr, gr)
        rows = pl.ds(base, gr)
        q = q_sc[rows, :]
        k = k_sc[rows, :]
        beta = b_sc[rows, :]
        parts = split3(g_sc[rows, :])
        gc = sum(jnp.dot(tril_b, part, preferred_element_type=F32) for part in parts)
        g_row = sum(lax.dot_general(widen(part), triu_b, (((0,), (0,)), ((), ())), preferred_element_type=F32)
                    for part in parts)
        decay = jnp.exp(jnp.where(tril, widen(gc) - g_row, NEG))
        kb = k * beta
        vb = v_sc[rows, :] * beta
        lower = jnp.where(strict, mm_t(kb, k) * decay, 0.0)
        neg = -jnp.where(blk[GDN_BASE], lower, 0.0)
        p1 = mm_hi(neg, neg)
        tmat = eye + neg
        tmat = tmat + mm_hi(tmat, p1)
        tmat = tmat + mm_hi(tmat, mm_hi(p1, p1))
        size = GDN_BASE
        while size < c:
            off = jnp.where(blk[2 * size] & jnp.logical_not(blk[size]), lower, 0.0)
            tmat = tmat - mm_hi(mm_hi(tmat, off), tmat)
            size *= 2
        eg = jnp.exp(gc)
        u_sc[rows, :] = mm(tmat, vb)
        w_sc[rows, :] = mm(tmat, kb * eg)
        intra = mm_t(q, k) * decay
        q_sc[rows, :] = q * eg
        for uu in range(GDN_GROUP):
            sl = slice(uu * c, (uu + 1) * c)
            crow = pl.ds(base + uu * c, c)
            g_last = gc[(uu + 1) * c - 1:(uu + 1) * c, :]
            a_sc[crow, 0:c] = intra[sl, sl]
            k_sc[crow, :] = k[sl] * jnp.exp(g_last - gc[sl])
            g_sc[crow, :] = jnp.broadcast_to(jnp.exp(g_last), (c, LANE))
        return carry


    def scan(n, carry):
        rows = pl.ds(pl.multiple_of(n * c, c), c)
        state = st_sc[...]
        v_new = u_sc[rows, :] - mm(w_sc[rows, :], state)
        o = mm(q_sc[rows, :], state) + mm(a_sc[rows, 0:c], v_new)
        decay_last = g_sc[pl.ds(pl.multiple_of(n * c, c), 1), :]
        st_sc[...] = state * decay_last + lax.dot_general(
            k_sc[rows, :].astype(BF), v_new.astype(BF), (((0,), (0,)), ((), ())), preferred_element_type=F32)
        u_sc[rows, :] = o
        return carry

    ngroups = s // gr
    local_group(0, 0)

    def pipelined(g, carry):
        local_group(g + 1, 0)
        for uu in range(GDN_GROUP):
            scan(g * GDN_GROUP + uu, 0)
        return carry

    lax.fori_loop(0, ngroups - 1, pipelined, 0)
    for uu in range(GDN_GROUP):
        scan((ngroups - 1) * GDN_GROUP + uu, 0)
    o = u_sc[...]
    ms = jnp.mean(o * o, axis=-1, keepdims=True)
    gate = zg_ref[...]
    o_ref[...] = (o * lax.rsqrt(ms + EPS) * og_ref[...] * (gate * jax.nn.sigmoid(gate))).astype(o_ref.dtype)


def mixer_d(z, s, conv_w, a_log, dt_bias, o_g):
    m = z.shape[0]
    b = m // s
    cb = Z_D // LANE
    par = jnp.stack([-jnp.exp(a_log.astype(F32)), dt_bias.astype(F32)], axis=0)
    cw = conv_w.astype(F32)

    def col(which):
        return lambda bb, h: (bb, cb + which * HEADS + h)

    return pl.pallas_call(
        functools.partial(_gdn_kernel, s=s),
        grid=(b, HEADS),
        in_specs=[pl.BlockSpec(memory_space=pltpu.SMEM),
                  pl.BlockSpec((s, LANE), col(0)),
                  pl.BlockSpec((s, LANE), col(1)),
                  pl.BlockSpec((s, LANE), col(2)),
                  pl.BlockSpec((s, LANE), col(3)),
                  pl.BlockSpec((s, LANE), lambda bb, h: (bb, cb + 4 * HEADS)),
                  pl.BlockSpec((CONV_W, LANE), lambda bb, h: (0, h)),
                  pl.BlockSpec((CONV_W, LANE), lambda bb, h: (0, HEADS + h)),
                  pl.BlockSpec((CONV_W, LANE), lambda bb, h: (0, 2 * HEADS + h)),
                  pl.BlockSpec((1, LANE), lambda bb, h: (0, 0))],
        out_specs=pl.BlockSpec((s, LANE), lambda bb, h: (bb, h)),
        out_shape=jax.ShapeDtypeStruct((m, HEADS * LANE), BF),
        scratch_shapes=[pltpu.VMEM((CONV_PAD + s, LANE), F32), pltpu.VMEM((s, LANE), F32),
                        pltpu.VMEM((s, LANE), F32), pltpu.VMEM((s, LANE), F32), pltpu.VMEM((s, LANE), F32),
                        pltpu.VMEM((s, LANE), F32), pltpu.VMEM((s, LANE), F32), pltpu.VMEM((s, LANE), F32),
                        pltpu.VMEM((s, LANE), F32), pltpu.VMEM((LANE, LANE), F32)],
        compiler_params=_cparams(("parallel", "arbitrary")),
        name="gated_deltanet",
    )(par, z, z, z, z, z, cw, cw, cw, o_g.reshape(1, LANE).astype(F32))


def _cast_kernel(x_ref, o_ref):
    o_ref[...] = x_ref[...].astype(o_ref.dtype)


def cast_bf16(w, bm=512):
    r, c = w.shape
    bm = min(bm, r)
    return pl.pallas_call(
        _cast_kernel,
        grid=(r // bm,),
        in_specs=[pl.BlockSpec((bm, c), lambda i: (i, 0))],
        out_specs=pl.BlockSpec((bm, c), lambda i: (i, 0)),
        out_shape=jax.ShapeDtypeStruct((r, c), BF),
        compiler_params=_cparams(("parallel",)),
        name="cast_bf16",
    )(w)


IN_COLS = DA_COLS + MLA_COLS + DIL_COLS + GDN_QKV + 2 * HEADS + HEADS * LANE


def _pack_w_in_kernel(w_ref, o_ref):
    rows = w_ref.shape[0]

    def put(dst, src, width):
        o_ref[:, dst:dst + width] = w_ref[:, src:src + width]

    def zero(dst, width):
        o_ref[:, dst:dst + width] = jnp.zeros((rows, width), BF)

    c0 = DA_COLS + MLA_COLS + DIL_COLS
    put(0, 0, DA_COLS + MLA_COLS)
    zero(Z_B + MLA_COLS, Z_C - Z_B - MLA_COLS)
    put(Z_C, DA_COLS + MLA_COLS, DIL_COLS)
    put(Z_D, c0, GDN_QKV)
    put(Z_D + GDN_QKV, c0 + GDN_QKV + 2 * HEADS, HEADS * LANE)
    put(Z_D + GDN_QKV + HEADS * LANE, c0 + GDN_QKV, 2 * HEADS)
    used = Z_D + GDN_QKV + HEADS * LANE + 2 * HEADS
    zero(used, Z_COLS - used)


def pack_w_in(w, bm=256):
    r = w.shape[0]
    return pl.pallas_call(
        _pack_w_in_kernel,
        grid=(r // bm,),
        in_specs=[pl.BlockSpec((bm, IN_COLS), lambda i: (i, 0))],
        out_specs=pl.BlockSpec((bm, Z_COLS), lambda i: (i, 0)),
        out_shape=jax.ShapeDtypeStruct((r, Z_COLS), BF),
        compiler_params=_cparams(("parallel",)),
        name="pack_w_in",
    )(w)


def _cast_stacked(w, bm=512):
    lead, c = w.shape[:-1], w.shape[-1]
    return cast_bf16(w.reshape(-1, c), bm).reshape(*lead, c)


def kernel(x, p, rel_bias, norm_mix_g, w_in, da_lambda, da_q_g, da_k_g, da_o_g, mla_w_uq, mla_w_ukv, mla_cq_g, mla_ckv_g, mla_qn_g, mla_kn_g, mla_qr_g, mla_kr_g, dil_q_g, dil_k_g, gdn_conv_w, gdn_a_log, gdn_dt_bias, gdn_o_g, w_bgate, b_bgate, w_branch, w_out, norm_ffn_g, w_ffn_in, w_ffn_out, norm_ple_g, w_ple, w_ple_gate):
    b, s, d = x.shape
    m = b * s
    bias_a = diff_bias_tiles(rel_bias, s)
    bias_c = dil_bias_tiles(rel_bias)
    cos_t, sin_t = rope_tables(s)
    w_in_p = pack_w_in(w_in.astype(BF).reshape(DEPTH * d, IN_COLS)).reshape(DEPTH, d, Z_COLS)
    w_branch_b = _cast_stacked(w_branch)
    w_out_b = _cast_stacked(w_out)
    w_ffn_out_b = _cast_stacked(w_ffn_out)
    w_ple_b = _cast_stacked(w_ple)
    w_ukv_b = _cast_stacked(mla_w_ukv)
    p_r = p.reshape(DEPTH, m, PLE_DIM)
    xr = x.reshape(m, d)
    for i in range(DEPTH):
        h = rmsnorm_rows(xr, norm_mix_g[i])
        z = matmul(h, w_in_p, F32, layer=i, name="in_proj")
        lam_init = 0.8 - 0.6 * math.exp(-0.3 * i)
        lq1, lk1, lq2, lk2 = (da_lambda[i, j].astype(F32) for j in range(4))
        lam = jnp.exp(jnp.sum(lq1 * lk1)) - jnp.exp(jnp.sum(lq2 * lk2)) + lam_init
        o_a = mixer_a(z, s, lam, lam_init, da_q_g[i], da_k_g[i], da_o_g[i], bias_a)
        o_b = mixer_b(z, s, mla_w_uq[i], w_ukv_b, mla_cq_g[i], mla_ckv_g[i], mla_qn_g[i], mla_kn_g[i],
                      mla_qr_g[i], mla_kr_g[i], cos_t, sin_t, layer=i)
        o_c = mixer_c(z, s, dil_q_g[i], dil_k_g[i], bias_c)
        o_d = mixer_d(z, s, gdn_conv_w[i], gdn_a_log[i], gdn_dt_bias[i], gdn_o_g[i])
        o_all = jnp.stack([o_a, o_b, o_c, o_d], axis=0)
        merged = gated_merge(h, w_bgate, b_bgate[i].reshape(N_BRANCH, 1, d), o_all, w_branch_b, i)
        xr = matmul_residual(xr, merged, w_out_b, bn=1024, layer=i, name="out_proj")
        hf = rmsnorm_rows(xr, norm_ffn_g[i])
        act = swiglu_in(hf, w_ffn_in, i)
        xr = matmul_residual(xr, act, w_ffn_out_b, bm=512, bn=512, layer=i, name="ffn_out")
        hp = rmsnorm_rows(xr, norm_ple_g[i])
        xr = ple_update(xr, p_r, w_ple_b, hp, w_ple_gate, i)
    return xr.reshape(b, s, d)
```
